```python
import jax, jax.numpy as jnp
from jax import lax
import numpy as np

D_MODEL = 2048
BATCH = 8
SEQ = 4096
DEPTH = 4

CHUNK = 64

FOX_WIDTH = D_MODEL // 2
FOX_HEAD_DIM = 128
FOX_HEADS = FOX_WIDTH // FOX_HEAD_DIM
Q_BLOCK = 128
FORGET_BIAS_MEAN = 2.0

SGU_WIDTH = D_MODEL // 2
SGU_GROUP_DIM = 128
SGU_GROUPS = SGU_WIDTH // SGU_GROUP_DIM
SGU_CHUNK = 128

D_FF = 5632
CONV_WIDTH = 3

RMS_EPS = 1e-6

IN_WIDTH = 3 * FOX_WIDTH + FOX_HEADS + 2 * SGU_WIDTH + 2 * D_MODEL

kernel_name = "hybrid_fox_sgu_convffn_trunk"


def rmsnorm(x, gain):
    xf = x.astype(jnp.float32)
    inv = lax.rsqrt(jnp.mean(xf * xf, axis=-1, keepdims=True) + RMS_EPS)
    return (xf * inv).astype(x.dtype) * gain


def split_in_proj(proj):
    sizes = (FOX_WIDTH, FOX_WIDTH, FOX_WIDTH, FOX_HEADS,
             SGU_WIDTH, SGU_WIDTH, D_MODEL, D_MODEL)
    points = tuple(int(p) for p in np.cumsum(sizes)[:-1])
    return jnp.split(proj, points, axis=-1)


def forgetting_attention(q, k, v, log_f):
    B, S, H, Dh = q.shape
    n_blk = S // Q_BLOCK
    scale = Dh ** -0.5
    c = jnp.cumsum(log_f, axis=1).transpose(0, 2, 1)
    qb = q.reshape(B, n_blk, Q_BLOCK, H, Dh).transpose(1, 0, 2, 3, 4)
    cb = c.reshape(B, H, n_blk, Q_BLOCK).transpose(2, 0, 1, 3)
    kpos = jnp.arange(S)

    def block(args):
        i, qi, ci = args
        s = jnp.einsum('bqhd,bkhd->bhqk', qi, k,
                       preferred_element_type=jnp.float32) * scale
        s = s + ci[..., :, None] - c[:, :, None, :]
        qpos = i * Q_BLOCK + jnp.arange(Q_BLOCK)
        mask = kpos[None, :] <= qpos[:, None]
        s = jnp.where(mask, s, -1e30)
        p = jax.nn.softmax(s, axis=-1).astype(v.dtype)
        return jnp.einsum('bhqk,bkhd->bqhd', p, v)

    o = lax.map(block, (jnp.arange(n_blk), qb, cb))
    return o.transpose(1, 0, 2, 3, 4).reshape(B, S, H * Dh)


def spatial_gating(u, v, g_norm, w_s, b_s):
    B, S, _ = v.shape
    n = S // SGU_CHUNK
    vc = rmsnorm(v, g_norm).reshape(B, n, SGU_CHUNK, SGU_GROUPS, SGU_GROUP_DIM)
    w = jnp.tril(w_s)
    mixed = jnp.einsum('gts,bnsgc->bntgc', w, vc) + b_s.T[None, None, :, :, None]
    return u * mixed.reshape(B, S, SGU_WIDTH)


def conv_ffn(h, w_up, conv_w, conv_b, w_down):
    S = h.shape[1]
    a, b = jnp.split(h @ w_up, 2, axis=-1)
    a_pad = jnp.pad(a, ((0, 0), (CONV_WIDTH - 1, 0), (0, 0)))
    acc = conv_b + conv_w[0] * a_pad[:, 0:S]
    for tap in range(1, CONV_WIDTH):
        acc = acc + conv_w[tap] * a_pad[:, tap:tap + S]
    return (jax.nn.gelu(acc) * b) @ w_down


def _fwd_setup_inputs(seed: int = 0) -> dict:
    key = jax.random.key(seed)
    ks = jax.random.split(key, 16)
    f32 = jnp.float32

    def nrm(k, shape, fan_in):
        return jax.random.normal(k, shape, f32) * (fan_in ** -0.5)

    def gain(k, shape):
        return 1.0 + 0.01 * jax.random.normal(k, shape, f32)

    return {
        "x": jax.random.normal(ks[0], (BATCH, SEQ, D_MODEL), f32),
        "g_mix": gain(ks[1], (DEPTH, D_MODEL)),
        "w_in": nrm(ks[2], (DEPTH, D_MODEL, IN_WIDTH), D_MODEL),
        "b_forget": FORGET_BIAS_MEAN + 0.5 * jax.random.normal(ks[3], (DEPTH, FOX_HEADS), f32),
        "g_sgu": gain(ks[4], (DEPTH, SGU_WIDTH)),
        "w_spatial": nrm(ks[5], (DEPTH, SGU_GROUPS, SGU_CHUNK, SGU_CHUNK), SGU_CHUNK),
        "b_spatial": 1.0 + 0.1 * jax.random.normal(ks[6], (DEPTH, SGU_GROUPS, SGU_CHUNK), f32),
        "w_branch_a": nrm(ks[7], (DEPTH, FOX_WIDTH, D_MODEL), FOX_WIDTH),
        "w_branch_b": nrm(ks[8], (DEPTH, SGU_WIDTH, D_MODEL), SGU_WIDTH),
        "w_out": nrm(ks[9], (DEPTH, D_MODEL, D_MODEL), D_MODEL),
        "g_ffn": gain(ks[10], (DEPTH, D_MODEL)),
        "w_up": nrm(ks[11], (DEPTH, D_MODEL, 2 * D_FF), D_MODEL),
        "conv_w": nrm(ks[12], (DEPTH, CONV_WIDTH, D_FF), CONV_WIDTH),
        "conv_b": 0.01 * jax.random.normal(ks[13], (DEPTH, D_FF), f32),
        "w_down": nrm(ks[14], (DEPTH, D_FF, D_MODEL), D_FF),
        "g_final": gain(ks[15], (D_MODEL,)),
    }


def _fwd_reference(x, g_mix, w_in, b_forget, g_sgu, w_spatial, b_spatial,
              w_branch_a, w_branch_b, w_out, g_ffn, w_up, conv_w, conv_b,
              w_down, g_final):
    B, S, _ = x.shape
    for l in range(DEPTH):
        h = rmsnorm(x, g_mix[l])
        q, k, v, f_logit, u, vg, gate_a, gate_b = split_in_proj(h @ w_in[l])
        log_f = jax.nn.log_sigmoid((f_logit + b_forget[l]).astype(jnp.float32))
        y_a = forgetting_attention(
            q.reshape(B, S, FOX_HEADS, FOX_HEAD_DIM),
            k.reshape(B, S, FOX_HEADS, FOX_HEAD_DIM),
            v.reshape(B, S, FOX_HEADS, FOX_HEAD_DIM),
            log_f)
        y_b = spatial_gating(jax.nn.gelu(u), jax.nn.gelu(vg),
                             g_sgu[l], w_spatial[l], b_spatial[l])
        merged = (jax.nn.sigmoid(gate_a) * (y_a @ w_branch_a[l])
                  + jax.nn.sigmoid(gate_b) * (y_b @ w_branch_b[l]))
        x = x + merged @ w_out[l]
        x = x + conv_ffn(rmsnorm(x, g_ffn[l]), w_up[l], conv_w[l], conv_b[l], w_down[l])
    return rmsnorm(x, g_final)


import jax as _jax
import jax.numpy as _jnp

TWIN_FORMAT = 'train_step'
FWD_PARAMS = ['x', 'g_mix', 'w_in', 'b_forget', 'g_sgu', 'w_spatial', 'b_spatial', 'w_branch_a', 'w_branch_b', 'w_out', 'g_ffn', 'w_up', 'conv_w', 'conv_b', 'w_down', 'g_final']
TWIN_WEIGHTS = ['g_mix', 'w_in', 'b_forget', 'g_sgu', 'w_spatial', 'b_spatial', 'w_branch_a', 'w_branch_b', 'w_out', 'g_ffn', 'w_up', 'conv_w', 'conv_b', 'w_down', 'g_final']
TWIN_DIFF_INPUT = 'x'
TWIN_INPUTS = ['x', 'g_mix', 'w_in', 'b_forget', 'g_sgu', 'w_spatial', 'b_spatial', 'w_branch_a', 'w_branch_b', 'w_out', 'g_ffn', 'w_up', 'conv_w', 'conv_b', 'w_down', 'g_final', 'loss_target', 'm_g_mix', 'm_w_in', 'm_b_forget', 'm_g_sgu', 'm_w_spatial', 'm_b_spatial', 'm_w_branch_a', 'm_w_branch_b', 'm_w_out', 'm_g_ffn', 'm_w_up', 'm_conv_w', 'm_conv_b', 'm_w_down', 'm_g_final', 'v_g_mix', 'v_w_in', 'v_b_forget', 'v_g_sgu', 'v_w_spatial', 'v_b_spatial', 'v_w_branch_a', 'v_w_branch_b', 'v_w_out', 'v_g_ffn', 'v_w_up', 'v_conv_w', 'v_conv_b', 'v_w_down', 'v_g_final']
TWIN_OUTPUTS = ['loss', 'grad_x', 'grad_g_mix', 'grad_w_in', 'grad_b_forget', 'grad_g_sgu', 'grad_w_spatial', 'grad_b_spatial', 'grad_w_branch_a', 'grad_w_branch_b', 'grad_w_out', 'grad_g_ffn', 'grad_w_up', 'grad_conv_w', 'grad_conv_b', 'grad_w_down', 'grad_g_final', 'delta_g_mix', 'delta_w_in', 'delta_b_forget', 'delta_g_sgu', 'delta_w_spatial', 'delta_b_spatial', 'delta_w_branch_a', 'delta_w_branch_b', 'delta_w_out', 'delta_g_ffn', 'delta_w_up', 'delta_conv_w', 'delta_conv_b', 'delta_w_down', 'delta_g_final', 'new_m_g_mix', 'new_m_w_in', 'new_m_b_forget', 'new_m_g_sgu', 'new_m_w_spatial', 'new_m_b_spatial', 'new_m_w_branch_a', 'new_m_w_branch_b', 'new_m_w_out', 'new_m_g_ffn', 'new_m_w_up', 'new_m_conv_w', 'new_m_conv_b', 'new_m_w_down', 'new_m_g_final', 'new_v_g_mix', 'new_v_w_in', 'new_v_b_forget', 'new_v_g_sgu', 'new_v_w_spatial', 'new_v_b_spatial', 'new_v_w_branch_a', 'new_v_w_branch_b', 'new_v_w_out', 'new_v_g_ffn', 'new_v_w_up', 'new_v_conv_w', 'new_v_conv_b', 'new_v_w_down', 'new_v_g_final']
TWIN_LEAF_KINDS = {'loss': 'loss', 'grad_x': 'grad_x', 'grad_g_mix': 'grad_w', 'grad_w_in': 'grad_w', 'grad_b_forget': 'grad_w', 'grad_g_sgu': 'grad_w', 'grad_w_spatial': 'grad_w', 'grad_b_spatial': 'grad_w', 'grad_w_branch_a': 'grad_w', 'grad_w_branch_b': 'grad_w', 'grad_w_out': 'grad_w', 'grad_g_ffn': 'grad_w', 'grad_w_up': 'grad_w', 'grad_conv_w': 'grad_w', 'grad_conv_b': 'grad_w', 'grad_w_down': 'grad_w', 'grad_g_final': 'grad_w', 'delta_g_mix': 'delta_w', 'delta_w_in': 'delta_w', 'delta_b_forget': 'delta_w', 'delta_g_sgu': 'delta_w', 'delta_w_spatial': 'delta_w', 'delta_b_spatial': 'delta_w', 'delta_w_branch_a': 'delta_w', 'delta_w_branch_b': 'delta_w', 'delta_w_out': 'delta_w', 'delta_g_ffn': 'delta_w', 'delta_w_up': 'delta_w', 'delta_conv_w': 'delta_w', 'delta_conv_b': 'delta_w', 'delta_w_down': 'delta_w', 'delta_g_final': 'delta_w', 'new_m_g_mix': 'new_m', 'new_m_w_in': 'new_m', 'new_m_b_forget': 'new_m', 'new_m_g_sgu': 'new_m', 'new_m_w_spatial': 'new_m', 'new_m_b_spatial': 'new_m', 'new_m_w_branch_a': 'new_m', 'new_m_w_branch_b': 'new_m', 'new_m_w_out': 'new_m', 'new_m_g_ffn': 'new_m', 'new_m_w_up': 'new_m', 'new_m_conv_w': 'new_m', 'new_m_conv_b': 'new_m', 'new_m_w_down': 'new_m', 'new_m_g_final': 'new_m', 'new_v_g_mix': 'new_v', 'new_v_w_in': 'new_v', 'new_v_b_forget': 'new_v', 'new_v_g_sgu': 'new_v', 'new_v_w_spatial': 'new_v', 'new_v_b_spatial': 'new_v', 'new_v_w_branch_a': 'new_v', 'new_v_w_branch_b': 'new_v', 'new_v_w_out': 'new_v', 'new_v_g_ffn': 'new_v', 'new_v_w_up': 'new_v', 'new_v_conv_w': 'new_v', 'new_v_conv_b': 'new_v', 'new_v_w_down': 'new_v', 'new_v_g_final': 'new_v'}


def _forward(args):
    return _fwd_reference(*[args[k] for k in FWD_PARAMS])


def _output_shape():
    def fwd():
        inp = _fwd_setup_inputs(0)
        return _fwd_reference(*[inp[k] for k in FWD_PARAMS])
    out = _jax.eval_shape(fwd)
    return out.shape, out.dtype

N_MICROBATCH = 1
ADAM_LR = 0.001
ADAM_B1 = 0.9
ADAM_B2 = 0.999
ADAM_EPS = 1e-08
ADAM_WD = 0.01
ADAM_STEP = 10
PER_EXAMPLE_BATCH_AXIS = {'x': 0, 'loss_target': 0}
SHARED_INPUTS = []
_WEIGHT_DTYPES = {'g_mix': _jnp.float32, 'w_in': _jnp.float32, 'b_forget': _jnp.float32, 'g_sgu': _jnp.float32, 'w_spatial': _jnp.float32, 'b_spatial': _jnp.float32, 'w_branch_a': _jnp.float32, 'w_branch_b': _jnp.float32, 'w_out': _jnp.float32, 'g_ffn': _jnp.float32, 'w_up': _jnp.float32, 'conv_w': _jnp.float32, 'conv_b': _jnp.float32, 'w_down': _jnp.float32, 'g_final': _jnp.float32}
MOMENT_SCALE = {'g_mix': 5.651546e-02, 'w_in': 2.600987e-02, 'b_forget': 1.653122e-01, 'g_sgu': 2.947758e-02, 'w_spatial': 2.996392e-02, 'b_spatial': 4.349196e-02, 'w_branch_a': 2.069257e-02, 'w_branch_b': 3.708290e-02, 'w_out': 4.216517e-02, 'g_ffn': 6.519248e-02, 'w_up': 2.798746e-02, 'conv_w': 2.847144e-02, 'conv_b': 2.725357e-02, 'w_down': 4.570494e-02, 'g_final': 1.600102e+01}


def _to_microbatches(a, axis):
    t = _jnp.moveaxis(a, axis, 0)
    t = t.reshape((N_MICROBATCH, t.shape[0] // N_MICROBATCH) + t.shape[1:])
    return _jnp.moveaxis(t, 1, axis + 1)


def setup_inputs(seed: int = 0) -> dict:
    inp = _fwd_setup_inputs(seed)
    key = _jax.random.fold_in(_jax.random.key(seed), 7919)
    shape, _ = _output_shape()
    out = dict(inp)
    out["loss_target"] = _jax.random.normal(_jax.random.fold_in(key, 0), shape, _jnp.float32)
    for i, name in enumerate(TWIN_WEIGHTS):
        w = inp[name].astype(_jnp.float32)
        if MOMENT_SCALE is None:
            s = _jnp.sqrt(_jnp.mean(_jnp.square(w)) + 1e-30)
        else:
            s = MOMENT_SCALE[name]
        km, kv = _jax.random.split(_jax.random.fold_in(key, i + 1))
        out[name] = w
        out["m_" + name] = s * _jax.random.normal(km, w.shape, _jnp.float32)
        out["v_" + name] = (s * s) * _jax.random.uniform(kv, w.shape, _jnp.float32, 0.5, 1.5)
    if N_MICROBATCH > 1:
        for name, axis in PER_EXAMPLE_BATCH_AXIS.items():
            out[name] = _to_microbatches(out[name], axis)
    return {'x': out['x'], 'g_mix': out['g_mix'], 'w_in': out['w_in'], 'b_forget': out['b_forget'], 'g_sgu': out['g_sgu'], 'w_spatial': out['w_spatial'], 'b_spatial': out['b_spatial'], 'w_branch_a': out['w_branch_a'], 'w_branch_b': out['w_branch_b'], 'w_out': out['w_out'], 'g_ffn': out['g_ffn'], 'w_up': out['w_up'], 'conv_w': out['conv_w'], 'conv_b': out['conv_b'], 'w_down': out['w_down'], 'g_final': out['g_final'], 'loss_target': out['loss_target'], 'm_g_mix': out['m_g_mix'], 'm_w_in': out['m_w_in'], 'm_b_forget': out['m_b_forget'], 'm_g_sgu': out['m_g_sgu'], 'm_w_spatial': out['m_w_spatial'], 'm_b_spatial': out['m_b_spatial'], 'm_w_branch_a': out['m_w_branch_a'], 'm_w_branch_b': out['m_w_branch_b'], 'm_w_out': out['m_w_out'], 'm_g_ffn': out['m_g_ffn'], 'm_w_up': out['m_w_up'], 'm_conv_w': out['m_conv_w'], 'm_conv_b': out['m_conv_b'], 'm_w_down': out['m_w_down'], 'm_g_final': out['m_g_final'], 'v_g_mix': out['v_g_mix'], 'v_w_in': out['v_w_in'], 'v_b_forget': out['v_b_forget'], 'v_g_sgu': out['v_g_sgu'], 'v_w_spatial': out['v_w_spatial'], 'v_b_spatial': out['v_b_spatial'], 'v_w_branch_a': out['v_w_branch_a'], 'v_w_branch_b': out['v_w_branch_b'], 'v_w_out': out['v_w_out'], 'v_g_ffn': out['v_g_ffn'], 'v_w_up': out['v_w_up'], 'v_conv_w': out['v_conv_w'], 'v_conv_b': out['v_conv_b'], 'v_w_down': out['v_w_down'], 'v_g_final': out['v_g_final']}


def _loss(weights, diff, rest, loss_target):
    with _jax.named_scope("forward"):
        args = {**rest, TWIN_DIFF_INPUT: diff, **{k: w.astype(_WEIGHT_DTYPES[k]) for k, w in weights.items()}}
        y = _forward(args)
    with _jax.named_scope("loss_head"):
        err = _jnp.square(y.astype(_jnp.float32) - loss_target)
        return 0.5 * _jnp.sum(_jnp.mean(err, axis=-1)) if err.ndim else 0.5 * err


def _adamw(w, g, m, v):
    m = ADAM_B1 * m + (1.0 - ADAM_B1) * g
    v = ADAM_B2 * v + (1.0 - ADAM_B2) * _jnp.square(g)
    m_hat = m / (1.0 - ADAM_B1 ** ADAM_STEP)
    v_hat = v / (1.0 - ADAM_B2 ** ADAM_STEP)
    delta = -ADAM_LR * (m_hat / (_jnp.sqrt(v_hat) + ADAM_EPS) + ADAM_WD * w)
    return delta, m, v


def reference(x, g_mix, w_in, b_forget, g_sgu, w_spatial, b_spatial, w_branch_a, w_branch_b, w_out, g_ffn, w_up, conv_w, conv_b, w_down, g_final, loss_target, m_g_mix, m_w_in, m_b_forget, m_g_sgu, m_w_spatial, m_b_spatial, m_w_branch_a, m_w_branch_b, m_w_out, m_g_ffn, m_w_up, m_conv_w, m_conv_b, m_w_down, m_g_final, v_g_mix, v_w_in, v_b_forget, v_g_sgu, v_w_spatial, v_b_spatial, v_w_branch_a, v_w_branch_b, v_w_out, v_g_ffn, v_w_up, v_conv_w, v_conv_b, v_w_down, v_g_final):
    given = dict(x=x, g_mix=g_mix, w_in=w_in, b_forget=b_forget, g_sgu=g_sgu, w_spatial=w_spatial, b_spatial=b_spatial, w_branch_a=w_branch_a, w_branch_b=w_branch_b, w_out=w_out, g_ffn=g_ffn, w_up=w_up, conv_w=conv_w, conv_b=conv_b, w_down=w_down, g_final=g_final, loss_target=loss_target, m_g_mix=m_g_mix, m_w_in=m_w_in, m_b_forget=m_b_forget, m_g_sgu=m_g_sgu, m_w_spatial=m_w_spatial, m_b_spatial=m_b_spatial, m_w_branch_a=m_w_branch_a, m_w_branch_b=m_w_branch_b, m_w_out=m_w_out, m_g_ffn=m_g_ffn, m_w_up=m_w_up, m_conv_w=m_conv_w, m_conv_b=m_conv_b, m_w_down=m_w_down, m_g_final=m_g_final, v_g_mix=v_g_mix, v_w_in=v_w_in, v_b_forget=v_b_forget, v_g_sgu=v_g_sgu, v_w_spatial=v_w_spatial, v_b_spatial=v_b_spatial, v_w_branch_a=v_w_branch_a, v_w_branch_b=v_w_branch_b, v_w_out=v_w_out, v_g_ffn=v_g_ffn, v_w_up=v_w_up, v_conv_w=v_conv_w, v_conv_b=v_conv_b, v_w_down=v_w_down, v_g_final=v_g_final)
    weights = {n: given[n] for n in TWIN_WEIGHTS}
    shared = {n: given[n] for n in SHARED_INPUTS}
    per_example = {n: given[n] for n in ['x']}
    grad_fn = _jax.value_and_grad(_loss, argnums=(0, 1))

    def one_microbatch(ex, loss_target):
        ex = dict(ex)
        diff = ex.pop(TWIN_DIFF_INPUT)
        return grad_fn(weights, diff, {**shared, **ex}, loss_target)

    if N_MICROBATCH == 1:
        loss, (grad_w, grad_x) = one_microbatch(per_example, given["loss_target"])
    else:
        def body(carry, xs):
            loss_sum, grad_sum = carry
            l_k, (gw_k, gx_k) = one_microbatch(xs[0], xs[1])
            with _jax.named_scope("update"):
                return (loss_sum + l_k, _jax.tree.map(_jnp.add, grad_sum, gw_k)), gx_k

        init = (_jnp.zeros((), _jnp.float32), _jax.tree.map(_jnp.zeros_like, weights))
        (loss, grad_w), grad_x = _jax.lax.scan(body, init, (per_example, given["loss_target"]))
    with _jax.named_scope("update"):
        delta_w, new_m, new_v = {}, {}, {}
        for n in TWIN_WEIGHTS:
            delta_w[n], new_m[n], new_v[n] = _adamw(weights[n], grad_w[n], given["m_" + n], given["v_" + n])
    return (loss, grad_x, *[grad_w[n] for n in TWIN_WEIGHTS], *[delta_w[n] for n in TWIN_WEIGHTS],
            *[new_m[n] for n in TWIN_WEIGHTS], *[new_v[n] for n in TWIN_WEIGHTS])
```

```python
import functools

import jax
import jax.numpy as jnp
from jax import lax
from jax.experimental import pallas as pl
from jax.experimental.pallas import tpu as pltpu

F32 = jnp.float32
BF16 = jnp.bfloat16

RMS_EPS = 1e-6
HEAD_DIM = 128
CONV_WIDTH = 3
ADAM_LR = 0.001
ADAM_B1 = 0.9
ADAM_B2 = 0.999
ADAM_EPS = 1e-08
ADAM_WD = 0.01
ADAM_STEP = 10
N_DEV = 8
MESH_AXES = ("x", "y", "c")
V7X_VMEM_LIMIT = 56 * 1024 * 1024
NEG = -1e30
ANY = pl.BlockSpec(memory_space=pl.ANY)
MESH = pl.DeviceIdType.MESH


def _tile(dim, pref):
    for t in (2048, 1024, 512, 256, 128):
        if t <= pref and dim % t == 0:
            return t
    return dim


def _params(sem):
    return pltpu.CompilerParams(dimension_semantics=sem, vmem_limit_bytes=V7X_VMEM_LIMIT)


def _gelu(x):
    t = jnp.tanh(0.7978845608028654 * (x + 0.044715 * (x * x * x)))
    return x * (0.5 * (1.0 + t))


def _gelu_and_grad(x):
    x2 = x * x
    t = jnp.tanh(0.7978845608028654 * (x + 0.044715 * (x2 * x)))
    cdf = 0.5 * (1.0 + t)
    dt = (1.0 - t * t) * (0.7978845608028654 * (1.0 + 3.0 * 0.044715 * x2))
    return x * cdf, cdf + 0.5 * x * dt


def _sigmoid(x):
    return 1.0 / (1.0 + jnp.exp(-x))


def _mm(a, b, *, ta=False, tb=False, out_dtype=BF16, res=None, name):
    m, k = (a.shape[1], a.shape[0]) if ta else a.shape
    n = b.shape[0] if tb else b.shape[1]
    assert (b.shape[1] if tb else b.shape[0]) == k
    tm, tn, tk = _tile(m, 1024), _tile(n, 1024), _tile(k, 1024)
    nk = k // tk
    dn = (((0,) if ta else (1,), (1,) if tb else (0,)), ((), ()))

    def body(*refs):
        a_ref, b_ref = refs[0], refs[1]
        r_ref = refs[2] if res is not None else None
        o_ref = refs[3] if res is not None else refs[2]
        part = lax.dot_general(a_ref[...], b_ref[...], dn, preferred_element_type=F32)

        def finish(r):
            if r_ref is not None:
                r = r + r_ref[...]
            o_ref[...] = r.astype(out_dtype)

        if nk == 1:
            finish(part)
        else:
            acc_ref = refs[-1]
            kk = pl.program_id(2)

            @pl.when(kk == 0)
            def _():
                acc_ref[...] = part

            @pl.when(kk > 0)
            def _():
                acc_ref[...] += part

            @pl.when(kk == nk - 1)
            def _():
                finish(acc_ref[...])

    a_spec = pl.BlockSpec((tk, tm), lambda i, j, kk: (kk, i)) if ta else pl.BlockSpec((tm, tk), lambda i, j, kk: (i, kk))
    b_spec = pl.BlockSpec((tn, tk), lambda i, j, kk: (j, kk)) if tb else pl.BlockSpec((tk, tn), lambda i, j, kk: (kk, j))
    o_spec = pl.BlockSpec((tm, tn), lambda i, j, kk: (i, j))
    in_specs = [a_spec, b_spec] + ([o_spec] if res is not None else [])
    args = (a, b) + ((res,) if res is not None else ())
    return pl.pallas_call(
        body, name=name, grid=(m // tm, n // tn, nk), in_specs=in_specs, out_specs=o_spec,
        out_shape=jax.ShapeDtypeStruct((m, n), out_dtype),
        scratch_shapes=[pltpu.VMEM((tm, tn), F32)] if nk > 1 else [],
        compiler_params=_params(("parallel", "parallel", "arbitrary")),
    )(*args)


def _rms_fwd(x, g, *, name):
    t, d = x.shape
    tr = _tile(t, 256)

    def body(x_ref, g_ref, h_ref):
        xf = x_ref[...]
        inv = lax.rsqrt(jnp.mean(xf * xf, axis=-1, keepdims=True) + RMS_EPS)
        h_ref[...] = ((xf * inv) * g_ref[...]).astype(BF16)

    return pl.pallas_call(
        body, name=name, grid=(t // tr,),
        in_specs=[pl.BlockSpec((tr, d), lambda i: (i, 0)), pl.BlockSpec((1, d), lambda i: (0, 0))],
        out_specs=pl.BlockSpec((tr, d), lambda i: (i, 0)),
        out_shape=jax.ShapeDtypeStruct((t, d), BF16),
        compiler_params=_params(("parallel",)),
    )(x, g)


def _rms_bwd(x, g, dh, dres, *, name):
    t, d = x.shape
    tr = _tile(t, 256)

    def body(x_ref, g_ref, dh_ref, dres_ref, dx_ref, dxb_ref, dg_ref):
        xf = x_ref[...]
        inv = lax.rsqrt(jnp.mean(xf * xf, axis=-1, keepdims=True) + RMS_EPS)
        xn = xf * inv
        dh_f = dh_ref[...].astype(F32)
        dxn = dh_f * g_ref[...]
        dx = dres_ref[...] + inv * (dxn - xn * jnp.mean(dxn * xn, axis=-1, keepdims=True))
        dx_ref[...] = dx
        dxb_ref[...] = dx.astype(BF16)
        part = jnp.sum(dh_f * xn, axis=0, keepdims=True)

        @pl.when(pl.program_id(0) == 0)
        def _():
            dg_ref[...] = part

        @pl.when(pl.program_id(0) > 0)
        def _():
            dg_ref[...] += part

    row = pl.BlockSpec((tr, d), lambda i: (i, 0))
    vec = pl.BlockSpec((1, d), lambda i: (0, 0))
    return pl.pallas_call(
        body, name=name, grid=(t // tr,), in_specs=[row, vec, row, row], out_specs=[row, row, vec],
        out_shape=[jax.ShapeDtypeStruct((t, d), F32), jax.ShapeDtypeStruct((t, d), BF16),
                   jax.ShapeDtypeStruct((1, d), F32)],
        compiler_params=_params(("arbitrary",)),
    )(x, g, dh, dres)


def _final_loss(x, g, target, *, name):
    t, d = x.shape
    tr = _tile(t, 256)

    def body(x_ref, g_ref, tg_ref, loss_ref, dx_ref, dxb_ref, dg_ref):
        xf = x_ref[...]
        gv = g_ref[...]
        inv = lax.rsqrt(jnp.mean(xf * xf, axis=-1, keepdims=True) + RMS_EPS)
        xn = xf * inv
        err = xn * gv - tg_ref[...]
        lpart = 0.5 * jnp.sum(jnp.mean(err * err, axis=-1, keepdims=True), axis=0, keepdims=True)
        dy = err * (1.0 / d)
        dxn = dy * gv
        dx = inv * (dxn - xn * jnp.mean(dxn * xn, axis=-1, keepdims=True))
        dx_ref[...] = dx
        dxb_ref[...] = dx.astype(BF16)
        gpart = jnp.sum(dy * xn, axis=0, keepdims=True)
        lrow = jnp.broadcast_to(lpart, (1, 128))

        @pl.when(pl.program_id(0) == 0)
        def _():
            dg_ref[...] = gpart
            loss_ref[...] = lrow

        @pl.when(pl.program_id(0) > 0)
        def _():
            dg_ref[...] += gpart
            loss_ref[...] += lrow

    row = pl.BlockSpec((tr, d), lambda i: (i, 0))
    vec = pl.BlockSpec((1, d), lambda i: (0, 0))
    lspec = pl.BlockSpec((1, 128), lambda i: (0, 0))
    return pl.pallas_call(
        body, name=name, grid=(t // tr,), in_specs=[row, vec, row], out_specs=[lspec, row, row, vec],
        out_shape=[jax.ShapeDtypeStruct((1, 128), F32), jax.ShapeDtypeStruct((t, d), F32),
                   jax.ShapeDtypeStruct((t, d), BF16), jax.ShapeDtypeStruct((1, d), F32)],
        compiler_params=_params(("arbitrary",)),
    )(x, g, target)


def _tri_ones(n, upper):
    r = lax.broadcasted_iota(jnp.int32, (n, n), 0)
    c = lax.broadcasted_iota(jnp.int32, (n, n), 1)
    return jnp.where((r <= c) if upper else (r >= c), 1.0, 0.0).astype(F32)


def _forget_fwd(flog, bpad, *, name):
    t = flog.shape[0]
    tb = _tile(t, 512)

    def body(f_ref, b_ref, c_ref, carry):
        z = f_ref[...] + b_ref[...]
        lf = jnp.minimum(z, 0.0) - jnp.log(1.0 + jnp.exp(-jnp.abs(z)))
        lft = lf.T
        tri = _tri_ones(tb, upper=True)

        @pl.when(pl.program_id(0) == 0)
        def _():
            carry[...] = jnp.zeros_like(carry)

        cs = jnp.dot(lft, tri, preferred_element_type=F32, precision=lax.Precision.HIGHEST) + carry[:, 0:1]
        c_ref[...] = cs
        carry[...] = jnp.broadcast_to(cs[:, tb - 1:tb], carry.shape)

    return pl.pallas_call(
        body, name=name, grid=(t // tb,),
        in_specs=[pl.BlockSpec((tb, 128), lambda i: (i, 0)), pl.BlockSpec((1, 128), lambda i: (0, 0))],
        out_specs=pl.BlockSpec((128, tb), lambda i: (0, i)),
        out_shape=jax.ShapeDtypeStruct((128, t), F32),
        scratch_shapes=[pltpu.VMEM((128, 128), F32)],
        compiler_params=_params(("arbitrary",)),
    )(flog, bpad)


def _forget_bwd(flog, bpad, dct, *, name):
    t = flog.shape[0]
    tb = _tile(t, 512)
    nb = t // tb

    def body(f_ref, b_ref, dc_ref, df_ref, db_ref, carry):
        i = pl.program_id(0)

        @pl.when(i == 0)
        def _():
            carry[...] = jnp.zeros_like(carry)

        tri = _tri_ones(tb, upper=False)
        dl = jnp.dot(dc_ref[...], tri, preferred_element_type=F32, precision=lax.Precision.HIGHEST) + carry[:, 0:1]
        carry[...] = jnp.broadcast_to(dl[:, 0:1], carry.shape)
        z = f_ref[...] + b_ref[...]
        df = dl.T * _sigmoid(-z)
        df_ref[...] = df.astype(BF16)
        part = jnp.sum(df, axis=0, keepdims=True)

        @pl.when(i == 0)
        def _():
            db_ref[...] = part

        @pl.when(i > 0)
        def _():
            db_ref[...] += part

    rev = lambda i: (nb - 1 - i, 0)
    return pl.pallas_call(
        body, name=name, grid=(nb,),
        in_specs=[pl.BlockSpec((tb, 128), rev), pl.BlockSpec((1, 128), lambda i: (0, 0)),
                  pl.BlockSpec((128, tb), lambda i: (0, nb - 1 - i))],
        out_specs=[pl.BlockSpec((tb, 128), rev), pl.BlockSpec((1, 128), lambda i: (0, 0))],
        out_shape=[jax.ShapeDtypeStruct((t, 128), BF16), jax.ShapeDtypeStruct((1, 128), F32)],
        scratch_shapes=[pltpu.VMEM((128, 128), F32)],
        compiler_params=_params(("arbitrary",)),
    )(flog, bpad, dct)


def _fox_fwd(qkv, c3, heads, *, name):
    t = qkv.shape[0]
    tq = _tile(t, 512)
    nq = t // tq
    scale = HEAD_DIM ** -0.5

    def body(q_ref, k_ref, v_ref, cq_ref, ck_ref, o_ref, o32_ref, lse_ref, m_s, l_s, acc_s):
        i, j = pl.program_id(1), pl.program_id(2)

        @pl.when(j == 0)
        def _():
            m_s[...] = jnp.full_like(m_s, NEG)
            l_s[...] = jnp.zeros_like(l_s)
            acc_s[...] = jnp.zeros_like(acc_s)

        @pl.when(j <= i)
        def _():
            s = lax.dot_general(q_ref[...], k_ref[...], (((1,), (1,)), ((), ())), preferred_element_type=F32)
            s = s * scale + (cq_ref[:, 0:1] - ck_ref[...])
            row = lax.broadcasted_iota(jnp.int32, (tq, tq), 0) + i * tq
            col = lax.broadcasted_iota(jnp.int32, (tq, tq), 1) + j * tq
            s = jnp.where(col <= row, s, NEG)
            m_prev = m_s[:, 0:1]
            m_new = jnp.maximum(m_prev, jnp.max(s, axis=1, keepdims=True))
            alpha = jnp.exp(m_prev - m_new)
            p = jnp.exp(s - m_new)
            l_new = alpha * l_s[:, 0:1] + jnp.sum(p, axis=1, keepdims=True)
            p_hi = p.astype(BF16)
            p_lo = (p - p_hi.astype(F32)).astype(BF16)
            vb = v_ref[...]
            pv = jnp.dot(p_hi, vb, preferred_element_type=F32) + jnp.dot(p_lo, vb, preferred_element_type=F32)
            acc_s[...] = alpha * acc_s[...] + pv
            m_s[...] = jnp.broadcast_to(m_new, m_s.shape)
            l_s[...] = jnp.broadcast_to(l_new, l_s.shape)

        @pl.when(j == i)
        def _():
            l = l_s[...]
            o = acc_s[...] / l
            o_ref[...] = o.astype(BF16)
            o32_ref[...] = o
            lse_ref[...] = m_s[...] + jnp.log(l)

    kj = lambda i, j: jnp.minimum(i, j)
    return pl.pallas_call(
        body, name=name, grid=(heads, nq, nq),
        in_specs=[pl.BlockSpec((tq, 128), lambda h, i, j: (i, h)),
                  pl.BlockSpec((tq, 128), lambda h, i, j: (kj(i, j), heads + h)),
                  pl.BlockSpec((tq, 128), lambda h, i, j: (kj(i, j), 2 * heads + h)),
                  pl.BlockSpec((None, 1, tq), lambda h, i, j: (h, 0, i)),
                  pl.BlockSpec((None, 1, tq), lambda h, i, j: (h, 0, kj(i, j)))],
        out_specs=[pl.BlockSpec((tq, 128), lambda h, i, j: (i, h)), pl.BlockSpec((tq, 128), lambda h, i, j: (i, h)),
                   pl.BlockSpec((None, tq, 128), lambda h, i, j: (h, i, 0))],
        out_shape=[jax.ShapeDtypeStruct((t, heads * 128), BF16), jax.ShapeDtypeStruct((t, heads * 128), F32),
                   jax.ShapeDtypeStruct((heads, t, 128), F32)],
        scratch_shapes=[pltpu.VMEM((tq, 128), F32), pltpu.VMEM((tq, 128), F32), pltpu.VMEM((tq, 128), F32)],
        compiler_params=_params(("parallel", "parallel", "arbitrary")),
    )(qkv, qkv, qkv, c3, c3)


def _fox_bwd(qkv, c3, o, do, lse, heads, *, name):
    t = qkv.shape[0]
    tq = _tile(t, 512)
    nq = t // tq
    scale = HEAD_DIM ** -0.5

    def body(q_ref, k_ref, v_ref, o_ref, do_ref, lse_ref, cq_ref, ck_ref,
             dq_ref, dk_ref, dv_ref, dc_ref, dk_s, dv_s, dc_s):
        j, i = pl.program_id(1), pl.program_id(2)

        @pl.when((j == 0) & (i == 0))
        def _():
            dq_ref[...] = jnp.zeros_like(dq_ref)

        @pl.when(i == j)
        def _():
            dk_s[...] = jnp.zeros_like(dk_s)
            dv_s[...] = jnp.zeros_like(dv_s)
            dc_s[...] = jnp.zeros_like(dc_s)

        @pl.when(i >= j)
        def _():
            q, k, v, dob = q_ref[...], k_ref[...], v_ref[...], do_ref[...]
            s = lax.dot_general(q, k, (((1,), (1,)), ((), ())), preferred_element_type=F32)
            s = s * scale + (cq_ref[:, 0:1] - ck_ref[...])
            row = lax.broadcasted_iota(jnp.int32, (tq, tq), 0) + i * tq
            col = lax.broadcasted_iota(jnp.int32, (tq, tq), 1) + j * tq
            s = jnp.where(col <= row, s, NEG)
            p = jnp.exp(s - lse_ref[:, 0:1])
            delta = jnp.sum(dob.astype(F32) * o_ref[...], axis=1, keepdims=True)
            dp = lax.dot_general(dob, v, (((1,), (1,)), ((), ())), preferred_element_type=F32)
            ds = p * (dp - delta)
            pb, dsb = p.astype(BF16), ds.astype(BF16)
            dv_s[...] += lax.dot_general(pb, dob, (((0,), (0,)), ((), ())), preferred_element_type=F32)
            dk_s[...] += lax.dot_general(dsb, q, (((0,), (0,)), ((), ())), preferred_element_type=F32)
            r0 = pl.multiple_of(i * tq, tq)
            dq_ref[pl.ds(r0, tq), :] += jnp.dot(dsb, k, preferred_element_type=F32) * scale
            dc_s[...] -= jnp.sum(ds, axis=0, keepdims=True)

        @pl.when(i == nq - 1)
        def _():
            dk_ref[...] = (dk_s[...] * scale).astype(BF16)
            dv_ref[...] = dv_s[...].astype(BF16)
            dc_ref[...] = dc_s[...]

    qi = lambda j, i: jnp.maximum(i, j)
    return pl.pallas_call(
        body, name=name, grid=(heads, nq, nq),
        in_specs=[pl.BlockSpec((tq, 128), lambda h, j, i: (qi(j, i), h)),
                  pl.BlockSpec((tq, 128), lambda h, j, i: (j, heads + h)),
                  pl.BlockSpec((tq, 128), lambda h, j, i: (j, 2 * heads + h)),
                  pl.BlockSpec((tq, 128), lambda h, j, i: (qi(j, i), h)),
                  pl.BlockSpec((tq, 128), lambda h, j, i: (qi(j, i), h)),
                  pl.BlockSpec((None, tq, 128), lambda h, j, i: (h, qi(j, i), 0)),
                  pl.BlockSpec((None, 1, tq), lambda h, j, i: (h, 0, qi(j, i))),
                  pl.BlockSpec((None, 1, tq), lambda h, j, i: (h, 0, j))],
        out_specs=[pl.BlockSpec((t, 128), lambda h, j, i: (0, h)),
                   pl.BlockSpec((tq, 128), lambda h, j, i: (j, h)),
                   pl.BlockSpec((tq, 128), lambda h, j, i: (j, h)),
                   pl.BlockSpec((None, 1, tq), lambda h, j, i: (h, 0, j))],
        out_shape=[jax.ShapeDtypeStruct((t, heads * 128), F32), jax.ShapeDtypeStruct((t, heads * 128), BF16),
                   jax.ShapeDtypeStruct((t, heads * 128), BF16), jax.ShapeDtypeStruct((heads, 1, t), F32)],
        scratch_shapes=[pltpu.VMEM((tq, 128), F32), pltpu.VMEM((tq, 128), F32), pltpu.VMEM((1, tq), F32)],
        compiler_params=_params(("arbitrary", "arbitrary", "arbitrary")),
    )(qkv, qkv, qkv, o, do, lse, c3, c3)


def _tril_mask():
    r = lax.broadcasted_iota(jnp.int32, (128, 128), 0)
    c = lax.broadcasted_iota(jnp.int32, (128, 128), 1)
    return r >= c


def _sgu_fwd(uv, g, w, bst, *, name):
    t = uv.shape[0]
    sw = uv.shape[1] // 2
    groups = sw // 128
    tr = _tile(t, 512)

    def body(u_ref, v_ref, g_ref, w_ref, b_ref, y_ref):
        gv = _gelu(v_ref[...].astype(F32))
        inv = lax.rsqrt(jnp.mean(gv * gv, axis=-1, keepdims=True) + RMS_EPS)
        vn = ((gv * inv) * g_ref[...]).astype(BF16)
        gu = _gelu(u_ref[...].astype(F32))
        mask = _tril_mask()
        for gi in range(groups):
            wg = jnp.where(mask, w_ref[gi], 0.0).astype(BF16)
            bcol = b_ref[:, gi:gi + 1]
            cs = slice(gi * 128, (gi + 1) * 128)
            for ci in range(tr // 128):
                rs = slice(ci * 128, (ci + 1) * 128)
                mixed = jnp.dot(wg, vn[rs, cs], preferred_element_type=F32) + bcol
                y_ref[rs, cs] = (gu[rs, cs] * mixed).astype(BF16)

    return pl.pallas_call(
        body, name=name, grid=(t // tr,),
        in_specs=[pl.BlockSpec((tr, sw), lambda i: (i, 0)), pl.BlockSpec((tr, sw), lambda i: (i, 1)),
                  pl.BlockSpec((1, sw), lambda i: (0, 0)), pl.BlockSpec((groups, 128, 128), lambda i: (0, 0, 0)),
                  pl.BlockSpec((128, 128), lambda i: (0, 0))],
        out_specs=pl.BlockSpec((tr, sw), lambda i: (i, 0)),
        out_shape=jax.ShapeDtypeStruct((t, sw), BF16),
        compiler_params=_params(("parallel",)),
    )(uv, uv, g, w, bst)


def _sgu_bwd(uv, g, w, bst, dy, *, name):
    t = uv.shape[0]
    sw = uv.shape[1] // 2
    groups = sw // 128
    tr = _tile(t, 256)
    nsteps = t // tr

    def body(u_ref, v_ref, g_ref, w_ref, b_ref, dy_ref, duv_ref, dw_ref, db_ref, dg_ref, dvn_s, dgu_s):
        step = pl.program_id(0)

        @pl.when(step == 0)
        def _():
            dw_ref[...] = jnp.zeros_like(dw_ref)
            db_ref[...] = jnp.zeros_like(db_ref)
            dg_ref[...] = jnp.zeros_like(dg_ref)

        vf = v_ref[...].astype(F32)
        gv, gv_grad = _gelu_and_grad(vf)
        inv = lax.rsqrt(jnp.mean(gv * gv, axis=-1, keepdims=True) + RMS_EPS)
        xn = gv * inv
        gvec = g_ref[...]
        vn = (xn * gvec).astype(BF16)
        uf = u_ref[...].astype(F32)
        gu, gu_grad = _gelu_and_grad(uf)
        dyf = dy_ref[...].astype(F32)
        mask = _tril_mask()
        lane = lax.broadcasted_iota(jnp.int32, (128, 128), 1)
        dball = jnp.zeros((128, 128), F32)
        for gi in range(groups):
            wg = jnp.where(mask, w_ref[gi], 0.0).astype(BF16)
            wgt = wg.T
            bcol = b_ref[:, gi:gi + 1]
            cs = slice(gi * 128, (gi + 1) * 128)
            dwg = jnp.zeros((128, 128), F32)
            dbg = jnp.zeros((128, 1), F32)
            for ci in range(tr // 128):
                rs = slice(ci * 128, (ci + 1) * 128)
                vnb = vn[rs, cs]
                mixed = jnp.dot(wg, vnb, preferred_element_type=F32) + bcol
                dgu_s[rs, cs] = dyf[rs, cs] * mixed
                dmix = dyf[rs, cs] * gu[rs, cs]
                dmb = dmix.astype(BF16)
                dvn_s[rs, cs] = jnp.dot(wgt, dmb, preferred_element_type=F32)
                dwg = dwg + lax.dot_general(dmb, vnb, (((1,), (1,)), ((), ())), preferred_element_type=F32)
                dbg = dbg + jnp.sum(dmix, axis=1, keepdims=True)
            dw_ref[gi] += dwg
            dball = dball + jnp.where(lane == gi, dbg, 0.0)
        db_ref[...] += dball
        dvn = dvn_s[...]
        dg_ref[...] += jnp.sum(dvn * xn, axis=0, keepdims=True)
        dxn = dvn * gvec
        dgv = inv * (dxn - xn * jnp.mean(dxn * xn, axis=-1, keepdims=True))
        duv_ref[:, 0:sw] = (dgu_s[...] * gu_grad).astype(BF16)
        duv_ref[:, sw:2 * sw] = (dgv * gv_grad).astype(BF16)

        @pl.when(step == nsteps - 1)
        def _():
            for gi in range(groups):
                dw_ref[gi] = jnp.where(mask, dw_ref[gi], 0.0)

    return pl.pallas_call(
        body, name=name, grid=(nsteps,),
        in_specs=[pl.BlockSpec((tr, sw), lambda i: (i, 0)), pl.BlockSpec((tr, sw), lambda i: (i, 1)),
                  pl.BlockSpec((1, sw), lambda i: (0, 0)), pl.BlockSpec((groups, 128, 128), lambda i: (0, 0, 0)),
                  pl.BlockSpec((128, 128), lambda i: (0, 0)), pl.BlockSpec((tr, sw), lambda i: (i, 0))],
        out_specs=[pl.BlockSpec((tr, 2 * sw), lambda i: (i, 0)),
                   pl.BlockSpec((groups, 128, 128), lambda i: (0, 0, 0)),
                   pl.BlockSpec((128, 128), lambda i: (0, 0)), pl.BlockSpec((1, sw), lambda i: (0, 0))],
        out_shape=[jax.ShapeDtypeStruct((t, 2 * sw), BF16), jax.ShapeDtypeStruct((groups, 128, 128), F32),
                   jax.ShapeDtypeStruct((128, 128), F32), jax.ShapeDtypeStruct((1, sw), F32)],
        scratch_shapes=[pltpu.VMEM((tr, sw), F32), pltpu.VMEM((tr, sw), F32)],
        compiler_params=_params(("arbitrary",)),
    )(uv, uv, g, w, bst, dy)


def _merge_fwd(ya, yb, wa, wb, gates, *, name):
    t, kdim = ya.shape
    d = wa.shape[1]
    tm, tn = _tile(t, 512), _tile(d, 1024)
    nj = d // tn

    def body(ya_ref, yb_ref, wa_ref, wb_ref, ga_ref, gb_ref, mg_ref, za_ref, zb_ref):
        za = jnp.dot(ya_ref[...], wa_ref[...], preferred_element_type=F32)
        zb = jnp.dot(yb_ref[...], wb_ref[...], preferred_element_type=F32)
        sa = _sigmoid(ga_ref[...].astype(F32))
        sb = _sigmoid(gb_ref[...].astype(F32))
        mg_ref[...] = (sa * za + sb * zb).astype(BF16)
        za_ref[...] = za.astype(BF16)
        zb_ref[...] = zb.astype(BF16)

    yspec = pl.BlockSpec((tm, kdim), lambda i, j: (i, 0))
    wspec = pl.BlockSpec((kdim, tn), lambda i, j: (0, j))
    ospec = pl.BlockSpec((tm, tn), lambda i, j: (i, j))
    return pl.pallas_call(
        body, name=name, grid=(t // tm, nj),
        in_specs=[yspec, yspec, wspec, wspec, ospec, pl.BlockSpec((tm, tn), lambda i, j: (i, nj + j))],
        out_specs=[ospec, ospec, ospec],
        out_shape=[jax.ShapeDtypeStruct((t, d), BF16)] * 3,
        compiler_params=_params(("parallel", "parallel")),
    )(ya, yb, wa, wb, gates, gates)


def _merge_bwd(dmg, gates, za, zb, *, name):
    t, d = dmg.shape
    tr = _tile(t, 256)

    def body(dm_ref, ga_ref, gb_ref, za_ref, zb_ref, dza_ref, dzb_ref, dg_ref):
        dm = dm_ref[...].astype(F32)
        sa = _sigmoid(ga_ref[...].astype(F32))
        sb = _sigmoid(gb_ref[...].astype(F32))
        dza_ref[...] = (dm * sa).astype(BF16)
        dzb_ref[...] = (dm * sb).astype(BF16)
        dg_ref[:, 0:d] = (dm * za_ref[...].astype(F32) * (sa * (1.0 - sa))).astype(BF16)
        dg_ref[:, d:2 * d] = (dm * zb_ref[...].astype(F32) * (sb * (1.0 - sb))).astype(BF16)

    row = pl.BlockSpec((tr, d), lambda i: (i, 0))
    return pl.pallas_call(
        body, name=name, grid=(t // tr,),
        in_specs=[row, row, pl.BlockSpec((tr, d), lambda i: (i, 1)), row, row],
        out_specs=[row, row, pl.BlockSpec((tr, 2 * d), lambda i: (i, 0))],
        out_shape=[jax.ShapeDtypeStruct((t, d), BF16), jax.ShapeDtypeStruct((t, d), BF16),
                   jax.ShapeDtypeStruct((t, 2 * d), BF16)],
        compiler_params=_params(("parallel",)),
    )(dmg, gates, gates, za, zb)


def _shift_down(ext, k, rows):
    return pltpu.roll(ext, k, 0)[8:8 + rows]


def _conv_fwd(ab, cw, cb, *, name):
    t = ab.shape[0]
    dff = ab.shape[1] // 2
    tr, tc = _tile(t, 512), _tile(dff, 512)
    nc = dff // tc
    r8 = tr // 8

    def body(a_ref, ap_ref, b_ref, cw_ref, cb_ref, g_ref):
        i = pl.program_id(0)
        prev = ap_ref[...].astype(F32) * jnp.where(i > 0, 1.0, 0.0)
        a = a_ref[...].astype(F32)
        ext = jnp.concatenate([prev, a], axis=0)
        acc = cb_ref[...] + cw_ref[0:1, :] * _shift_down(ext, 2, tr) + cw_ref[1:2, :] * _shift_down(ext, 1, tr) \
            + cw_ref[2:3, :] * a
        g_ref[...] = (_gelu(acc) * b_ref[...].astype(F32)).astype(BF16)

    return pl.pallas_call(
        body, name=name, grid=(t // tr, nc),
        in_specs=[pl.BlockSpec((tr, tc), lambda i, j: (i, j)),
                  pl.BlockSpec((8, tc), lambda i, j: (jnp.maximum(i * r8 - 1, 0), j)),
                  pl.BlockSpec((tr, tc), lambda i, j: (i, nc + j)),
                  pl.BlockSpec((CONV_WIDTH, tc), lambda i, j: (0, j)), pl.BlockSpec((1, tc), lambda i, j: (0, j))],
        out_specs=pl.BlockSpec((tr, tc), lambda i, j: (i, j)),
        out_shape=jax.ShapeDtypeStruct((t, dff), BF16),
        compiler_params=_params(("parallel", "parallel")),
    )(ab, ab, ab, cw, cb)


def _conv_bwd(ab, cw, cb, dgg, *, name):
    t = ab.shape[0]
    dff = ab.shape[1] // 2
    tr, tc = _tile(t, 256), _tile(dff, 512)
    nc, nr = dff // tc, t // tr
    r8 = tr // 8
    ext_rows = tr + 16

    def body(a_ref, ap_ref, an_ref, b_ref, bn_ref, dg_ref, dgn_ref, cw_ref, cb_ref,
             da_ref, db_ref, dcw_ref, dcb_ref):
        i = pl.program_id(1)
        has_prev = jnp.where(i > 0, 1.0, 0.0)
        has_next = jnp.where(i < nr - 1, 1.0, 0.0)
        a_ext = jnp.concatenate([ap_ref[...].astype(F32) * has_prev, a_ref[...].astype(F32),
                                 an_ref[...].astype(F32)], axis=0)
        b_ext = jnp.concatenate([b_ref[0:8, :].astype(F32), b_ref[...].astype(F32), bn_ref[...].astype(F32)], axis=0)
        dg_ext = jnp.concatenate([dg_ref[0:8, :].astype(F32), dg_ref[...].astype(F32),
                                  dgn_ref[...].astype(F32) * has_next], axis=0)
        w0, w1, w2 = cw_ref[0:1, :], cw_ref[1:2, :], cw_ref[2:3, :]
        a_m2 = pltpu.roll(a_ext, 2, 0)
        a_m1 = pltpu.roll(a_ext, 1, 0)
        acc = cb_ref[...] + w0 * a_m2 + w1 * a_m1 + w2 * a_ext
        gel, gel_grad = _gelu_and_grad(acc)
        dacc = dg_ext * b_ext * gel_grad
        d_a = w2 * dacc + w1 * pltpu.roll(dacc, ext_rows - 1, 0) + w0 * pltpu.roll(dacc, ext_rows - 2, 0)
        da_ref[...] = d_a[8:8 + tr].astype(BF16)
        db_ref[...] = (dg_ext[8:8 + tr] * gel[8:8 + tr]).astype(BF16)
        dm = dacc[8:8 + tr]
        dcw = jnp.concatenate([jnp.sum(dm * a_m2[8:8 + tr], axis=0, keepdims=True),
                               jnp.sum(dm * a_m1[8:8 + tr], axis=0, keepdims=True),
                               jnp.sum(dm * a_ext[8:8 + tr], axis=0, keepdims=True)], axis=0)
        dcb = jnp.sum(dm, axis=0, keepdims=True)

        @pl.when(i == 0)
        def _():
            dcw_ref[...] = dcw
            dcb_ref[...] = dcb

        @pl.when(i > 0)
        def _():
            dcw_ref[...] += dcw
            dcb_ref[...] += dcb

    cur = lambda off: pl.BlockSpec((tr, tc), lambda j, i: (i, off + j))
    prv = lambda off: pl.BlockSpec((8, tc), lambda j, i: (jnp.maximum(i * r8 - 1, 0), off + j))
    nxt = lambda off: pl.BlockSpec((8, tc), lambda j, i: (jnp.minimum((i + 1) * r8, nr * r8 - 1), off + j))
    return pl.pallas_call(
        body, name=name, grid=(nc, nr),
        in_specs=[cur(0), prv(0), nxt(0), cur(nc), nxt(nc), cur(0), nxt(0),
                  pl.BlockSpec((CONV_WIDTH, tc), lambda j, i: (0, j)), pl.BlockSpec((1, tc), lambda j, i: (0, j))],
        out_specs=[cur(0), cur(0), pl.BlockSpec((CONV_WIDTH, tc), lambda j, i: (0, j)),
                   pl.BlockSpec((1, tc), lambda j, i: (0, j))],
        out_shape=[jax.ShapeDtypeStruct((t, dff), BF16), jax.ShapeDtypeStruct((t, dff), BF16),
                   jax.ShapeDtypeStruct((CONV_WIDTH, dff), F32), jax.ShapeDtypeStruct((1, dff), F32)],
        compiler_params=_params(("parallel", "arbitrary")),
    )(ab, ab, ab, ab, ab, dgg, dgg, cw, cb)


def _place():
    x, y, c = lax.axis_index("x"), lax.axis_index("y"), lax.axis_index("c")
    chips = [(1 - x, y), (x, 1 - y), (1 - x, 1 - y)]
    return x, y, c, chips


def _all_gather(shards, *, name):
    n = len(shards)

    def body(*refs):
        ins, outs = refs[:n], refs[n:2 * n]
        send_sems, recv_sems, local_sems = refs[2 * n:]
        x, y, c, chips = _place()
        me, sibling = (x, y, c), (x, y, 1 - c)

        def slot(p):
            return 4 * p[0] + 2 * p[1] + p[2]

        def copy(a, k, block, to, src=None):
            dst = outs[a].at[slot(block)]
            return pltpu.make_async_remote_copy(
                src_ref=dst if src is None else src, dst_ref=dst, send_sem=send_sems.at[a, k],
                recv_sem=recv_sems.at[a, k], device_id=to, device_id_type=MESH)

        mine = [pltpu.make_async_copy(ins[a], outs[a].at[slot(me)], local_sems.at[a]) for a in range(n)]
        for cp in mine:
            cp.start()
        first = []
        for a in range(n):
            first.append(copy(a, 0, me, sibling, src=ins[a]))
            first += [copy(a, 1 + j, me, (*chip, c), src=ins[a]) for j, chip in enumerate(chips)]
        for cp in first:
            cp.start()
        passed = []
        for j, chip in enumerate(chips):
            for a in range(n):
                copy(a, 1 + j, (*chip, c), me).wait_recv()
                fwd = copy(a, 4 + j, (*chip, c), sibling)
                fwd.start()
                passed.append(fwd)
        for a in range(n):
            copy(a, 0, sibling, me).wait_recv()
            for j, chip in enumerate(chips):
                copy(a, 4 + j, (*chip, 1 - c), me).wait_recv()
        for cp in first + passed:
            cp.wait_send()
        for cp in mine:
            cp.wait()

    return pl.pallas_call(
        body, name=name, in_specs=[ANY] * n, out_specs=[ANY] * n,
        out_shape=[jax.ShapeDtypeStruct((N_DEV,) + s.shape, s.dtype) for s in shards],
        scratch_shapes=[pltpu.SemaphoreType.DMA((n, 7)), pltpu.SemaphoreType.DMA((n, 7)),
                        pltpu.SemaphoreType.DMA((n,))],
    )(*shards)


def _sibling_exchange(grads, *, name):
    n = len(grads)

    def body(*refs):
        ins, outs = refs[:n], refs[n:2 * n]
        send_sems, recv_sems = refs[2 * n:]
        x, y, c, _ = _place()
        copies = []
        for a in range(n):
            for k in range(4):
                copies.append(pltpu.make_async_remote_copy(
                    src_ref=ins[a].at[2 * k + 1 - c], dst_ref=outs[a].at[k], send_sem=send_sems.at[a, k],
                    recv_sem=recv_sems.at[a, k], device_id=(x, y, 1 - c), device_id_type=MESH))
        for cp in copies:
            cp.start()
        for cp in copies:
            cp.wait()

    return pl.pallas_call(
        body, name=name, in_specs=[ANY] * n, out_specs=[ANY] * n,
        out_shape=[jax.ShapeDtypeStruct((4,) + g.shape[1:], g.dtype) for g in grads],
        scratch_shapes=[pltpu.SemaphoreType.DMA((n, 4)), pltpu.SemaphoreType.DMA((n, 4))],
    )(*grads)


def _chip_exchange(parts, *, name):
    n = len(parts)

    def body(*refs):
        ins, outs = refs[:n], refs[n:2 * n]
        send_sems, recv_sems = refs[2 * n:]
        x, y, c, chips = _place()
        copies = []
        for a in range(n):
            for j, chip in enumerate(chips):
                copies.append(pltpu.make_async_remote_copy(
                    src_ref=ins[a].at[2 * chip[0] + chip[1]], dst_ref=outs[a].at[j], send_sem=send_sems.at[a, j],
                    recv_sem=recv_sems.at[a, j], device_id=(*chip, c), device_id_type=MESH))
        for cp in copies:
            cp.start()
        for cp in copies:
            cp.wait()

    return pl.pallas_call(
        body, name=name, in_specs=[ANY] * n, out_specs=[ANY] * n,
        out_shape=[jax.ShapeDtypeStruct((3,) + p.shape[1:], p.dtype) for p in parts],
        scratch_shapes=[pltpu.SemaphoreType.DMA((n, 3)), pltpu.SemaphoreType.DMA((n, 3))],
    )(*parts)


def _as3d(shape):
    if len(shape) == 1:
        return (1, 1, shape[0])
    if len(shape) == 2:
        return (1,) + tuple(shape)
    lead = 1
    for s in shape[:-2]:
        lead *= s
    return (lead, shape[-2], shape[-1])


def _row_tile(rows, cols):
    tr = rows
    for cand in (512, 256, 128, 64, 32, 16, 8):
        if rows % cand == 0 and cand * cols * 4 <= 2 * 1024 * 1024:
            return cand
    return tr


def _pair_add(grad8, recv4, cidx, *, name):
    shape = grad8.shape[1:]
    lead, rows, cols = _as3d(shape)
    g = grad8.reshape((N_DEV, lead, rows, cols))
    r = recv4.reshape((4, lead, rows, cols))
    tr = _row_tile(rows, cols)

    def body(c_ref, g_ref, r_ref, o_ref):
        o_ref[...] = (g_ref[...].astype(F32) + r_ref[...].astype(F32)).astype(BF16)

    blk = (None, None, tr, cols)
    out = pl.pallas_call(
        body, name=name,
        grid_spec=pltpu.PrefetchScalarGridSpec(
            num_scalar_prefetch=1, grid=(4, lead, rows // tr),
            in_specs=[pl.BlockSpec(blk, lambda k, l, i, c_ref: (2 * k + c_ref[0], l, i, 0)),
                      pl.BlockSpec(blk, lambda k, l, i, c_ref: (k, l, i, 0))],
            out_specs=pl.BlockSpec(blk, lambda k, l, i, c_ref: (k, l, i, 0))),
        out_shape=jax.ShapeDtypeStruct((4, lead, rows, cols), BF16),
        compiler_params=_params(("parallel", "parallel", "parallel")),
    )(cidx, g, r)
    return out.reshape((4,) + shape)


def _adamw_math(w, g, m, v):
    m = ADAM_B1 * m + (1.0 - ADAM_B1) * g
    v = ADAM_B2 * v + (1.0 - ADAM_B2) * (g * g)
    m_hat = m / (1.0 - ADAM_B1 ** ADAM_STEP)
    v_hat = v / (1.0 - ADAM_B2 ** ADAM_STEP)
    delta = -ADAM_LR * (m_hat / (jnp.sqrt(v_hat) + ADAM_EPS) + ADAM_WD * w)
    return delta, m, v


def _shard_adamw(part4, recv3, w, m, v, kidx, *, name):
    shape = w.shape
    lead, rows, cols = _as3d(shape)
    p = part4.reshape((4, lead, rows, cols))
    r = recv3.reshape((3, lead, rows, cols))
    w3, m3, v3 = (a.reshape((lead, rows, cols)) for a in (w, m, v))
    tr = _row_tile(rows, cols)

    def body(k_ref, p_ref, r0_ref, r1_ref, r2_ref, w_ref, m_ref, v_ref, g_out, d_out, m_out, v_out):
        g = ((p_ref[...].astype(F32) + r0_ref[...].astype(F32)) + r1_ref[...].astype(F32)) + r2_ref[...].astype(F32)
        delta, mn, vn = _adamw_math(w_ref[...], g, m_ref[...], v_ref[...])
        g_out[...] = g
        d_out[...] = delta
        m_out[...] = mn
        v_out[...] = vn

    blk4 = (None, None, tr, cols)
    blk3 = (None, tr, cols)
    rspec = lambda j: pl.BlockSpec(blk4, lambda l, i, k_ref: (j, l, i, 0))
    espec = pl.BlockSpec(blk3, lambda l, i, k_ref: (l, i, 0))
    outs = pl.pallas_call(
        body, name=name,
        grid_spec=pltpu.PrefetchScalarGridSpec(
            num_scalar_prefetch=1, grid=(lead, rows // tr),
            in_specs=[pl.BlockSpec(blk4, lambda l, i, k_ref: (k_ref[0], l, i, 0)), rspec(0), rspec(1), rspec(2),
                      espec, espec, espec],
            out_specs=[espec] * 4),
        out_shape=[jax.ShapeDtypeStruct((lead, rows, cols), F32)] * 4,
        compiler_params=_params(("parallel", "parallel")),
    )(kidx, p, r, r, r, w3, m3, v3)
    return [o.reshape(shape) for o in outs]


def _small_adamw(gath, w, m, v, *, name):
    rows = w.shape[0]

    def body(g_ref, w_ref, m_ref, v_ref, g_out, d_out, m_out, v_out):
        g = g_ref[0]
        for dev in range(1, N_DEV):
            g = g + g_ref[dev]
        delta, mn, vn = _adamw_math(w_ref[...], g, m_ref[...], v_ref[...])
        g_out[...] = g
        d_out[...] = delta
        m_out[...] = mn
        v_out[...] = vn

    tr = _row_tile(rows, 128 * N_DEV)
    espec = pl.BlockSpec((tr, 128), lambda i: (i, 0))
    return pl.pallas_call(
        body, name=name, grid=(rows // tr,),
        in_specs=[pl.BlockSpec((N_DEV, tr, 128), lambda i: (0, i, 0)), espec, espec, espec],
        out_specs=[espec] * 4, out_shape=[jax.ShapeDtypeStruct((rows, 128), F32)] * 4,
        compiler_params=_params(("parallel",)),
    )(gath, w, m, v)


def _pack(arrs):
    flat = jnp.concatenate([a.reshape(-1) for a in arrs])
    total = flat.shape[0]
    rows = -(-total // (128 * 64)) * 64
    return jnp.pad(flat, (0, rows * 128 - total)).reshape(rows, 128)


def _unpack(packed, like):
    flat = packed.reshape(-1)
    out, off = [], 0
    for a in like:
        out.append(flat[off:off + a.size].reshape(a.shape))
        off += a.size
    return out


def kernel(x, g_mix, w_in, b_forget, g_sgu, w_spatial, b_spatial, w_branch_a, w_branch_b, w_out, g_ffn, w_up, conv_w, conv_b, w_down, g_final, loss_target, m_g_mix, m_w_in, m_b_forget, m_g_sgu, m_w_spatial, m_b_spatial, m_w_branch_a, m_w_branch_b, m_w_out, m_g_ffn, m_w_up, m_conv_w, m_conv_b, m_w_down, m_g_final, v_g_mix, v_w_in, v_b_forget, v_g_sgu, v_w_spatial, v_b_spatial, v_w_branch_a, v_w_branch_b, v_w_out, v_g_ffn, v_w_up, v_conv_w, v_conv_b, v_w_down, v_g_final):
    depth, d = g_mix.shape
    heads = b_forget.shape[1]
    fw = heads * HEAD_DIM
    sw = g_sgu.shape[1]
    dff = conv_b.shape[1]
    t = x.shape[1]
    in_width = w_in.shape[2] * N_DEV
    o_f, o_u, o_g = 3 * fw, 3 * fw + heads, 3 * fw + heads + 2 * sw

    sharded = [w_in, w_branch_a, w_branch_b, w_out, w_up, conv_w, w_down]
    g_w_in, g_wa, g_wb, g_wo, g_wu, g_cw, g_wd = _all_gather(
        [w if w is conv_w else w.astype(BF16) for w in sharded], name="weights_all_gather")
    cols = lambda g: jnp.moveaxis(g, 0, 2).reshape(g.shape[1], g.shape[2], N_DEV * g.shape[3])
    rows = lambda g: jnp.moveaxis(g, 0, 1).reshape(g.shape[1], N_DEV * g.shape[2], g.shape[3])
    f_w_in = cols(g_w_in)
    assert f_w_in.shape[2] == in_width
    f_wqkv, f_wuv, f_wg = f_w_in[:, :, :o_f], f_w_in[:, :, o_u:o_g], f_w_in[:, :, o_g:]
    f_wf = jnp.pad(f_w_in[:, :, o_f:o_u], ((0, 0), (0, 0), (0, 128 - heads)))
    f_wa, f_wb, f_wu = cols(g_wa), cols(g_wb), cols(g_wu)
    f_wo, f_wd = rows(g_wo), rows(g_wd)
    f_cw = cols(g_cw).astype(F32)

    bpad = jnp.pad(b_forget, ((0, 0), (0, 128 - heads)))
    bst = jnp.pad(jnp.swapaxes(b_spatial, 1, 2), ((0, 0), (0, 0), (0, 128 - b_spatial.shape[1])))

    xs = x[0]
    saved = []
    for l in range(depth):
        n = lambda s: f"{s}_{l}"
        gm = g_mix[l][None]
        h = _rms_fwd(xs, gm, name=n("rms_mix"))
        qkv = _mm(h, f_wqkv[l], name=n("proj_qkv"))
        uv = _mm(h, f_wuv[l], name=n("proj_uv"))
        gates = _mm(h, f_wg[l], name=n("proj_gates"))
        flog = _mm(h, f_wf[l], out_dtype=F32, name=n("proj_forget"))
        c3 = _forget_fwd(flog, bpad[l][None], name=n("forget_fwd")).reshape(128, 1, t)
        ya, ya32, lse = _fox_fwd(qkv, c3, heads, name=n("fox_fwd"))
        yb = _sgu_fwd(uv, g_sgu[l][None], w_spatial[l], bst[l], name=n("sgu_fwd"))
        merged, za, zb = _merge_fwd(ya, yb, f_wa[l], f_wb[l], gates, name=n("merge_fwd"))
        x1 = _mm(merged, f_wo[l], out_dtype=F32, res=xs, name=n("out_proj"))
        h2 = _rms_fwd(x1, g_ffn[l][None], name=n("rms_ffn"))
        ab = _mm(h2, f_wu[l], name=n("ffn_up"))
        gg = _conv_fwd(ab, f_cw[l], conv_b[l][None], name=n("conv_fwd"))
        x2 = _mm(gg, f_wd[l], out_dtype=F32, res=x1, name=n("ffn_down"))
        saved.append((xs, h, qkv, uv, gates, flog, c3, ya, ya32, lse, yb, merged, za, zb, x1, h2, ab, gg))
        xs = x2

    loss_row, dx, dxb, d_g_final = _final_loss(xs, g_final[None], loss_target[0], name="final_loss")
    loss = lax.psum(loss_row[0, 0], MESH_AXES)

    gr = {k: [None] * depth for k in ("g_mix", "w_in", "b_forget", "g_sgu", "w_spatial", "b_spatial", "w_branch_a",
                                      "w_branch_b", "w_out", "g_ffn", "w_up", "conv_w", "conv_b", "w_down")}
    for l in reversed(range(depth)):
        n = lambda s: f"{s}_{l}"
        xs, h, qkv, uv, gates, flog, c3, ya, ya32, lse, yb, merged, za, zb, x1, h2, ab, gg = saved[l]
        gr["w_down"][l] = _mm(gg, dxb, ta=True, name=n("d_w_down"))
        dgg = _mm(dxb, f_wd[l], tb=True, name=n("d_gg"))
        d_a, d_b, d_cw, d_cb = _conv_bwd(ab, f_cw[l], conv_b[l][None], dgg, name=n("conv_bwd"))
        gr["conv_w"][l], gr["conv_b"][l] = d_cw.astype(BF16), d_cb[0]
        dab = jnp.concatenate([d_a, d_b], axis=1)
        gr["w_up"][l] = _mm(h2, dab, ta=True, name=n("d_w_up"))
        dh2 = _mm(dab, f_wu[l], tb=True, out_dtype=F32, name=n("d_h2"))
        dx, dxb, dg = _rms_bwd(x1, g_ffn[l][None], dh2, dx, name=n("rms_ffn_bwd"))
        gr["g_ffn"][l] = dg[0]
        gr["w_out"][l] = _mm(merged, dxb, ta=True, name=n("d_w_out"))
        dmg = _mm(dxb, f_wo[l], tb=True, name=n("d_merged"))
        dza, dzb, dgates = _merge_bwd(dmg, gates, za, zb, name=n("merge_bwd"))
        gr["w_branch_a"][l] = _mm(ya, dza, ta=True, name=n("d_w_a"))
        gr["w_branch_b"][l] = _mm(yb, dzb, ta=True, name=n("d_w_b"))
        dya = _mm(dza, f_wa[l], tb=True, name=n("d_ya"))
        dyb = _mm(dzb, f_wb[l], tb=True, name=n("d_yb"))
        duv, d_ws, d_bst, d_gs = _sgu_bwd(uv, g_sgu[l][None], w_spatial[l], bst[l], dyb, name=n("sgu_bwd"))
        gr["w_spatial"][l], gr["g_sgu"][l] = d_ws, d_gs[0]
        gr["b_spatial"][l] = d_bst[:, :b_spatial.shape[1]].T
        dq, dk, dv, dc3 = _fox_bwd(qkv, c3, ya32, dya, lse, heads, name=n("fox_bwd"))
        dct = jnp.pad(dc3.reshape(heads, t), ((0, 128 - heads), (0, 0)))
        dflog, d_bf = _forget_bwd(flog, bpad[l][None], dct, name=n("forget_bwd"))
        gr["b_forget"][l] = d_bf[0, :heads]
        dqkv = jnp.concatenate([dq.astype(BF16), dk, dv], axis=1)
        gw = [_mm(h, dqkv, ta=True, name=n("d_w_qkv")), _mm(h, dflog, ta=True, name=n("d_w_forget"))[:, :heads],
              _mm(h, duv, ta=True, name=n("d_w_uv")), _mm(h, dgates, ta=True, name=n("d_w_gates"))]
        gr["w_in"][l] = jnp.concatenate(gw, axis=1)
        dh = _mm(dflog, f_wf[l], tb=True, out_dtype=F32, name=n("d_h_forget"))
        dh = _mm(dqkv, f_wqkv[l], tb=True, out_dtype=F32, res=dh, name=n("d_h_qkv"))
        dh = _mm(duv, f_wuv[l], tb=True, out_dtype=F32, res=dh, name=n("d_h_uv"))
        dh = _mm(dgates, f_wg[l], tb=True, out_dtype=F32, res=dh, name=n("d_h_gates"))
        dx, dxb, dg = _rms_bwd(xs, g_mix[l][None], dh, dx, name=n("rms_mix_bwd"))
        gr["g_mix"][l] = dg[0]
    grad_x = dx[None]

    col_shards = lambda g: jnp.moveaxis(jnp.stack(g).reshape(depth, g[0].shape[0], N_DEV, -1), 2, 0)
    row_shards = lambda g: jnp.moveaxis(jnp.stack(g).reshape(depth, N_DEV, -1, g[0].shape[1]), 1, 0)
    full = [col_shards(gr["w_in"]), col_shards(gr["w_branch_a"]), col_shards(gr["w_branch_b"]),
            row_shards(gr["w_out"]), col_shards(gr["w_up"]), col_shards(gr["conv_w"]), row_shards(gr["w_down"])]
    cidx = lax.axis_index("c").astype(jnp.int32).reshape(1)
    kidx = (2 * lax.axis_index("x") + lax.axis_index("y")).astype(jnp.int32).reshape(1)
    names = ["w_in", "w_branch_a", "w_branch_b", "w_out", "w_up", "conv_w", "w_down"]
    from_sibling = _sibling_exchange(full, name="grads_sibling_exchange")
    parts = [_pair_add(g8, r4, cidx, name=f"grads_pair_add_{nm}") for g8, r4, nm in zip(full, from_sibling, names)]
    from_chips = _chip_exchange(parts, name="grads_chip_exchange")
    moments = {"w_in": (m_w_in, v_w_in), "w_branch_a": (m_w_branch_a, v_w_branch_a),
               "w_branch_b": (m_w_branch_b, v_w_branch_b), "w_out": (m_w_out, v_w_out), "w_up": (m_w_up, v_w_up),
               "conv_w": (m_conv_w, v_conv_w), "w_down": (m_w_down, v_w_down)}
    res = {}
    for nm, w, p4, r3 in zip(names, sharded, parts, from_chips):
        res[nm] = _shard_adamw(p4, r3, w, moments[nm][0], moments[nm][1], kidx, name=f"adamw_{nm}")

    small = ["g_mix", "b_forget", "g_sgu", "w_spatial", "b_spatial", "g_ffn", "conv_b", "g_final"]
    small_w = [g_mix, b_forget, g_sgu, w_spatial, b_spatial, g_ffn, conv_b, g_final]
    small_m = [m_g_mix, m_b_forget, m_g_sgu, m_w_spatial, m_b_spatial, m_g_ffn, m_conv_b, m_g_final]
    small_v = [v_g_mix, v_b_forget, v_g_sgu, v_w_spatial, v_b_spatial, v_g_ffn, v_conv_b, v_g_final]
    small_g = [jnp.stack(gr[nm]) for nm in small[:-1]] + [d_g_final[0]]
    (gath,) = _all_gather([_pack(small_g)], name="small_grads_all_gather")
    outs = _small_adamw(gath, _pack(small_w), _pack(small_m), _pack(small_v), name="adamw_replicated")
    for nm, vals in zip(small, zip(*[_unpack(o, small_w) for o in outs])):
        res[nm] = list(vals)

    order = ["g_mix", "w_in", "b_forget", "g_sgu", "w_spatial", "b_spatial", "w_branch_a", "w_branch_b", "w_out",
             "g_ffn", "w_up", "conv_w", "conv_b", "w_down", "g_final"]
    return (loss, grad_x, *[res[nm][0] for nm in order], *[res[nm][1] for nm in order],
            *[res[nm][2] for nm in order], *[res[nm][3] for nm in order])
```

```python
import numpy as np

import jax
import jax.numpy as jnp
from jax import lax
from jax.experimental import pallas as pl
from jax.experimental.pallas import tpu as pltpu

F32 = jnp.float32
BF16 = jnp.bfloat16

RMS_EPS = 1e-6
HEAD_DIM = 128
CONV_WIDTH = 3
ADAM_LR = 0.001
ADAM_B1 = 0.9
ADAM_B2 = 0.999
ADAM_EPS = 1e-08
ADAM_WD = 0.01
ADAM_STEP = 10
N_DEV = 8
V7X_VMEM_LIMIT = 56 * 1024 * 1024
NEG = -1e30
ANY = pl.BlockSpec(memory_space=pl.ANY)
MESH = pl.DeviceIdType.MESH


def _tile(dim, pref):
    for t in (2048, 1024, 512, 256, 128):
        if t <= pref and dim % t == 0:
            return t
    return dim


def _gelu(x):
    t = jnp.tanh(0.7978845608028654 * (x + 0.044715 * (x * x * x)))
    return x * (0.5 * (1.0 + t))


def _gelu_and_grad(x):
    x2 = x * x
    t = jnp.tanh(0.7978845608028654 * (x + 0.044715 * (x2 * x)))
    cdf = 0.5 * (1.0 + t)
    dt = (1.0 - t * t) * (0.7978845608028654 * (1.0 + 3.0 * 0.044715 * x2))
    return x * cdf, cdf + 0.5 * x * dt


def _sigmoid(x):
    return 1.0 / (1.0 + jnp.exp(-x))


class _Comm:
    def __init__(self, srcs, new, alias, n_copies, emit):
        self.srcs = list(srcs)
        self.new = list(new)
        self.alias = list(alias)
        self.n_copies = n_copies
        self.emit = emit


def _place():
    x, y, c = lax.axis_index("x"), lax.axis_index("y"), lax.axis_index("c")
    chips = [(1 - x, y), (x, 1 - y), (1 - x, 1 - y)]
    return x, y, c, chips


def _remote(src, dst, sems, k, to):
    return pltpu.make_async_remote_copy(src_ref=src, dst_ref=dst, send_sem=sems[0].at[k], recv_sem=sems[1].at[k],
                                        device_id=to, device_id_type=MESH)


def _gather_first(shards):
    n = len(shards)

    def emit(srcs, new, alias, send_sems, recv_sems):
        x, y, c, chips = _place()
        me = 4 * x + 2 * y + c
        sems = (send_sems, recv_sems)
        copies = []
        for a in range(n):
            copies.append(pltpu.make_async_copy(srcs[a], new[a].at[me], send_sems.at[5 * a + 4]))
            copies.append(_remote(srcs[a], new[a].at[me], sems, 5 * a, (x, y, 1 - c)))
            for j, chip in enumerate(chips):
                copies.append(_remote(srcs[a], new[a].at[me], sems, 5 * a + 1 + j, (*chip, c)))
        return copies

    new = [jax.ShapeDtypeStruct((N_DEV,) + s.shape, s.dtype) for s in shards]
    return _Comm(shards, new, [], 5 * n, emit)


def _gather_second(bufs):
    n = len(bufs)

    def emit(srcs, new, alias, send_sems, recv_sems):
        x, y, c, chips = _place()
        sems = (send_sems, recv_sems)
        copies = []
        for a in range(n):
            for j, chip in enumerate(chips):
                blk = alias[a].at[4 * chip[0] + 2 * chip[1] + c]
                copies.append(_remote(blk, blk, sems, 3 * a + j, (x, y, 1 - c)))
        return copies

    return _Comm([], [], bufs, 3 * n, emit)


def _scatter_first(grads):
    n = len(grads)

    def emit(srcs, new, alias, send_sems, recv_sems):
        x, y, c, _ = _place()
        sems = (send_sems, recv_sems)
        return [_remote(srcs[a].at[2 * k + 1 - c], new[a].at[k], sems, 4 * a + k, (x, y, 1 - c))
                for a in range(n) for k in range(4)]

    new = [jax.ShapeDtypeStruct((4,) + g.shape[1:], g.dtype) for g in grads]
    return _Comm(grads, new, [], 4 * n, emit)


def _scatter_second(parts):
    n = len(parts)

    def emit(srcs, new, alias, send_sems, recv_sems):
        x, y, c, chips = _place()
        sems = (send_sems, recv_sems)
        return [_remote(srcs[a].at[2 * chip[0] + chip[1]], new[a].at[j], sems, 3 * a + j, (*chip, c))
                for a in range(n) for j, chip in enumerate(chips)]

    new = [jax.ShapeDtypeStruct((3,) + p.shape[1:], p.dtype) for p in parts]
    return _Comm(parts, new, [], 3 * n, emit)


def _pcall(body, *, name, grid, in_specs, out_specs, out_shape, args, sem, scratch=(), aliases=None, comm=None,
           prefetch=0):
    in_specs, out_specs, out_shape, scratch = list(in_specs), list(out_specs), list(out_shape), list(scratch)
    aliases = dict(aliases or {})
    n_in, n_out, n_scr = len(in_specs), len(out_shape), len(scratch)

    def make(body_fn, ins, outs, shapes, scr, sem_):
        params = pltpu.CompilerParams(dimension_semantics=sem_, vmem_limit_bytes=V7X_VMEM_LIMIT)
        if prefetch:
            spec = pltpu.PrefetchScalarGridSpec(num_scalar_prefetch=prefetch, grid=grid, in_specs=ins,
                                                out_specs=outs, scratch_shapes=scr)
            return pl.pallas_call(body_fn, name=name, grid_spec=spec, out_shape=shapes,
                                  input_output_aliases=aliases, compiler_params=params)
        return pl.pallas_call(body_fn, name=name, grid=grid, in_specs=ins, out_specs=outs, out_shape=shapes,
                              scratch_shapes=scr, input_output_aliases=aliases, compiler_params=params)

    if comm is None:
        return list(make(body, in_specs, out_specs, out_shape, scratch, sem)(*args)), []

    n_src, n_new, n_al = len(comm.srcs), len(comm.new), len(comm.alias)
    for a in range(n_al):
        aliases[prefetch + n_in + n_src + a] = n_out + n_new + a

    def wrapped(*refs):
        pre, refs = refs[:prefetch], refs[prefetch:]
        ins = refs[:n_in]
        src_refs = refs[n_in:n_in + n_src]
        o0 = n_in + n_src + n_al
        outs = refs[o0:o0 + n_out]
        new_refs = refs[o0 + n_out:o0 + n_out + n_new]
        alias_refs = refs[o0 + n_out + n_new:o0 + n_out + n_new + n_al]
        s0 = o0 + n_out + n_new + n_al
        scr = refs[s0:s0 + n_scr]
        send_sems, recv_sems = refs[s0 + n_scr], refs[s0 + n_scr + 1]
        first = pl.program_id(0) == 0
        last = pl.program_id(0) == grid[0] - 1
        for dim in range(1, len(grid)):
            first = first & (pl.program_id(dim) == 0)
            last = last & (pl.program_id(dim) == grid[dim] - 1)

        @pl.when(first)
        def _():
            for cp in comm.emit(src_refs, new_refs, alias_refs, send_sems, recv_sems):
                cp.start()

        body(*pre, *ins, *outs, *scr)

        @pl.when(last)
        def _():
            for cp in comm.emit(src_refs, new_refs, alias_refs, send_sems, recv_sems):
                cp.wait()

    call = make(wrapped, in_specs + [ANY] * (n_src + n_al), out_specs + [ANY] * (n_new + n_al),
                out_shape + comm.new + [jax.ShapeDtypeStruct(b.shape, b.dtype) for b in comm.alias],
                scratch + [pltpu.SemaphoreType.DMA((comm.n_copies,)), pltpu.SemaphoreType.DMA((comm.n_copies,))],
                ("arbitrary",) * len(grid))
    res = list(call(*args, *comm.srcs, *comm.alias))
    return res[:n_out], res[n_out:]


def _comm_only(comm, *, name):
    def body(o_ref):
        o_ref[...] = jnp.zeros_like(o_ref)

    _, couts = _pcall(body, name=name, grid=(1,), in_specs=[], out_specs=[pl.BlockSpec((8, 128), lambda i: (0, 0))],
                      out_shape=[jax.ShapeDtypeStruct((8, 128), F32)], args=(), sem=("arbitrary",), comm=comm)
    return couts


def _mm(a, b, *, ta=False, tb=False, out_dtype=BF16, res=None, name, b_cols=False, b_off=0,
        out_cols=None, out_alias=None, comm=None):
    m, k = (a.shape[1], a.shape[0]) if ta else a.shape
    tm = _tile(m, 1024)
    if b_cols and not tb:
        ns = b.shape[2]
        assert b.shape[1] == k
        n, tn, tk = b.shape[0] * ns, ns, _tile(k, 1024)
        b_spec = pl.BlockSpec((None, tk, ns), lambda i, j, kk: (j, kk, 0))
    elif b_cols:
        ns = b.shape[2]
        assert k % ns == 0
        n, tk = b.shape[1], ns
        tn = _tile(n, 1024)
        b_spec = pl.BlockSpec((None, tn, ns), lambda i, j, kk: (b_off + kk, j, 0))
    else:
        n = b.shape[0] if tb else b.shape[1]
        assert (b.shape[1] if tb else b.shape[0]) == k
        tk = _tile(k, 1024)
        tn = out_cols[2] if out_cols is not None else _tile(n, 1024)
        b_spec = pl.BlockSpec((tn, tk), lambda i, j, kk: (j, kk)) if tb else pl.BlockSpec((tk, tn), lambda i, j, kk: (kk, j))
    nk = k // tk
    dn = (((0,) if ta else (1,), (1,) if tb else (0,)), ((), ()))
    n_extra = (res is not None) + (out_alias is not None)

    def body(*refs):
        a_ref, b_ref = refs[0], refs[1]
        r_ref = refs[2] if res is not None else None
        o_ref = refs[2 + n_extra]
        part = lax.dot_general(a_ref[...], b_ref[...], dn, preferred_element_type=F32)

        def finish(r):
            if r_ref is not None:
                r = r + r_ref[...]
            o_ref[...] = r.astype(out_dtype)

        if nk == 1:
            finish(part)
        else:
            acc_ref = refs[3 + n_extra]
            kk = pl.program_id(2)

            @pl.when(kk == 0)
            def _():
                acc_ref[...] = part

            @pl.when(kk > 0)
            def _():
                acc_ref[...] += part

            @pl.when(kk == nk - 1)
            def _():
                finish(acc_ref[...])

    a_spec = pl.BlockSpec((tk, tm), lambda i, j, kk: (kk, i)) if ta else pl.BlockSpec((tm, tk), lambda i, j, kk: (i, kk))
    in_specs, args, aliases = [a_spec, b_spec], [a, b], {}
    if res is not None:
        in_specs.append(pl.BlockSpec((tm, tn), lambda i, j, kk: (i, j)))
        args.append(res)
    if out_cols is not None:
        s_total, o_off, ns_o = out_cols
        assert tn == ns_o and n % ns_o == 0
        o_spec = pl.BlockSpec((None, tm, tn), lambda i, j, kk: (o_off + j, i, 0))
        o_shape = jax.ShapeDtypeStruct((s_total, m, tn), out_dtype)
        if out_alias is not None:
            in_specs.append(ANY)
            args.append(out_alias)
            aliases[len(args) - 1] = 0
    else:
        o_spec = pl.BlockSpec((tm, tn), lambda i, j, kk: (i, j))
        o_shape = jax.ShapeDtypeStruct((m, n), out_dtype)
    outs, couts = _pcall(body, name=name, grid=(m // tm, n // tn, nk), in_specs=in_specs, out_specs=[o_spec],
                         out_shape=[o_shape], args=args, sem=("parallel", "parallel", "arbitrary"),
                         scratch=[pltpu.VMEM((tm, tn), F32)] if nk > 1 else [], aliases=aliases, comm=comm)
    return (outs[0], couts) if comm is not None else outs[0]


def _rms_fwd(x, g, *, name):
    t, d = x.shape
    tr = _tile(t, 256)

    def body(x_ref, g_ref, h_ref):
        xf = x_ref[...]
        inv = lax.rsqrt(jnp.mean(xf * xf, axis=-1, keepdims=True) + RMS_EPS)
        h_ref[...] = ((xf * inv) * g_ref[...]).astype(BF16)

    row = pl.BlockSpec((tr, d), lambda i: (i, 0))
    return _pcall(body, name=name, grid=(t // tr,), in_specs=[row, pl.BlockSpec((1, d), lambda i: (0, 0))],
                  out_specs=[row], out_shape=[jax.ShapeDtypeStruct((t, d), BF16)], args=(x, g),
                  sem=("parallel",))[0][0]


def _rms_bwd(x, g, dh, dres, *, name):
    t, d = x.shape
    tr = _tile(t, 256)

    def body(x_ref, g_ref, dh_ref, dres_ref, dx_ref, dxb_ref, dg_ref):
        xf = x_ref[...]
        inv = lax.rsqrt(jnp.mean(xf * xf, axis=-1, keepdims=True) + RMS_EPS)
        xn = xf * inv
        dh_f = dh_ref[...].astype(F32)
        dxn = dh_f * g_ref[...]
        dx = dres_ref[...] + inv * (dxn - xn * jnp.mean(dxn * xn, axis=-1, keepdims=True))
        dx_ref[...] = dx
        dxb_ref[...] = dx.astype(BF16)
        part = jnp.sum(dh_f * xn, axis=0, keepdims=True)

        @pl.when(pl.program_id(0) == 0)
        def _():
            dg_ref[...] = part

        @pl.when(pl.program_id(0) > 0)
        def _():
            dg_ref[...] += part

    row = pl.BlockSpec((tr, d), lambda i: (i, 0))
    vec = pl.BlockSpec((1, d), lambda i: (0, 0))
    return _pcall(body, name=name, grid=(t // tr,), in_specs=[row, vec, row, row], out_specs=[row, row, vec],
                  out_shape=[jax.ShapeDtypeStruct((t, d), F32), jax.ShapeDtypeStruct((t, d), BF16),
                             jax.ShapeDtypeStruct((1, d), F32)], args=(x, g, dh, dres), sem=("arbitrary",))[0]


def _final_loss(x, g, target, *, name):
    t, d = x.shape
    tr = _tile(t, 256)

    def body(x_ref, g_ref, tg_ref, loss_ref, dx_ref, dxb_ref, dg_ref):
        xf = x_ref[...]
        gv = g_ref[...]
        inv = lax.rsqrt(jnp.mean(xf * xf, axis=-1, keepdims=True) + RMS_EPS)
        xn = xf * inv
        err = xn * gv - tg_ref[...]
        lpart = 0.5 * jnp.sum(jnp.mean(err * err, axis=-1, keepdims=True), axis=0, keepdims=True)
        dy = err * (1.0 / d)
        dxn = dy * gv
        dx = inv * (dxn - xn * jnp.mean(dxn * xn, axis=-1, keepdims=True))
        dx_ref[...] = dx
        dxb_ref[...] = dx.astype(BF16)
        gpart = jnp.sum(dy * xn, axis=0, keepdims=True)
        lrow = jnp.broadcast_to(lpart, (1, 128))

        @pl.when(pl.program_id(0) == 0)
        def _():
            dg_ref[...] = gpart
            loss_ref[...] = lrow

        @pl.when(pl.program_id(0) > 0)
        def _():
            dg_ref[...] += gpart
            loss_ref[...] += lrow

    row = pl.BlockSpec((tr, d), lambda i: (i, 0))
    vec = pl.BlockSpec((1, d), lambda i: (0, 0))
    lspec = pl.BlockSpec((1, 128), lambda i: (0, 0))
    return _pcall(body, name=name, grid=(t // tr,), in_specs=[row, vec, row], out_specs=[lspec, row, row, vec],
                  out_shape=[jax.ShapeDtypeStruct((1, 128), F32), jax.ShapeDtypeStruct((t, d), F32),
                             jax.ShapeDtypeStruct((t, d), BF16), jax.ShapeDtypeStruct((1, d), F32)],
                  args=(x, g, target), sem=("arbitrary",))[0]


def _tri_ones(n, upper):
    r = lax.broadcasted_iota(jnp.int32, (n, n), 0)
    c = lax.broadcasted_iota(jnp.int32, (n, n), 1)
    return jnp.where((r <= c) if upper else (r >= c), 1.0, 0.0).astype(F32)


def _forget_fwd(flog, bpad, *, name):
    t = flog.shape[0]
    tb = _tile(t, 512)

    def body(f_ref, b_ref, c_ref, carry):
        z = f_ref[...] + b_ref[...]
        lf = jnp.minimum(z, 0.0) - jnp.log(1.0 + jnp.exp(-jnp.abs(z)))
        lft = lf.T
        tri = _tri_ones(tb, upper=True)

        @pl.when(pl.program_id(0) == 0)
        def _():
            carry[...] = jnp.zeros_like(carry)

        cs = jnp.dot(lft, tri, preferred_element_type=F32, precision=lax.Precision.HIGHEST) + carry[:, 0:1]
        c_ref[...] = cs
        carry[...] = jnp.broadcast_to(cs[:, tb - 1:tb], carry.shape)

    return _pcall(body, name=name, grid=(t // tb,),
                  in_specs=[pl.BlockSpec((tb, 128), lambda i: (i, 0)), pl.BlockSpec((1, 128), lambda i: (0, 0))],
                  out_specs=[pl.BlockSpec((128, tb), lambda i: (0, i))],
                  out_shape=[jax.ShapeDtypeStruct((128, t), F32)], args=(flog, bpad), sem=("arbitrary",),
                  scratch=[pltpu.VMEM((128, 128), F32)])[0][0]


def _forget_bwd(flog, bpad, dct, *, name):
    t = flog.shape[0]
    tb = _tile(t, 512)
    nb = t // tb

    def body(f_ref, b_ref, dc_ref, df_ref, db_ref, carry):
        i = pl.program_id(0)

        @pl.when(i == 0)
        def _():
            carry[...] = jnp.zeros_like(carry)

        tri = _tri_ones(tb, upper=False)
        dl = jnp.dot(dc_ref[...], tri, preferred_element_type=F32, precision=lax.Precision.HIGHEST) + carry[:, 0:1]
        carry[...] = jnp.broadcast_to(dl[:, 0:1], carry.shape)
        z = f_ref[...] + b_ref[...]
        df = dl.T * _sigmoid(-z)
        df_ref[...] = df.astype(BF16)
        part = jnp.sum(df, axis=0, keepdims=True)

        @pl.when(i == 0)
        def _():
            db_ref[...] = part

        @pl.when(i > 0)
        def _():
            db_ref[...] += part

    rev = lambda i: (nb - 1 - i, 0)
    return _pcall(body, name=name, grid=(nb,),
                  in_specs=[pl.BlockSpec((tb, 128), rev), pl.BlockSpec((1, 128), lambda i: (0, 0)),
                            pl.BlockSpec((128, tb), lambda i: (0, nb - 1 - i))],
                  out_specs=[pl.BlockSpec((tb, 128), rev), pl.BlockSpec((1, 128), lambda i: (0, 0))],
                  out_shape=[jax.ShapeDtypeStruct((t, 128), BF16), jax.ShapeDtypeStruct((1, 128), F32)],
                  args=(flog, bpad, dct), sem=("arbitrary",), scratch=[pltpu.VMEM((128, 128), F32)])[0]


def _causal_pairs(nq, k_major):
    if k_major:
        pairs = [(i, j) for j in range(nq) for i in range(j, nq)]
    else:
        pairs = [(i, j) for i in range(nq) for j in range(i + 1)]
    return (jnp.asarray(np.array([p[0] for p in pairs], np.int32)),
            jnp.asarray(np.array([p[1] for p in pairs], np.int32)), len(pairs))


def _logits(q, k, cq_ref, ck_ref, scale, masked):
    s = lax.dot_general(q, k, (((1,), (1,)), ((), ())), preferred_element_type=F32)
    s = s * scale + (cq_ref[:, 0:1] - ck_ref[...])
    if masked:
        row = lax.broadcasted_iota(jnp.int32, s.shape, 0)
        col = lax.broadcasted_iota(jnp.int32, s.shape, 1)
        s = jnp.where(col <= row, s, NEG)
    return s


def _fox_fwd(qkv, c3, heads, *, name, comm=None):
    t = qkv.shape[0]
    tq = _tile(t, 512)
    nq = t // tq
    scale = HEAD_DIM ** -0.5
    i_tab, j_tab, npairs = _causal_pairs(nq, k_major=False)

    def body(it_ref, jt_ref, q_ref, k_ref, v_ref, cq_ref, ck_ref, o_ref, o32_ref, lse_ref, m_s, l_s, acc_s):
        p_id = pl.program_id(1)
        i, j = it_ref[p_id], jt_ref[p_id]

        @pl.when(j == 0)
        def _():
            m_s[...] = jnp.full_like(m_s, NEG)
            l_s[...] = jnp.zeros_like(l_s)
            acc_s[...] = jnp.zeros_like(acc_s)

        def update(masked):
            s = _logits(q_ref[...], k_ref[...], cq_ref, ck_ref, scale, masked)
            m_prev = m_s[:, 0:1]
            m_new = jnp.maximum(m_prev, jnp.max(s, axis=1, keepdims=True))
            alpha = jnp.exp(m_prev - m_new)
            p = jnp.exp(s - m_new)
            l_new = alpha * l_s[:, 0:1] + jnp.sum(p, axis=1, keepdims=True)
            p_hi = p.astype(BF16)
            p_lo = (p - p_hi.astype(F32)).astype(BF16)
            vb = v_ref[...]
            pv = jnp.dot(p_hi, vb, preferred_element_type=F32) + jnp.dot(p_lo, vb, preferred_element_type=F32)
            acc_s[...] = alpha * acc_s[...] + pv
            m_s[...] = jnp.broadcast_to(m_new, m_s.shape)
            l_s[...] = jnp.broadcast_to(l_new, l_s.shape)

        @pl.when(j < i)
        def _():
            update(False)

        @pl.when(j == i)
        def _():
            update(True)
            l = l_s[...]
            o = acc_s[...] / l
            o_ref[...] = o.astype(BF16)
            o32_ref[...] = o
            lse_ref[...] = m_s[...] + jnp.log(l)

    qb = lambda h, p, it, jt: (it[p], h)
    outs, couts = _pcall(
        body, name=name, grid=(heads, npairs), prefetch=2,
        in_specs=[pl.BlockSpec((tq, 128), qb),
                  pl.BlockSpec((tq, 128), lambda h, p, it, jt: (jt[p], heads + h)),
                  pl.BlockSpec((tq, 128), lambda h, p, it, jt: (jt[p], 2 * heads + h)),
                  pl.BlockSpec((None, 1, tq), lambda h, p, it, jt: (h, 0, it[p])),
                  pl.BlockSpec((None, 1, tq), lambda h, p, it, jt: (h, 0, jt[p]))],
        out_specs=[pl.BlockSpec((tq, 128), qb), pl.BlockSpec((tq, 128), qb),
                   pl.BlockSpec((None, tq, 128), lambda h, p, it, jt: (h, it[p], 0))],
        out_shape=[jax.ShapeDtypeStruct((t, heads * 128), BF16), jax.ShapeDtypeStruct((t, heads * 128), F32),
                   jax.ShapeDtypeStruct((heads, t, 128), F32)],
        args=(i_tab, j_tab, qkv, qkv, qkv, c3, c3), sem=("parallel", "arbitrary"),
        scratch=[pltpu.VMEM((tq, 128), F32), pltpu.VMEM((tq, 128), F32), pltpu.VMEM((tq, 128), F32)], comm=comm)
    return outs, couts


def _fox_bwd(qkv, c3, o, do, lse, heads, *, name, comm=None):
    t = qkv.shape[0]
    tq = _tile(t, 512)
    nq = t // tq
    scale = HEAD_DIM ** -0.5
    i_tab, j_tab, npairs = _causal_pairs(nq, k_major=True)

    def body(it_ref, jt_ref, q_ref, k_ref, v_ref, o_ref, do_ref, lse_ref, cq_ref, ck_ref,
             dq_ref, dk_ref, dv_ref, dc_ref, dq_s, dk_s, dv_s, dc_s):
        p_id = pl.program_id(1)
        i, j = it_ref[p_id], jt_ref[p_id]

        @pl.when(p_id == 0)
        def _():
            dq_s[...] = jnp.zeros_like(dq_s)

        @pl.when(i == j)
        def _():
            dk_s[...] = jnp.zeros_like(dk_s)
            dv_s[...] = jnp.zeros_like(dv_s)
            dc_s[...] = jnp.zeros_like(dc_s)

        def update(masked):
            q, k, v, dob = q_ref[...], k_ref[...], v_ref[...], do_ref[...]
            s = _logits(q, k, cq_ref, ck_ref, scale, masked)
            p = jnp.exp(s - lse_ref[:, 0:1])
            delta = jnp.sum(dob.astype(F32) * o_ref[...], axis=1, keepdims=True)
            dp = lax.dot_general(dob, v, (((1,), (1,)), ((), ())), preferred_element_type=F32)
            ds = p * (dp - delta)
            pb, dsb = p.astype(BF16), ds.astype(BF16)
            dv_s[...] += lax.dot_general(pb, dob, (((0,), (0,)), ((), ())), preferred_element_type=F32)
            dk_s[...] += lax.dot_general(dsb, q, (((0,), (0,)), ((), ())), preferred_element_type=F32)
            r0 = pl.multiple_of(i * tq, tq)
            dq_s[pl.ds(r0, tq), :] += jnp.dot(dsb, k, preferred_element_type=F32) * scale
            dc_s[...] -= jnp.sum(ds, axis=0, keepdims=True)

        @pl.when(i > j)
        def _():
            update(False)

        @pl.when(i == j)
        def _():
            update(True)

        @pl.when(i == nq - 1)
        def _():
            dk_ref[...] = (dk_s[...] * scale).astype(BF16)
            dv_ref[...] = dv_s[...].astype(BF16)
            dc_ref[...] = dc_s[...]

        @pl.when(p_id == npairs - 1)
        def _():
            dq_ref[...] = dq_s[...].astype(BF16)

    qb = lambda h, p, it, jt: (it[p], h)
    kb = lambda h, p, it, jt: (jt[p], h)
    outs, couts = _pcall(
        body, name=name, grid=(heads, npairs), prefetch=2,
        in_specs=[pl.BlockSpec((tq, 128), qb),
                  pl.BlockSpec((tq, 128), lambda h, p, it, jt: (jt[p], heads + h)),
                  pl.BlockSpec((tq, 128), lambda h, p, it, jt: (jt[p], 2 * heads + h)),
                  pl.BlockSpec((tq, 128), qb), pl.BlockSpec((tq, 128), qb),
                  pl.BlockSpec((None, tq, 128), lambda h, p, it, jt: (h, it[p], 0)),
                  pl.BlockSpec((None, 1, tq), lambda h, p, it, jt: (h, 0, it[p])),
                  pl.BlockSpec((None, 1, tq), lambda h, p, it, jt: (h, 0, jt[p]))],
        out_specs=[pl.BlockSpec((t, 128), lambda h, p, it, jt: (0, h)), pl.BlockSpec((tq, 128), kb),
                   pl.BlockSpec((tq, 128), kb), pl.BlockSpec((None, 1, tq), lambda h, p, it, jt: (h, 0, jt[p]))],
        out_shape=[jax.ShapeDtypeStruct((t, heads * 128), BF16), jax.ShapeDtypeStruct((t, heads * 128), BF16),
                   jax.ShapeDtypeStruct((t, heads * 128), BF16), jax.ShapeDtypeStruct((heads, 1, t), F32)],
        args=(i_tab, j_tab, qkv, qkv, qkv, o, do, lse, c3, c3), sem=("arbitrary", "arbitrary"),
        scratch=[pltpu.VMEM((t, 128), F32), pltpu.VMEM((tq, 128), F32), pltpu.VMEM((tq, 128), F32),
                 pltpu.VMEM((1, tq), F32)], comm=comm)
    return outs, couts


def _tril_mask():
    r = lax.broadcasted_iota(jnp.int32, (128, 128), 0)
    c = lax.broadcasted_iota(jnp.int32, (128, 128), 1)
    return r >= c


def _sgu_fwd(uv, g, w, bst, *, name):
    t = uv.shape[0]
    sw = uv.shape[1] // 2
    groups = sw // 128
    tr = _tile(t, 512)

    def body(u_ref, v_ref, g_ref, w_ref, b_ref, y_ref):
        gv = _gelu(v_ref[...].astype(F32))
        inv = lax.rsqrt(jnp.mean(gv * gv, axis=-1, keepdims=True) + RMS_EPS)
        vn = ((gv * inv) * g_ref[...]).astype(BF16)
        gu = _gelu(u_ref[...].astype(F32))
        mask = _tril_mask()
        for gi in range(groups):
            wg = jnp.where(mask, w_ref[gi], 0.0).astype(BF16)
            bcol = b_ref[:, gi:gi + 1]
            cs = slice(gi * 128, (gi + 1) * 128)
            for ci in range(tr // 128):
                rs = slice(ci * 128, (ci + 1) * 128)
                mixed = jnp.dot(wg, vn[rs, cs], preferred_element_type=F32) + bcol
                y_ref[rs, cs] = (gu[rs, cs] * mixed).astype(BF16)

    return _pcall(body, name=name, grid=(t // tr,),
                  in_specs=[pl.BlockSpec((tr, sw), lambda i: (i, 0)), pl.BlockSpec((tr, sw), lambda i: (i, 1)),
                            pl.BlockSpec((1, sw), lambda i: (0, 0)),
                            pl.BlockSpec((groups, 128, 128), lambda i: (0, 0, 0)),
                            pl.BlockSpec((128, 128), lambda i: (0, 0))],
                  out_specs=[pl.BlockSpec((tr, sw), lambda i: (i, 0))],
                  out_shape=[jax.ShapeDtypeStruct((t, sw), BF16)], args=(uv, uv, g, w, bst),
                  sem=("parallel",))[0][0]


def _sgu_bwd(uv, g, w, bst, dy, *, name):
    t = uv.shape[0]
    sw = uv.shape[1] // 2
    groups = sw // 128
    tr = _tile(t, 256)
    nsteps = t // tr

    def body(u_ref, v_ref, g_ref, w_ref, b_ref, dy_ref, duv_ref, dw_ref, db_ref, dg_ref, dvn_s, dgu_s):
        step = pl.program_id(0)

        @pl.when(step == 0)
        def _():
            dw_ref[...] = jnp.zeros_like(dw_ref)
            db_ref[...] = jnp.zeros_like(db_ref)
            dg_ref[...] = jnp.zeros_like(dg_ref)

        vf = v_ref[...].astype(F32)
        gv, gv_grad = _gelu_and_grad(vf)
        inv = lax.rsqrt(jnp.mean(gv * gv, axis=-1, keepdims=True) + RMS_EPS)
        xn = gv * inv
        gvec = g_ref[...]
        vn = (xn * gvec).astype(BF16)
        uf = u_ref[...].astype(F32)
        gu, gu_grad = _gelu_and_grad(uf)
        dyf = dy_ref[...].astype(F32)
        mask = _tril_mask()
        lane = lax.broadcasted_iota(jnp.int32, (128, 128), 1)
        dball = jnp.zeros((128, 128), F32)
        for gi in range(groups):
            wg = jnp.where(mask, w_ref[gi], 0.0).astype(BF16)
            wgt = wg.T
            bcol = b_ref[:, gi:gi + 1]
            cs = slice(gi * 128, (gi + 1) * 128)
            dwg = jnp.zeros((128, 128), F32)
            dbg = jnp.zeros((128, 1), F32)
            for ci in range(tr // 128):
                rs = slice(ci * 128, (ci + 1) * 128)
                vnb = vn[rs, cs]
                mixed = jnp.dot(wg, vnb, preferred_element_type=F32) + bcol
                dgu_s[rs, cs] = dyf[rs, cs] * mixed
                dmix = dyf[rs, cs] * gu[rs, cs]
                dmb = dmix.astype(BF16)
                dvn_s[rs, cs] = jnp.dot(wgt, dmb, preferred_element_type=F32)
                dwg = dwg + lax.dot_general(dmb, vnb, (((1,), (1,)), ((), ())), preferred_element_type=F32)
                dbg = dbg + jnp.sum(dmix, axis=1, keepdims=True)
            dw_ref[gi] += dwg
            dball = dball + jnp.where(lane == gi, dbg, 0.0)
        db_ref[...] += dball
        dvn = dvn_s[...]
        dg_ref[...] += jnp.sum(dvn * xn, axis=0, keepdims=True)
        dxn = dvn * gvec
        dgv = inv * (dxn - xn * jnp.mean(dxn * xn, axis=-1, keepdims=True))
        duv_ref[:, 0:sw] = (dgu_s[...] * gu_grad).astype(BF16)
        duv_ref[:, sw:2 * sw] = (dgv * gv_grad).astype(BF16)

        @pl.when(step == nsteps - 1)
        def _():
            for gi in range(groups):
                dw_ref[gi] = jnp.where(mask, dw_ref[gi], 0.0)

    return _pcall(body, name=name, grid=(nsteps,),
                  in_specs=[pl.BlockSpec((tr, sw), lambda i: (i, 0)), pl.BlockSpec((tr, sw), lambda i: (i, 1)),
                            pl.BlockSpec((1, sw), lambda i: (0, 0)),
                            pl.BlockSpec((groups, 128, 128), lambda i: (0, 0, 0)),
                            pl.BlockSpec((128, 128), lambda i: (0, 0)), pl.BlockSpec((tr, sw), lambda i: (i, 0))],
                  out_specs=[pl.BlockSpec((tr, 2 * sw), lambda i: (i, 0)),
                             pl.BlockSpec((groups, 128, 128), lambda i: (0, 0, 0)),
                             pl.BlockSpec((128, 128), lambda i: (0, 0)), pl.BlockSpec((1, sw), lambda i: (0, 0))],
                  out_shape=[jax.ShapeDtypeStruct((t, 2 * sw), BF16), jax.ShapeDtypeStruct((groups, 128, 128), F32),
                             jax.ShapeDtypeStruct((128, 128), F32), jax.ShapeDtypeStruct((1, sw), F32)],
                  args=(uv, uv, g, w, bst, dy), sem=("arbitrary",),
                  scratch=[pltpu.VMEM((tr, sw), F32), pltpu.VMEM((tr, sw), F32)])[0]


def _merge_fwd(ya, yb, wa, wb, gates, *, name, comm=None):
    t, kdim = ya.shape
    nsh, _, ns = wa.shape
    d = nsh * ns
    tm = _tile(t, 1024)

    def body(ya_ref, yb_ref, wa_ref, wb_ref, ga_ref, gb_ref, mg_ref, za_ref, zb_ref):
        za = jnp.dot(ya_ref[...], wa_ref[...], preferred_element_type=F32)
        zb = jnp.dot(yb_ref[...], wb_ref[...], preferred_element_type=F32)
        sa = _sigmoid(ga_ref[...].astype(F32))
        sb = _sigmoid(gb_ref[...].astype(F32))
        mg_ref[...] = (sa * za + sb * zb).astype(BF16)
        za_ref[...] = za.astype(BF16)
        zb_ref[...] = zb.astype(BF16)

    yspec = pl.BlockSpec((tm, kdim), lambda i, j: (i, 0))
    wspec = pl.BlockSpec((None, kdim, ns), lambda i, j: (j, 0, 0))
    ospec = pl.BlockSpec((tm, ns), lambda i, j: (i, j))
    outs, couts = _pcall(body, name=name, grid=(t // tm, nsh),
                         in_specs=[yspec, yspec, wspec, wspec, ospec, pl.BlockSpec((tm, ns), lambda i, j: (i, nsh + j))],
                         out_specs=[ospec, ospec, ospec], out_shape=[jax.ShapeDtypeStruct((t, d), BF16)] * 3,
                         args=(ya, yb, wa, wb, gates, gates), sem=("parallel", "parallel"), comm=comm)
    return outs, couts


def _merge_bwd(dmg, gates, za, zb, *, name):
    t, d = dmg.shape
    tr = _tile(t, 256)

    def body(dm_ref, ga_ref, gb_ref, za_ref, zb_ref, dza_ref, dzb_ref, dg_ref):
        dm = dm_ref[...].astype(F32)
        sa = _sigmoid(ga_ref[...].astype(F32))
        sb = _sigmoid(gb_ref[...].astype(F32))
        dza_ref[...] = (dm * sa).astype(BF16)
        dzb_ref[...] = (dm * sb).astype(BF16)
        dg_ref[:, 0:d] = (dm * za_ref[...].astype(F32) * (sa * (1.0 - sa))).astype(BF16)
        dg_ref[:, d:2 * d] = (dm * zb_ref[...].astype(F32) * (sb * (1.0 - sb))).astype(BF16)

    row = pl.BlockSpec((tr, d), lambda i: (i, 0))
    return _pcall(body, name=name, grid=(t // tr,),
                  in_specs=[row, row, pl.BlockSpec((tr, d), lambda i: (i, 1)), row, row],
                  out_specs=[row, row, pl.BlockSpec((tr, 2 * d), lambda i: (i, 0))],
                  out_shape=[jax.ShapeDtypeStruct((t, d), BF16), jax.ShapeDtypeStruct((t, d), BF16),
                             jax.ShapeDtypeStruct((t, 2 * d), BF16)],
                  args=(dmg, gates, gates, za, zb), sem=("parallel",))[0]


def _shift_down(ext, k, rows):
    return pltpu.roll(ext, k, 0)[8:8 + rows]


def _conv_fwd(ab, cw, cb, *, name):
    t = ab.shape[0]
    dff = ab.shape[1] // 2
    tr, tc = _tile(t, 512), _tile(dff, 512)
    nc = dff // tc
    r8 = tr // 8

    def body(a_ref, ap_ref, b_ref, cw_ref, cb_ref, g_ref):
        i = pl.program_id(0)
        prev = ap_ref[...].astype(F32) * jnp.where(i > 0, 1.0, 0.0)
        a = a_ref[...].astype(F32)
        ext = jnp.concatenate([prev, a], axis=0)
        acc = cb_ref[...] + cw_ref[0:1, :] * _shift_down(ext, 2, tr) + cw_ref[1:2, :] * _shift_down(ext, 1, tr) \
            + cw_ref[2:3, :] * a
        g_ref[...] = (_gelu(acc) * b_ref[...].astype(F32)).astype(BF16)

    return _pcall(body, name=name, grid=(t // tr, nc),
                  in_specs=[pl.BlockSpec((tr, tc), lambda i, j: (i, j)),
                            pl.BlockSpec((8, tc), lambda i, j: (jnp.maximum(i * r8 - 1, 0), j)),
                            pl.BlockSpec((tr, tc), lambda i, j: (i, nc + j)),
                            pl.BlockSpec((CONV_WIDTH, tc), lambda i, j: (0, j)),
                            pl.BlockSpec((1, tc), lambda i, j: (0, j))],
                  out_specs=[pl.BlockSpec((tr, tc), lambda i, j: (i, j))],
                  out_shape=[jax.ShapeDtypeStruct((t, dff), BF16)], args=(ab, ab, ab, cw, cb),
                  sem=("parallel", "parallel"))[0][0]


def _conv_bwd(ab, cw, cb, dgg, *, name):
    t = ab.shape[0]
    dff = ab.shape[1] // 2
    tr, tc = _tile(t, 256), _tile(dff, 512)
    nc, nr = dff // tc, t // tr
    r8 = tr // 8
    ext_rows = tr + 16

    def body(a_ref, ap_ref, an_ref, b_ref, bn_ref, dg_ref, dgn_ref, cw_ref, cb_ref,
             da_ref, db_ref, dcw_ref, dcb_ref):
        i = pl.program_id(1)
        has_prev = jnp.where(i > 0, 1.0, 0.0)
        has_next = jnp.where(i < nr - 1, 1.0, 0.0)
        a_ext = jnp.concatenate([ap_ref[...].astype(F32) * has_prev, a_ref[...].astype(F32),
                                 an_ref[...].astype(F32)], axis=0)
        b_ext = jnp.concatenate([b_ref[0:8, :].astype(F32), b_ref[...].astype(F32), bn_ref[...].astype(F32)], axis=0)
        dg_ext = jnp.concatenate([dg_ref[0:8, :].astype(F32), dg_ref[...].astype(F32),
                                  dgn_ref[...].astype(F32) * has_next], axis=0)
        w0, w1, w2 = cw_ref[0:1, :], cw_ref[1:2, :], cw_ref[2:3, :]
        a_m2 = pltpu.roll(a_ext, 2, 0)
        a_m1 = pltpu.roll(a_ext, 1, 0)
        acc = cb_ref[...] + w0 * a_m2 + w1 * a_m1 + w2 * a_ext
        gel, gel_grad = _gelu_and_grad(acc)
        dacc = dg_ext * b_ext * gel_grad
        d_a = w2 * dacc + w1 * pltpu.roll(dacc, ext_rows - 1, 0) + w0 * pltpu.roll(dacc, ext_rows - 2, 0)
        da_ref[...] = d_a[8:8 + tr].astype(BF16)
        db_ref[...] = (dg_ext[8:8 + tr] * gel[8:8 + tr]).astype(BF16)
        dm = dacc[8:8 + tr]
        dcw = jnp.concatenate([jnp.sum(dm * a_m2[8:8 + tr], axis=0, keepdims=True),
                               jnp.sum(dm * a_m1[8:8 + tr], axis=0, keepdims=True),
                               jnp.sum(dm * a_ext[8:8 + tr], axis=0, keepdims=True)], axis=0)
        dcb = jnp.sum(dm, axis=0, keepdims=True)

        @pl.when(i == 0)
        def _():
            dcw_ref[...] = dcw
            dcb_ref[...] = dcb

        @pl.when(i > 0)
        def _():
            dcw_ref[...] += dcw
            dcb_ref[...] += dcb

    cur = lambda off: pl.BlockSpec((tr, tc), lambda j, i: (i, off + j))
    prv = lambda off: pl.BlockSpec((8, tc), lambda j, i: (jnp.maximum(i * r8 - 1, 0), off + j))
    nxt = lambda off: pl.BlockSpec((8, tc), lambda j, i: (jnp.minimum((i + 1) * r8, nr * r8 - 1), off + j))
    return _pcall(body, name=name, grid=(nc, nr),
                  in_specs=[cur(0), prv(0), nxt(0), cur(nc), nxt(nc), cur(0), nxt(0),
                            pl.BlockSpec((CONV_WIDTH, tc), lambda j, i: (0, j)),
                            pl.BlockSpec((1, tc), lambda j, i: (0, j))],
                  out_specs=[cur(0), cur(0), pl.BlockSpec((CONV_WIDTH, tc), lambda j, i: (0, j)),
                             pl.BlockSpec((1, tc), lambda j, i: (0, j))],
                  out_shape=[jax.ShapeDtypeStruct((t, dff), BF16), jax.ShapeDtypeStruct((t, dff), BF16),
                             jax.ShapeDtypeStruct((CONV_WIDTH, dff), F32), jax.ShapeDtypeStruct((1, dff), F32)],
                  args=(ab, ab, ab, ab, ab, dgg, dgg, cw, cb), sem=("parallel", "arbitrary"))[0]


def _row_tile(rows, cols):
    for cand in (512, 256, 128, 64, 32, 16, 8):
        if rows % cand == 0 and cand * cols * 4 <= 2 * 1024 * 1024:
            return cand
    return rows


def _pair_add(grad8, recv4, cidx, *, name):
    _, rows, cols = grad8.shape
    tr = _row_tile(rows, cols)

    def body(c_ref, g_ref, r_ref, o_ref):
        o_ref[...] = (g_ref[...].astype(F32) + r_ref[...].astype(F32)).astype(BF16)

    blk = (None, tr, cols)
    return _pcall(body, name=name, grid=(4, rows // tr), prefetch=1,
                  in_specs=[pl.BlockSpec(blk, lambda k, i, c_ref: (2 * k + c_ref[0], i, 0)),
                            pl.BlockSpec(blk, lambda k, i, c_ref: (k, i, 0))],
                  out_specs=[pl.BlockSpec(blk, lambda k, i, c_ref: (k, i, 0))],
                  out_shape=[jax.ShapeDtypeStruct((4, rows, cols), BF16)], args=(cidx, grad8, recv4),
                  sem=("parallel", "parallel"))[0][0]


def _adamw_math(w, g, m, v):
    m = ADAM_B1 * m + (1.0 - ADAM_B1) * g
    v = ADAM_B2 * v + (1.0 - ADAM_B2) * (g * g)
    m_hat = m / (1.0 - ADAM_B1 ** ADAM_STEP)
    v_hat = v / (1.0 - ADAM_B2 ** ADAM_STEP)
    delta = -ADAM_LR * (m_hat / (jnp.sqrt(v_hat) + ADAM_EPS) + ADAM_WD * w)
    return delta, m, v


def _shard_adamw(part4, recv3, w, m, v, kidx, layer, bufs, *, name):
    depth, rows, cols = w.shape
    tr = _row_tile(rows, cols)

    def body(k_ref, p_ref, r0_ref, r1_ref, r2_ref, w_ref, m_ref, v_ref, *rest):
        g_out, d_out, m_out, v_out = rest[-4:]
        g = ((p_ref[...].astype(F32) + r0_ref[...].astype(F32)) + r1_ref[...].astype(F32)) + r2_ref[...].astype(F32)
        delta, mn, vn = _adamw_math(w_ref[...], g, m_ref[...], v_ref[...])
        g_out[...] = g
        d_out[...] = delta
        m_out[...] = mn
        v_out[...] = vn

    blk = (None, tr, cols)
    rspec = lambda j: pl.BlockSpec(blk, lambda i, k_ref: (j, i, 0))
    espec = pl.BlockSpec(blk, lambda i, k_ref: (layer, i, 0))
    in_specs = [pl.BlockSpec(blk, lambda i, k_ref: (k_ref[0], i, 0)), rspec(0), rspec(1), rspec(2), espec, espec, espec]
    args = [kidx, part4, recv3, recv3, recv3, w, m, v]
    aliases = {}
    if bufs is not None:
        in_specs += [ANY] * 4
        aliases = {8 + q: q for q in range(4)}
        args += list(bufs)
    return _pcall(body, name=name, grid=(rows // tr,), prefetch=1, in_specs=in_specs, out_specs=[espec] * 4,
                  out_shape=[jax.ShapeDtypeStruct((depth, rows, cols), F32)] * 4, args=args, sem=("parallel",),
                  aliases=aliases)[0]


def _small_adamw(gath, w, m, v, *, name):
    rows = w.shape[0]

    def body(g_ref, w_ref, m_ref, v_ref, g_out, d_out, m_out, v_out):
        g = g_ref[0]
        for dev in range(1, N_DEV):
            g = g + g_ref[dev]
        delta, mn, vn = _adamw_math(w_ref[...], g, m_ref[...], v_ref[...])
        g_out[...] = g
        d_out[...] = delta
        m_out[...] = mn
        v_out[...] = vn

    tr = _row_tile(rows, 128 * N_DEV)
    espec = pl.BlockSpec((tr, 128), lambda i: (i, 0))
    return _pcall(body, name=name, grid=(rows // tr,),
                  in_specs=[pl.BlockSpec((N_DEV, tr, 128), lambda i: (0, i, 0)), espec, espec, espec],
                  out_specs=[espec] * 4, out_shape=[jax.ShapeDtypeStruct((rows, 128), F32)] * 4,
                  args=(gath, w, m, v), sem=("parallel",))[0]


def _pack(arrs):
    flat = jnp.concatenate([a.reshape(-1) for a in arrs])
    total = flat.shape[0]
    rows = -(-total // (128 * 64)) * 64
    return jnp.pad(flat, (0, rows * 128 - total)).reshape(rows, 128)


def _unpack(packed, like):
    flat = packed.reshape(-1)
    out, off = [], 0
    for a in like:
        out.append(flat[off:off + a.size].reshape(a.shape))
        off += a.size
    return out


def kernel(x, g_mix, w_in, b_forget, g_sgu, w_spatial, b_spatial, w_branch_a, w_branch_b, w_out, g_ffn, w_up, conv_w, conv_b, w_down, g_final, loss_target, m_g_mix, m_w_in, m_b_forget, m_g_sgu, m_w_spatial, m_b_spatial, m_w_branch_a, m_w_branch_b, m_w_out, m_g_ffn, m_w_up, m_conv_w, m_conv_b, m_w_down, m_g_final, v_g_mix, v_w_in, v_b_forget, v_g_sgu, v_w_spatial, v_b_spatial, v_w_branch_a, v_w_branch_b, v_w_out, v_g_ffn, v_w_up, v_conv_w, v_conv_b, v_w_down, v_g_final):
    depth, d = g_mix.shape
    heads = b_forget.shape[1]
    fw = heads * HEAD_DIM
    sw = g_sgu.shape[1]
    dff = conv_b.shape[1]
    t = x.shape[1]
    nsi = w_in.shape[2]
    nsu = w_up.shape[2]
    o_f, o_u, o_g = 3 * fw, 3 * fw + heads, 3 * fw + heads + 2 * sw

    bpad = jnp.pad(b_forget, ((0, 0), (0, 128 - heads)))
    bst = jnp.pad(jnp.swapaxes(b_spatial, 1, 2), ((0, 0), (0, 0), (0, 128 - b_spatial.shape[1])))

    def mixer_shards(l):
        return [w_in[l].astype(BF16), w_branch_a[l].astype(BF16), w_branch_b[l].astype(BF16),
                w_out[l].astype(BF16), conv_w[l]]

    def ffn_shards(l):
        return [w_up[l].astype(BF16), w_down[l].astype(BF16)]

    def unpack_mixer(bufs):
        g_in, g_wa, g_wb, g_wo, g_cw = bufs
        f_in = jnp.moveaxis(g_in, 0, 1).reshape(d, N_DEV * nsi)
        return dict(wqkv=f_in[:, :o_f], wf=jnp.pad(f_in[:, o_f:o_u], ((0, 0), (0, 128 - heads))),
                    wuv=f_in[:, o_u:o_g], wg=f_in[:, o_g:], wa=g_wa, wb=g_wb,
                    wo=g_wo.reshape(d, d), cw=jnp.moveaxis(g_cw, 0, 1).reshape(CONV_WIDTH, dff))

    mixer_w = [None] * depth
    ffn_w = [None] * depth
    first = _comm_only(_gather_first(mixer_shards(0)), name="gather_mixer_first_0")
    mixer_w[0] = unpack_mixer(_comm_only(_gather_second(first), name="gather_mixer_second_0"))

    xs = x[0]
    saved = []
    for l in range(depth):
        n = lambda s: f"{s}_{l}"
        mw = mixer_w[l]
        h = _rms_fwd(xs, g_mix[l][None], name=n("rms_mix"))
        qkv = _mm(h, mw["wqkv"], name=n("proj_qkv"))
        uv = _mm(h, mw["wuv"], name=n("proj_uv"))
        gates = _mm(h, mw["wg"], name=n("proj_gates"))
        flog = _mm(h, mw["wf"], out_dtype=F32, name=n("proj_forget"))
        c3 = _forget_fwd(flog, bpad[l][None], name=n("forget_fwd")).reshape(128, 1, t)
        (ya, ya32, lse), ffn_first = _fox_fwd(qkv, c3, heads, name=n("fox_fwd"), comm=_gather_first(ffn_shards(l)))
        yb = _sgu_fwd(uv, g_sgu[l][None], w_spatial[l], bst[l], name=n("sgu_fwd"))
        (merged, za, zb), _ = _merge_fwd(ya, yb, mw["wa"], mw["wb"], gates, name=n("merge_fwd"))
        x1, ffn_bufs = _mm(merged, mw["wo"], out_dtype=F32, res=xs, name=n("out_proj"), comm=_gather_second(ffn_first))
        g_wu, g_wd = ffn_bufs
        ffn_w[l] = dict(wu=g_wu, wd=g_wd.reshape(dff, d))
        h2 = _rms_fwd(x1, g_ffn[l][None], name=n("rms_ffn"))
        if l + 1 < depth:
            ab, nxt_first = _mm(h2, g_wu, b_cols=True, name=n("ffn_up"), comm=_gather_first(mixer_shards(l + 1)))
        else:
            ab = _mm(h2, g_wu, b_cols=True, name=n("ffn_up"))
        gg = _conv_fwd(ab, mw["cw"], conv_b[l][None], name=n("conv_fwd"))
        if l + 1 < depth:
            x2, nxt = _mm(gg, ffn_w[l]["wd"], out_dtype=F32, res=x1, name=n("ffn_down"), comm=_gather_second(nxt_first))
            mixer_w[l + 1] = unpack_mixer(nxt)
        else:
            x2 = _mm(gg, ffn_w[l]["wd"], out_dtype=F32, res=x1, name=n("ffn_down"))
        saved.append((xs, h, qkv, uv, gates, flog, c3, ya, ya32, lse, yb, merged, za, zb, x1, h2, ab, gg))
        xs = x2

    loss_row, dx, dxb, d_g_final = _final_loss(xs, g_final[None], loss_target[0], name="final_loss")

    cidx = lax.axis_index("c").astype(jnp.int32).reshape(1)
    kidx = (2 * lax.axis_index("x") + lax.axis_index("y")).astype(jnp.int32).reshape(1)
    small_g = {k: [None] * depth for k in ("g_mix", "b_forget", "g_sgu", "w_spatial", "b_spatial", "g_ffn", "conv_b")}
    parts = {}
    recvs = {}
    mixer_names = ["w_in", "w_branch_a", "w_branch_b", "w_out"]
    ffn_names = ["w_up", "w_down", "conv_w"]
    for nm in mixer_names + ffn_names:
        parts[nm], recvs[nm] = [None] * depth, [None] * depth
    pending_mixer = None
    for l in reversed(range(depth)):
        n = lambda s: f"{s}_{l}"
        mw, fw_ = mixer_w[l], ffn_w[l]
        xs, h, qkv, uv, gates, flog, c3, ya, ya32, lse, yb, merged, za, zb, x1, h2, ab, gg = saved[l]
        g_wd = _mm(gg, dxb, ta=True, name=n("d_w_down"))
        dgg = _mm(dxb, fw_["wd"], tb=True, name=n("d_gg"))
        d_a, d_b, d_cw, d_cb = _conv_bwd(ab, mw["cw"], conv_b[l][None], dgg, name=n("conv_bwd"))
        small_g["conv_b"][l] = d_cb[0]
        g_wu = _mm(h2, d_a, ta=True, out_cols=(N_DEV, 0, nsu), name=n("d_w_up_a"))
        if pending_mixer is not None:
            g_wu, got = _mm(h2, d_b, ta=True, out_cols=(N_DEV, N_DEV // 2, nsu), out_alias=g_wu, name=n("d_w_up_b"),
                            comm=_scatter_second(pending_mixer))
            for nm, r in zip(mixer_names, got):
                recvs[nm][l + 1] = r
        else:
            g_wu = _mm(h2, d_b, ta=True, out_cols=(N_DEV, N_DEV // 2, nsu), out_alias=g_wu, name=n("d_w_up_b"))
        ffn_grads = [g_wu, g_wd.reshape(N_DEV, dff // N_DEV, d),
                     jnp.moveaxis(d_cw.reshape(CONV_WIDTH, N_DEV, dff // N_DEV), 1, 0)]
        dh2 = _mm(d_a, fw_["wu"], tb=True, b_cols=True, out_dtype=F32, name=n("d_h2_a"))
        dh2, from_sib = _mm(d_b, fw_["wu"], tb=True, b_cols=True, b_off=N_DEV // 2, out_dtype=F32, res=dh2,
                            name=n("d_h2_b"), comm=_scatter_first(ffn_grads))
        ffn_parts = [_pair_add(g8, r4, cidx, name=n(f"pair_add_{nm}")) for nm, g8, r4 in zip(ffn_names, ffn_grads, from_sib)]
        for nm, p in zip(ffn_names, ffn_parts):
            parts[nm][l] = p
        dx, dxb, dg = _rms_bwd(x1, g_ffn[l][None], dh2, dx, name=n("rms_ffn_bwd"))
        small_g["g_ffn"][l] = dg[0]
        g_wo = _mm(merged, dxb, ta=True, name=n("d_w_out"))
        dmg = _mm(dxb, mw["wo"], tb=True, name=n("d_merged"))
        dza, dzb, dgates = _merge_bwd(dmg, gates, za, zb, name=n("merge_bwd"))
        g_wa = _mm(ya, dza, ta=True, out_cols=(N_DEV, 0, d // N_DEV), name=n("d_w_a"))
        g_wb = _mm(yb, dzb, ta=True, out_cols=(N_DEV, 0, d // N_DEV), name=n("d_w_b"))
        dya = _mm(dza, mw["wa"], tb=True, b_cols=True, name=n("d_ya"))
        dyb = _mm(dzb, mw["wb"], tb=True, b_cols=True, name=n("d_yb"))
        duv, d_ws, d_bst, d_gs = _sgu_bwd(uv, g_sgu[l][None], w_spatial[l], bst[l], dyb, name=n("sgu_bwd"))
        small_g["w_spatial"][l], small_g["g_sgu"][l] = d_ws, d_gs[0]
        small_g["b_spatial"][l] = d_bst[:, :b_spatial.shape[1]].T
        (dq, dk, dv, dc3), got = _fox_bwd(qkv, c3, ya32, dya, lse, heads, name=n("fox_bwd"),
                                          comm=_scatter_second(ffn_parts))
        for nm, r in zip(ffn_names, got):
            recvs[nm][l] = r
        dct = jnp.pad(dc3.reshape(heads, t), ((0, 128 - heads), (0, 0)))
        dflog, d_bf = _forget_bwd(flog, bpad[l][None], dct, name=n("forget_bwd"))
        small_g["b_forget"][l] = d_bf[0, :heads]
        gw = [_mm(h, dq, ta=True, name=n("d_w_q")), _mm(h, dk, ta=True, name=n("d_w_k")),
              _mm(h, dv, ta=True, name=n("d_w_v")), _mm(h, dflog, ta=True, name=n("d_w_forget"))[:, :heads],
              _mm(h, duv, ta=True, name=n("d_w_uv")), _mm(h, dgates, ta=True, name=n("d_w_gates"))]
        g_in = jnp.moveaxis(jnp.concatenate(gw, axis=1).reshape(d, N_DEV, nsi), 1, 0)
        mixer_grads = [g_in, g_wa, g_wb, g_wo.reshape(N_DEV, d // N_DEV, d)]
        dh = _mm(dflog, mw["wf"], tb=True, out_dtype=F32, name=n("d_h_forget"))
        dh = _mm(dq, mw["wqkv"][:, :fw], tb=True, out_dtype=F32, res=dh, name=n("d_h_q"))
        dh = _mm(dk, mw["wqkv"][:, fw:2 * fw], tb=True, out_dtype=F32, res=dh, name=n("d_h_k"))
        dh = _mm(dv, mw["wqkv"][:, 2 * fw:], tb=True, out_dtype=F32, res=dh, name=n("d_h_v"))
        dh = _mm(duv, mw["wuv"], tb=True, out_dtype=F32, res=dh, name=n("d_h_uv"))
        dh, from_sib = _mm(dgates, mw["wg"], tb=True, out_dtype=F32, res=dh, name=n("d_h_gates"),
                           comm=_scatter_first(mixer_grads))
        pending_mixer = [_pair_add(g8, r4, cidx, name=n(f"pair_add_{nm}"))
                         for nm, g8, r4 in zip(mixer_names, mixer_grads, from_sib)]
        for nm, p in zip(mixer_names, pending_mixer):
            parts[nm][l] = p
        dx, dxb, dg = _rms_bwd(xs, g_mix[l][None], dh, dx, name=n("rms_mix_bwd"))
        small_g["g_mix"][l] = dg[0]
    grad_x = dx[None]
    for nm, r in zip(mixer_names, _comm_only(_scatter_second(pending_mixer), name="scatter_mixer_second_0")):
        recvs[nm][0] = r

    weights = {"w_in": (w_in, m_w_in, v_w_in), "w_branch_a": (w_branch_a, m_w_branch_a, v_w_branch_a),
               "w_branch_b": (w_branch_b, m_w_branch_b, v_w_branch_b), "w_out": (w_out, m_w_out, v_w_out),
               "w_up": (w_up, m_w_up, v_w_up), "conv_w": (conv_w, m_conv_w, v_conv_w),
               "w_down": (w_down, m_w_down, v_w_down)}
    res = {}
    for nm, (w, m, v) in weights.items():
        bufs = None
        for l in range(depth):
            bufs = _shard_adamw(parts[nm][l], recvs[nm][l], w, m, v, kidx, l, bufs, name=f"adamw_{nm}_{l}")
        res[nm] = bufs

    small = ["g_mix", "b_forget", "g_sgu", "w_spatial", "b_spatial", "g_ffn", "conv_b", "g_final", "loss"]
    zero = jnp.zeros((1,), F32)
    small_w = [g_mix, b_forget, g_sgu, w_spatial, b_spatial, g_ffn, conv_b, g_final, zero]
    small_m = [m_g_mix, m_b_forget, m_g_sgu, m_w_spatial, m_b_spatial, m_g_ffn, m_conv_b, m_g_final, zero]
    small_v = [v_g_mix, v_b_forget, v_g_sgu, v_w_spatial, v_b_spatial, v_g_ffn, v_conv_b, v_g_final, zero]
    small_grads = [jnp.stack(small_g[nm]) for nm in small[:-2]] + [d_g_final[0], loss_row[0, 0:1]]
    first = _comm_only(_gather_first([_pack(small_grads)]), name="gather_small_first")
    (gath,) = _comm_only(_gather_second(first), name="gather_small_second")
    outs = _small_adamw(gath, _pack(small_w), _pack(small_m), _pack(small_v), name="adamw_replicated")
    for nm, vals in zip(small, zip(*[_unpack(o, small_w) for o in outs])):
        res[nm] = list(vals)
    loss = res["loss"][0][0]

    order = ["g_mix", "w_in", "b_forget", "g_sgu", "w_spatial", "b_spatial", "w_branch_a", "w_branch_b", "w_out",
             "g_ffn", "w_up", "conv_w", "conv_b", "w_down", "g_final"]
    return (loss, grad_x, *[res[nm][0] for nm in order], *[res[nm][1] for nm in order],
            *[res[nm][2] for nm in order], *[res[nm][3] for nm in order])
```

```python
import numpy as np

import jax
import jax.numpy as jnp
from jax import lax
from jax.experimental import pallas as pl
from jax.experimental.pallas import tpu as pltpu

F32 = jnp.float32
BF16 = jnp.bfloat16

RMS_EPS = 1e-6
HEAD_DIM = 128
CONV_WIDTH = 3
ADAM_LR = 0.001
ADAM_B1 = 0.9
ADAM_B2 = 0.999
ADAM_EPS = 1e-08
ADAM_WD = 0.01
ADAM_STEP = 10
N_DEV = 8
V7X_VMEM_LIMIT = 58 * 1024 * 1024
MM_VMEM_BUDGET = 46 * 1024 * 1024
NEG = -1e30
ANY = pl.BlockSpec(memory_space=pl.ANY)
MESH = pl.DeviceIdType.MESH


def _tile(dim, pref):
    for t in (2048, 1024, 512, 256, 128):
        if t <= pref and dim % t == 0:
            return t
    return dim


def _gelu(x):
    t = jnp.tanh(0.7978845608028654 * (x + 0.044715 * (x * x * x)))
    return x * (0.5 * (1.0 + t))


def _gelu_and_grad(x):
    x2 = x * x
    t = jnp.tanh(0.7978845608028654 * (x + 0.044715 * (x2 * x)))
    cdf = 0.5 * (1.0 + t)
    dt = (1.0 - t * t) * (0.7978845608028654 * (1.0 + 3.0 * 0.044715 * x2))
    return x * cdf, cdf + 0.5 * x * dt


def _sigmoid(x):
    return 1.0 / (1.0 + jnp.exp(-x))


class _Comm:
    def __init__(self, srcs, new, alias, n_copies, emit):
        self.srcs = list(srcs)
        self.new = list(new)
        self.alias = list(alias)
        self.n_copies = n_copies
        self.emit = emit


def _place():
    x, y, c = lax.axis_index("x"), lax.axis_index("y"), lax.axis_index("c")
    chips = [(1 - x, y), (x, 1 - y), (1 - x, 1 - y)]
    return x, y, c, chips


def _remote(src, dst, sems, k, to):
    return pltpu.make_async_remote_copy(src_ref=src, dst_ref=dst, send_sem=sems[0].at[k], recv_sem=sems[1].at[k],
                                        device_id=to, device_id_type=MESH)


def _gather_first(shards):
    n = len(shards)

    def emit(srcs, new, alias, send_sems, recv_sems):
        x, y, c, chips = _place()
        me = 4 * x + 2 * y + c
        sems = (send_sems, recv_sems)
        copies = []
        for a in range(n):
            copies.append(pltpu.make_async_copy(srcs[a], new[a].at[me], send_sems.at[5 * a + 4]))
            copies.append(_remote(srcs[a], new[a].at[me], sems, 5 * a, (x, y, 1 - c)))
            for j, chip in enumerate(chips):
                copies.append(_remote(srcs[a], new[a].at[me], sems, 5 * a + 1 + j, (*chip, c)))
        return copies

    new = [jax.ShapeDtypeStruct((N_DEV,) + s.shape, s.dtype) for s in shards]
    return _Comm(shards, new, [], 5 * n, emit)


def _gather_second(bufs):
    n = len(bufs)

    def emit(srcs, new, alias, send_sems, recv_sems):
        x, y, c, chips = _place()
        sems = (send_sems, recv_sems)
        copies = []
        for a in range(n):
            for j, chip in enumerate(chips):
                blk = alias[a].at[4 * chip[0] + 2 * chip[1] + c]
                copies.append(_remote(blk, blk, sems, 3 * a + j, (x, y, 1 - c)))
        return copies

    return _Comm([], [], bufs, 3 * n, emit)


def _scatter_first(grads):
    n = len(grads)

    def emit(srcs, new, alias, send_sems, recv_sems):
        x, y, c, _ = _place()
        sems = (send_sems, recv_sems)
        return [_remote(srcs[a].at[2 * k + 1 - c], new[a].at[k], sems, 4 * a + k, (x, y, 1 - c))
                for a in range(n) for k in range(4)]

    new = [jax.ShapeDtypeStruct((4,) + g.shape[1:], g.dtype) for g in grads]
    return _Comm(grads, new, [], 4 * n, emit)


def _scatter_second(parts):
    n = len(parts)

    def emit(srcs, new, alias, send_sems, recv_sems):
        x, y, c, chips = _place()
        sems = (send_sems, recv_sems)
        return [_remote(srcs[a].at[2 * chip[0] + chip[1]], new[a].at[j], sems, 3 * a + j, (*chip, c))
                for a in range(n) for j, chip in enumerate(chips)]

    new = [jax.ShapeDtypeStruct((3,) + p.shape[1:], p.dtype) for p in parts]
    return _Comm(parts, new, [], 3 * n, emit)


def _pcall(body, *, name, grid, in_specs, out_specs, out_shape, args, sem, scratch=(), aliases=None, comm=None,
           prefetch=0):
    in_specs, out_specs, out_shape, scratch = list(in_specs), list(out_specs), list(out_shape), list(scratch)
    aliases = dict(aliases or {})
    n_in, n_out, n_scr = len(in_specs), len(out_shape), len(scratch)

    def make(body_fn, ins, outs, shapes, scr, sem_):
        params = pltpu.CompilerParams(dimension_semantics=sem_, vmem_limit_bytes=V7X_VMEM_LIMIT)
        if prefetch:
            spec = pltpu.PrefetchScalarGridSpec(num_scalar_prefetch=prefetch, grid=grid, in_specs=ins,
                                                out_specs=outs, scratch_shapes=scr)
            return pl.pallas_call(body_fn, name=name, grid_spec=spec, out_shape=shapes,
                                  input_output_aliases=aliases, compiler_params=params)
        return pl.pallas_call(body_fn, name=name, grid=grid, in_specs=ins, out_specs=outs, out_shape=shapes,
                              scratch_shapes=scr, input_output_aliases=aliases, compiler_params=params)

    if comm is None:
        return list(make(body, in_specs, out_specs, out_shape, scratch, sem)(*args)), []

    n_src, n_new, n_al = len(comm.srcs), len(comm.new), len(comm.alias)
    for a in range(n_al):
        aliases[prefetch + n_in + n_src + a] = n_out + n_new + a

    def wrapped(*refs):
        pre, refs = refs[:prefetch], refs[prefetch:]
        ins = refs[:n_in]
        src_refs = refs[n_in:n_in + n_src]
        o0 = n_in + n_src + n_al
        outs = refs[o0:o0 + n_out]
        new_refs = refs[o0 + n_out:o0 + n_out + n_new]
        alias_refs = refs[o0 + n_out + n_new:o0 + n_out + n_new + n_al]
        s0 = o0 + n_out + n_new + n_al
        scr = refs[s0:s0 + n_scr]
        send_sems, recv_sems = refs[s0 + n_scr], refs[s0 + n_scr + 1]
        first = pl.program_id(0) == 0
        last = pl.program_id(0) == grid[0] - 1
        for dim in range(1, len(grid)):
            first = first & (pl.program_id(dim) == 0)
            last = last & (pl.program_id(dim) == grid[dim] - 1)

        @pl.when(first)
        def _():
            for cp in comm.emit(src_refs, new_refs, alias_refs, send_sems, recv_sems):
                cp.start()

        body(*pre, *ins, *outs, *scr)

        @pl.when(last)
        def _():
            for cp in comm.emit(src_refs, new_refs, alias_refs, send_sems, recv_sems):
                cp.wait()

    call = make(wrapped, in_specs + [ANY] * (n_src + n_al), out_specs + [ANY] * (n_new + n_al),
                out_shape + comm.new + [jax.ShapeDtypeStruct(b.shape, b.dtype) for b in comm.alias],
                scratch + [pltpu.SemaphoreType.DMA((comm.n_copies,)), pltpu.SemaphoreType.DMA((comm.n_copies,))],
                ("arbitrary",) * len(grid))
    res = list(call(*args, *comm.srcs, *comm.alias))
    return res[:n_out], res[n_out:]


def _comm_only(comm, *, name):
    def body(o_ref):
        o_ref[...] = jnp.zeros_like(o_ref)

    _, couts = _pcall(body, name=name, grid=(1,), in_specs=[], out_specs=[pl.BlockSpec((8, 128), lambda i: (0, 0))],
                      out_shape=[jax.ShapeDtypeStruct((8, 128), F32)], args=(), sem=("arbitrary",), comm=comm)
    return couts


def _divisor_tiles(dim, cap):
    tiles = [t for t in range(128, min(dim, cap) + 1, 128) if dim % t == 0]
    return sorted(tiles, reverse=True) or [dim]


def _mm_tiles(m, n, k, obytes, has_res, tn_fixed=None, tk_fixed=None):
    tms = _divisor_tiles(m, 1408)
    tns = [tn_fixed] if tn_fixed else _divisor_tiles(n, 1408)
    tks = [tk_fixed] if tk_fixed else [k] + [tt for tt in _divisor_tiles(k, 2048) if tt != k]
    best, best_score = None, None
    for tk in tks:
        for tm in tms:
            for tn in tns:
                nk = k // tk
                use = 4 * tm * tk + 4 * tk * tn + 2 * tm * tn * obytes
                use += (8 * tm * tn if has_res else 0) + (4 * tm * tn if nk > 1 else 0)
                score = (nk == 1, min(tm, 1024), tn, tm, tk)
                if use <= MM_VMEM_BUDGET and (best is None or score > best_score):
                    best, best_score = (tm, tn, tk), score
    assert best is not None, (m, n, k)
    return best


def _mm(a, b, *, ta=False, tb=False, out_dtype=BF16, res=None, name, b_cols=False, b_off=0,
        out_cols=None, out_alias=None, comm=None):
    m, k = (a.shape[1], a.shape[0]) if ta else a.shape
    obytes = jnp.dtype(out_dtype).itemsize
    if b_cols and not tb:
        ns = b.shape[2]
        assert b.shape[1] == k
        n = b.shape[0] * ns
        tm, tn, tk = _mm_tiles(m, n, k, obytes, res is not None, tn_fixed=ns)
        b_spec = pl.BlockSpec((None, tk, ns), lambda i, j, kk: (j, kk, 0))
    elif b_cols:
        ns = b.shape[2]
        assert k % ns == 0
        n = b.shape[1]
        tm, tn, tk = _mm_tiles(m, n, k, obytes, res is not None, tk_fixed=ns)
        b_spec = pl.BlockSpec((None, tn, ns), lambda i, j, kk: (b_off + kk, j, 0))
    else:
        n = b.shape[0] if tb else b.shape[1]
        assert (b.shape[1] if tb else b.shape[0]) == k
        tm, tn, tk = _mm_tiles(m, n, k, obytes, res is not None, tn_fixed=out_cols[2] if out_cols is not None else None)
        b_spec = pl.BlockSpec((tn, tk), lambda i, j, kk: (j, kk)) if tb else pl.BlockSpec((tk, tn), lambda i, j, kk: (kk, j))
    nk = k // tk
    dn = (((0,) if ta else (1,), (1,) if tb else (0,)), ((), ()))
    n_extra = (res is not None) + (out_alias is not None)

    def body(*refs):
        a_ref, b_ref = refs[0], refs[1]
        r_ref = refs[2] if res is not None else None
        o_ref = refs[2 + n_extra]
        part = lax.dot_general(a_ref[...], b_ref[...], dn, preferred_element_type=F32)

        def finish(r):
            if r_ref is not None:
                r = r + r_ref[...]
            o_ref[...] = r.astype(out_dtype)

        if nk == 1:
            finish(part)
        else:
            acc_ref = refs[3 + n_extra]
            kk = pl.program_id(2)

            @pl.when(kk == 0)
            def _():
                acc_ref[...] = part

            @pl.when(kk > 0)
            def _():
                acc_ref[...] += part

            @pl.when(kk == nk - 1)
            def _():
                finish(acc_ref[...])

    a_spec = pl.BlockSpec((tk, tm), lambda i, j, kk: (kk, i)) if ta else pl.BlockSpec((tm, tk), lambda i, j, kk: (i, kk))
    in_specs, args, aliases = [a_spec, b_spec], [a, b], {}
    if res is not None:
        in_specs.append(pl.BlockSpec((tm, tn), lambda i, j, kk: (i, j)))
        args.append(res)
    if out_cols is not None:
        s_total, o_off, ns_o = out_cols
        assert tn == ns_o and n % ns_o == 0
        o_spec = pl.BlockSpec((None, tm, tn), lambda i, j, kk: (o_off + j, i, 0))
        o_shape = jax.ShapeDtypeStruct((s_total, m, tn), out_dtype)
        if out_alias is not None:
            in_specs.append(ANY)
            args.append(out_alias)
            aliases[len(args) - 1] = 0
    else:
        o_spec = pl.BlockSpec((tm, tn), lambda i, j, kk: (i, j))
        o_shape = jax.ShapeDtypeStruct((m, n), out_dtype)
    outs, couts = _pcall(body, name=name, grid=(m // tm, n // tn, nk), in_specs=in_specs, out_specs=[o_spec],
                         out_shape=[o_shape], args=args, sem=("parallel", "parallel", "arbitrary"),
                         scratch=[pltpu.VMEM((tm, tn), F32)] if nk > 1 else [], aliases=aliases, comm=comm)
    return (outs[0], couts) if comm is not None else outs[0]


def _rms_fwd(x, g, *, name):
    t, d = x.shape
    tr = _tile(t, 256)

    def body(x_ref, g_ref, h_ref):
        xf = x_ref[...]
        inv = lax.rsqrt(jnp.mean(xf * xf, axis=-1, keepdims=True) + RMS_EPS)
        h_ref[...] = ((xf * inv) * g_ref[...]).astype(BF16)

    row = pl.BlockSpec((tr, d), lambda i: (i, 0))
    return _pcall(body, name=name, grid=(t // tr,), in_specs=[row, pl.BlockSpec((1, d), lambda i: (0, 0))],
                  out_specs=[row], out_shape=[jax.ShapeDtypeStruct((t, d), BF16)], args=(x, g),
                  sem=("parallel",))[0][0]


def _rms_bwd(x, g, dh, dres, *, name):
    t, d = x.shape
    tr = _tile(t, 256)

    def body(x_ref, g_ref, dh_ref, dres_ref, dx_ref, dxb_ref, dg_ref):
        xf = x_ref[...]
        inv = lax.rsqrt(jnp.mean(xf * xf, axis=-1, keepdims=True) + RMS_EPS)
        xn = xf * inv
        dh_f = dh_ref[...].astype(F32)
        dxn = dh_f * g_ref[...]
        dx = dres_ref[...] + inv * (dxn - xn * jnp.mean(dxn * xn, axis=-1, keepdims=True))
        dx_ref[...] = dx
        dxb_ref[...] = dx.astype(BF16)
        part = jnp.sum(dh_f * xn, axis=0, keepdims=True)

        @pl.when(pl.program_id(0) == 0)
        def _():
            dg_ref[...] = part

        @pl.when(pl.program_id(0) > 0)
        def _():
            dg_ref[...] += part

    row = pl.BlockSpec((tr, d), lambda i: (i, 0))
    vec = pl.BlockSpec((1, d), lambda i: (0, 0))
    return _pcall(body, name=name, grid=(t // tr,), in_specs=[row, vec, row, row], out_specs=[row, row, vec],
                  out_shape=[jax.ShapeDtypeStruct((t, d), F32), jax.ShapeDtypeStruct((t, d), BF16),
                             jax.ShapeDtypeStruct((1, d), F32)], args=(x, g, dh, dres), sem=("arbitrary",))[0]


def _final_loss(x, g, target, *, name):
    t, d = x.shape
    tr = _tile(t, 256)

    def body(x_ref, g_ref, tg_ref, loss_ref, dx_ref, dxb_ref, dg_ref):
        xf = x_ref[...]
        gv = g_ref[...]
        inv = lax.rsqrt(jnp.mean(xf * xf, axis=-1, keepdims=True) + RMS_EPS)
        xn = xf * inv
        err = xn * gv - tg_ref[...]
        lpart = 0.5 * jnp.sum(jnp.mean(err * err, axis=-1, keepdims=True), axis=0, keepdims=True)
        dy = err * (1.0 / d)
        dxn = dy * gv
        dx = inv * (dxn - xn * jnp.mean(dxn * xn, axis=-1, keepdims=True))
        dx_ref[...] = dx
        dxb_ref[...] = dx.astype(BF16)
        gpart = jnp.sum(dy * xn, axis=0, keepdims=True)
        lrow = jnp.broadcast_to(lpart, (1, 128))

        @pl.when(pl.program_id(0) == 0)
        def _():
            dg_ref[...] = gpart
            loss_ref[...] = lrow

        @pl.when(pl.program_id(0) > 0)
        def _():
            dg_ref[...] += gpart
            loss_ref[...] += lrow

    row = pl.BlockSpec((tr, d), lambda i: (i, 0))
    vec = pl.BlockSpec((1, d), lambda i: (0, 0))
    lspec = pl.BlockSpec((1, 128), lambda i: (0, 0))
    return _pcall(body, name=name, grid=(t // tr,), in_specs=[row, vec, row], out_specs=[lspec, row, row, vec],
                  out_shape=[jax.ShapeDtypeStruct((1, 128), F32), jax.ShapeDtypeStruct((t, d), F32),
                             jax.ShapeDtypeStruct((t, d), BF16), jax.ShapeDtypeStruct((1, d), F32)],
                  args=(x, g, target), sem=("arbitrary",))[0]


def _tri_ones(n, upper):
    r = lax.broadcasted_iota(jnp.int32, (n, n), 0)
    c = lax.broadcasted_iota(jnp.int32, (n, n), 1)
    return jnp.where((r <= c) if upper else (r >= c), 1.0, 0.0).astype(F32)


def _forget_fwd(flog, bpad, *, name):
    t = flog.shape[0]
    tb = _tile(t, 512)

    def body(f_ref, b_ref, c_ref, carry):
        z = f_ref[...] + b_ref[...]
        lf = jnp.minimum(z, 0.0) - jnp.log(1.0 + jnp.exp(-jnp.abs(z)))
        lft = lf.T
        tri = _tri_ones(tb, upper=True)

        @pl.when(pl.program_id(0) == 0)
        def _():
            carry[...] = jnp.zeros_like(carry)

        cs = jnp.dot(lft, tri, preferred_element_type=F32, precision=lax.Precision.HIGHEST) + carry[:, 0:1]
        c_ref[...] = cs
        carry[...] = jnp.broadcast_to(cs[:, tb - 1:tb], carry.shape)

    return _pcall(body, name=name, grid=(t // tb,),
                  in_specs=[pl.BlockSpec((tb, 128), lambda i: (i, 0)), pl.BlockSpec((1, 128), lambda i: (0, 0))],
                  out_specs=[pl.BlockSpec((128, tb), lambda i: (0, i))],
                  out_shape=[jax.ShapeDtypeStruct((128, t), F32)], args=(flog, bpad), sem=("arbitrary",),
                  scratch=[pltpu.VMEM((128, 128), F32)])[0][0]


def _forget_bwd(flog, bpad, dct, *, name):
    t = flog.shape[0]
    tb = _tile(t, 512)
    nb = t // tb

    def body(f_ref, b_ref, dc_ref, df_ref, db_ref, carry):
        i = pl.program_id(0)

        @pl.when(i == 0)
        def _():
            carry[...] = jnp.zeros_like(carry)

        tri = _tri_ones(tb, upper=False)
        dl = jnp.dot(dc_ref[...], tri, preferred_element_type=F32, precision=lax.Precision.HIGHEST) + carry[:, 0:1]
        carry[...] = jnp.broadcast_to(dl[:, 0:1], carry.shape)
        z = f_ref[...] + b_ref[...]
        df = dl.T * _sigmoid(-z)
        df_ref[...] = df.astype(BF16)
        part = jnp.sum(df, axis=0, keepdims=True)

        @pl.when(i == 0)
        def _():
            db_ref[...] = part

        @pl.when(i > 0)
        def _():
            db_ref[...] += part

    rev = lambda i: (nb - 1 - i, 0)
    return _pcall(body, name=name, grid=(nb,),
                  in_specs=[pl.BlockSpec((tb, 128), rev), pl.BlockSpec((1, 128), lambda i: (0, 0)),
                            pl.BlockSpec((128, tb), lambda i: (0, nb - 1 - i))],
                  out_specs=[pl.BlockSpec((tb, 128), rev), pl.BlockSpec((1, 128), lambda i: (0, 0))],
                  out_shape=[jax.ShapeDtypeStruct((t, 128), BF16), jax.ShapeDtypeStruct((1, 128), F32)],
                  args=(flog, bpad, dct), sem=("arbitrary",), scratch=[pltpu.VMEM((128, 128), F32)])[0]


def _causal_pairs(nq, k_major):
    if k_major:
        pairs = [(i, j) for j in range(nq) for i in range(j, nq)]
    else:
        pairs = [(i, j) for i in range(nq) for j in range(i + 1)]
    return (jnp.asarray(np.array([p[0] for p in pairs], np.int32)),
            jnp.asarray(np.array([p[1] for p in pairs], np.int32)), len(pairs))


def _logits(q, k, cq, ck, scale, masked):
    s = lax.dot_general(q, k, (((1,), (1,)), ((), ())), preferred_element_type=F32)
    s = s * scale + (cq[:, 0:1] - ck)
    if masked:
        row = lax.broadcasted_iota(jnp.int32, s.shape, 0)
        col = lax.broadcasted_iota(jnp.int32, s.shape, 1)
        s = jnp.where(col <= row, s, NEG)
    return s


def _head_group(heads):
    return 4 if heads % 4 == 0 else (2 if heads % 2 == 0 else 1)


def _fox_fwd(qkv, c3, heads, *, name, comm=None):
    t = qkv.shape[0]
    tq = _tile(t, 512)
    nq = t // tq
    hb = _head_group(heads)
    ng = heads // hb
    scale = HEAD_DIM ** -0.5
    i_tab, j_tab, npairs = _causal_pairs(nq, k_major=False)

    def body(it_ref, jt_ref, q_ref, k_ref, v_ref, cq_ref, ck_ref, o_ref, o32_ref, lse_ref, m_s, l_s, acc_s):
        p_id = pl.program_id(1)
        i, j = it_ref[p_id], jt_ref[p_id]

        @pl.when(j == 0)
        def _():
            m_s[...] = jnp.full_like(m_s, NEG)
            l_s[...] = jnp.zeros_like(l_s)
            acc_s[...] = jnp.zeros_like(acc_s)

        def update(masked):
            for hh in range(hb):
                ls = slice(hh * 128, (hh + 1) * 128)
                s = _logits(q_ref[:, ls], k_ref[:, ls], cq_ref[hh], ck_ref[hh], scale, masked)
                m_prev = m_s[hh, :, 0:1]
                m_new = jnp.maximum(m_prev, jnp.max(s, axis=1, keepdims=True))
                alpha = jnp.exp(m_prev - m_new)
                p = jnp.exp(s - m_new)
                l_s[hh, :, 0:1] = alpha * l_s[hh, :, 0:1] + jnp.sum(p, axis=1, keepdims=True)
                p_hi = p.astype(BF16)
                p_lo = (p - p_hi.astype(F32)).astype(BF16)
                vb = v_ref[:, ls]
                pv = jnp.dot(p_hi, vb, preferred_element_type=F32) + jnp.dot(p_lo, vb, preferred_element_type=F32)
                acc_s[hh] = alpha * acc_s[hh] + pv
                m_s[hh, :, 0:1] = m_new

        @pl.when(j < i)
        def _():
            update(False)

        @pl.when(j == i)
        def _():
            update(True)
            for hh in range(hb):
                ls = slice(hh * 128, (hh + 1) * 128)
                l = l_s[hh, :, 0:1]
                o = acc_s[hh] / l
                o_ref[:, ls] = o.astype(BF16)
                o32_ref[:, ls] = o
                lse_ref[hh] = jnp.broadcast_to(m_s[hh, :, 0:1] + jnp.log(l), (tq, 128))

    w = hb * 128
    qb = lambda g, p, it, jt: (it[p], g)
    outs, couts = _pcall(
        body, name=name, grid=(ng, npairs), prefetch=2,
        in_specs=[pl.BlockSpec((tq, w), qb),
                  pl.BlockSpec((tq, w), lambda g, p, it, jt: (jt[p], ng + g)),
                  pl.BlockSpec((tq, w), lambda g, p, it, jt: (jt[p], 2 * ng + g)),
                  pl.BlockSpec((hb, 1, tq), lambda g, p, it, jt: (g, 0, it[p])),
                  pl.BlockSpec((hb, 1, tq), lambda g, p, it, jt: (g, 0, jt[p]))],
        out_specs=[pl.BlockSpec((tq, w), qb), pl.BlockSpec((tq, w), qb),
                   pl.BlockSpec((hb, tq, 128), lambda g, p, it, jt: (g, it[p], 0))],
        out_shape=[jax.ShapeDtypeStruct((t, heads * 128), BF16), jax.ShapeDtypeStruct((t, heads * 128), F32),
                   jax.ShapeDtypeStruct((heads, t, 128), F32)],
        args=(i_tab, j_tab, qkv, qkv, qkv, c3, c3), sem=("parallel", "arbitrary"),
        scratch=[pltpu.VMEM((hb, tq, 128), F32), pltpu.VMEM((hb, tq, 128), F32), pltpu.VMEM((hb, tq, 128), F32)],
        comm=comm)
    return outs, couts


def _fox_bwd(qkv, c3, o, do, lse, heads, *, name, comm=None):
    t = qkv.shape[0]
    tq = _tile(t, 512)
    nq = t // tq
    hb = _head_group(heads)
    ng = heads // hb
    scale = HEAD_DIM ** -0.5
    i_tab, j_tab, npairs = _causal_pairs(nq, k_major=True)

    def body(it_ref, jt_ref, q_ref, k_ref, v_ref, o_ref, do_ref, lse_ref, cq_ref, ck_ref,
             dq_ref, dk_ref, dv_ref, dc_ref, dq_s, dk_s, dv_s, dc_s):
        p_id = pl.program_id(1)
        i, j = it_ref[p_id], jt_ref[p_id]

        @pl.when(p_id == 0)
        def _():
            dq_s[...] = jnp.zeros_like(dq_s)

        @pl.when(i == j)
        def _():
            dk_s[...] = jnp.zeros_like(dk_s)
            dv_s[...] = jnp.zeros_like(dv_s)
            dc_s[...] = jnp.zeros_like(dc_s)

        def update(masked):
            r0 = pl.multiple_of(i * tq, tq)
            for hh in range(hb):
                ls = slice(hh * 128, (hh + 1) * 128)
                q, k, v, dob = q_ref[:, ls], k_ref[:, ls], v_ref[:, ls], do_ref[:, ls]
                s = _logits(q, k, cq_ref[hh], ck_ref[hh], scale, masked)
                p = jnp.exp(s - lse_ref[hh, :, 0:1])
                delta = jnp.sum(dob.astype(F32) * o_ref[:, ls], axis=1, keepdims=True)
                dp = lax.dot_general(dob, v, (((1,), (1,)), ((), ())), preferred_element_type=F32)
                ds = p * (dp - delta)
                pb, dsb = p.astype(BF16), ds.astype(BF16)
                dv_s[hh] += lax.dot_general(pb, dob, (((0,), (0,)), ((), ())), preferred_element_type=F32)
                dk_s[hh] += lax.dot_general(dsb, q, (((0,), (0,)), ((), ())), preferred_element_type=F32)
                dq_s[hh, pl.ds(r0, tq), :] += jnp.dot(dsb, k, preferred_element_type=F32) * scale
                dc_s[hh] -= jnp.sum(ds, axis=0, keepdims=True)

        @pl.when(i > j)
        def _():
            update(False)

        @pl.when(i == j)
        def _():
            update(True)

        @pl.when(i == nq - 1)
        def _():
            for hh in range(hb):
                ls = slice(hh * 128, (hh + 1) * 128)
                dk_ref[:, ls] = (dk_s[hh] * scale).astype(BF16)
                dv_ref[:, ls] = dv_s[hh].astype(BF16)
            dc_ref[...] = dc_s[...]

        @pl.when(p_id == npairs - 1)
        def _():
            for hh in range(hb):
                dq_ref[:, hh * 128:(hh + 1) * 128] = dq_s[hh].astype(BF16)

    w = hb * 128
    qb = lambda g, p, it, jt: (it[p], g)
    kb = lambda g, p, it, jt: (jt[p], g)
    outs, couts = _pcall(
        body, name=name, grid=(ng, npairs), prefetch=2,
        in_specs=[pl.BlockSpec((tq, w), qb),
                  pl.BlockSpec((tq, w), lambda g, p, it, jt: (jt[p], ng + g)),
                  pl.BlockSpec((tq, w), lambda g, p, it, jt: (jt[p], 2 * ng + g)),
                  pl.BlockSpec((tq, w), qb), pl.BlockSpec((tq, w), qb),
                  pl.BlockSpec((hb, tq, 128), lambda g, p, it, jt: (g, it[p], 0)),
                  pl.BlockSpec((hb, 1, tq), lambda g, p, it, jt: (g, 0, it[p])),
                  pl.BlockSpec((hb, 1, tq), lambda g, p, it, jt: (g, 0, jt[p]))],
        out_specs=[pl.BlockSpec((t, w), lambda g, p, it, jt: (0, g)), pl.BlockSpec((tq, w), kb),
                   pl.BlockSpec((tq, w), kb), pl.BlockSpec((hb, 1, tq), lambda g, p, it, jt: (g, 0, jt[p]))],
        out_shape=[jax.ShapeDtypeStruct((t, heads * 128), BF16), jax.ShapeDtypeStruct((t, heads * 128), BF16),
                   jax.ShapeDtypeStruct((t, heads * 128), BF16), jax.ShapeDtypeStruct((heads, 1, t), F32)],
        args=(i_tab, j_tab, qkv, qkv, qkv, o, do, lse, c3, c3), sem=("arbitrary", "arbitrary"),
        scratch=[pltpu.VMEM((hb, t, 128), F32), pltpu.VMEM((hb, tq, 128), F32), pltpu.VMEM((hb, tq, 128), F32),
                 pltpu.VMEM((hb, 1, tq), F32)], comm=comm)
    return outs, couts


def _tril_mask():
    r = lax.broadcasted_iota(jnp.int32, (128, 128), 0)
    c = lax.broadcasted_iota(jnp.int32, (128, 128), 1)
    return r >= c


def _sgu_fwd(uv, g, w, bst, *, name):
    t = uv.shape[0]
    sw = uv.shape[1] // 2
    groups = sw // 128
    tr = _tile(t, 512)

    def body(u_ref, v_ref, g_ref, w_ref, b_ref, y_ref):
        gv = _gelu(v_ref[...].astype(F32))
        inv = lax.rsqrt(jnp.mean(gv * gv, axis=-1, keepdims=True) + RMS_EPS)
        vn = ((gv * inv) * g_ref[...]).astype(BF16)
        gu = _gelu(u_ref[...].astype(F32))
        mask = _tril_mask()
        for gi in range(groups):
            wg = jnp.where(mask, w_ref[gi], 0.0).astype(BF16)
            bcol = b_ref[:, gi:gi + 1]
            cs = slice(gi * 128, (gi + 1) * 128)
            for ci in range(tr // 128):
                rs = slice(ci * 128, (ci + 1) * 128)
                mixed = jnp.dot(wg, vn[rs, cs], preferred_element_type=F32) + bcol
                y_ref[rs, cs] = (gu[rs, cs] * mixed).astype(BF16)

    return _pcall(body, name=name, grid=(t // tr,),
                  in_specs=[pl.BlockSpec((tr, sw), lambda i: (i, 0)), pl.BlockSpec((tr, sw), lambda i: (i, 1)),
                            pl.BlockSpec((1, sw), lambda i: (0, 0)),
                            pl.BlockSpec((groups, 128, 128), lambda i: (0, 0, 0)),
                            pl.BlockSpec((128, 128), lambda i: (0, 0))],
                  out_specs=[pl.BlockSpec((tr, sw), lambda i: (i, 0))],
                  out_shape=[jax.ShapeDtypeStruct((t, sw), BF16)], args=(uv, uv, g, w, bst),
                  sem=("parallel",))[0][0]


def _sgu_bwd(uv, g, w, bst, dy, *, name):
    t = uv.shape[0]
    sw = uv.shape[1] // 2
    groups = sw // 128
    tr = _tile(t, 256)
    nsteps = t // tr

    def body(u_ref, v_ref, g_ref, w_ref, b_ref, dy_ref, duv_ref, dw_ref, db_ref, dg_ref, dvn_s, dgu_s):
        step = pl.program_id(0)

        @pl.when(step == 0)
        def _():
            dw_ref[...] = jnp.zeros_like(dw_ref)
            db_ref[...] = jnp.zeros_like(db_ref)
            dg_ref[...] = jnp.zeros_like(dg_ref)

        vf = v_ref[...].astype(F32)
        gv, gv_grad = _gelu_and_grad(vf)
        inv = lax.rsqrt(jnp.mean(gv * gv, axis=-1, keepdims=True) + RMS_EPS)
        xn = gv * inv
        gvec = g_ref[...]
        vn = (xn * gvec).astype(BF16)
        uf = u_ref[...].astype(F32)
        gu, gu_grad = _gelu_and_grad(uf)
        dyf = dy_ref[...].astype(F32)
        mask = _tril_mask()
        lane = lax.broadcasted_iota(jnp.int32, (128, 128), 1)
        dball = jnp.zeros((128, 128), F32)
        for gi in range(groups):
            wg = jnp.where(mask, w_ref[gi], 0.0).astype(BF16)
            wgt = wg.T
            bcol = b_ref[:, gi:gi + 1]
            cs = slice(gi * 128, (gi + 1) * 128)
            dwg = jnp.zeros((128, 128), F32)
            dbg = jnp.zeros((128, 1), F32)
            for ci in range(tr // 128):
                rs = slice(ci * 128, (ci + 1) * 128)
                vnb = vn[rs, cs]
                mixed = jnp.dot(wg, vnb, preferred_element_type=F32) + bcol
                dgu_s[rs, cs] = dyf[rs, cs] * mixed
                dmix = dyf[rs, cs] * gu[rs, cs]
                dmb = dmix.astype(BF16)
                dvn_s[rs, cs] = jnp.dot(wgt, dmb, preferred_element_type=F32)
                dwg = dwg + lax.dot_general(dmb, vnb, (((1,), (1,)), ((), ())), preferred_element_type=F32)
                dbg = dbg + jnp.sum(dmix, axis=1, keepdims=True)
            dw_ref[gi] += dwg
            dball = dball + jnp.where(lane == gi, dbg, 0.0)
        db_ref[...] += dball
        dvn = dvn_s[...]
        dg_ref[...] += jnp.sum(dvn * xn, axis=0, keepdims=True)
        dxn = dvn * gvec
        dgv = inv * (dxn - xn * jnp.mean(dxn * xn, axis=-1, keepdims=True))
        duv_ref[:, 0:sw] = (dgu_s[...] * gu_grad).astype(BF16)
        duv_ref[:, sw:2 * sw] = (dgv * gv_grad).astype(BF16)

        @pl.when(step == nsteps - 1)
        def _():
            for gi in range(groups):
                dw_ref[gi] = jnp.where(mask, dw_ref[gi], 0.0)

    return _pcall(body, name=name, grid=(nsteps,),
                  in_specs=[pl.BlockSpec((tr, sw), lambda i: (i, 0)), pl.BlockSpec((tr, sw), lambda i: (i, 1)),
                            pl.BlockSpec((1, sw), lambda i: (0, 0)),
                            pl.BlockSpec((groups, 128, 128), lambda i: (0, 0, 0)),
                            pl.BlockSpec((128, 128), lambda i: (0, 0)), pl.BlockSpec((tr, sw), lambda i: (i, 0))],
                  out_specs=[pl.BlockSpec((tr, 2 * sw), lambda i: (i, 0)),
                             pl.BlockSpec((groups, 128, 128), lambda i: (0, 0, 0)),
                             pl.BlockSpec((128, 128), lambda i: (0, 0)), pl.BlockSpec((1, sw), lambda i: (0, 0))],
                  out_shape=[jax.ShapeDtypeStruct((t, 2 * sw), BF16), jax.ShapeDtypeStruct((groups, 128, 128), F32),
                             jax.ShapeDtypeStruct((128, 128), F32), jax.ShapeDtypeStruct((1, sw), F32)],
                  args=(uv, uv, g, w, bst, dy), sem=("arbitrary",),
                  scratch=[pltpu.VMEM((tr, sw), F32), pltpu.VMEM((tr, sw), F32)])[0]


def _merge_fwd(ya, yb, wa, wb, gates, *, name, comm=None):
    t, kdim = ya.shape
    nsh, _, ns = wa.shape
    d = nsh * ns
    tm = _tile(t, 1024)

    def body(ya_ref, yb_ref, wa_ref, wb_ref, ga_ref, gb_ref, mg_ref, za_ref, zb_ref):
        za = jnp.dot(ya_ref[...], wa_ref[...], preferred_element_type=F32)
        zb = jnp.dot(yb_ref[...], wb_ref[...], preferred_element_type=F32)
        sa = _sigmoid(ga_ref[...].astype(F32))
        sb = _sigmoid(gb_ref[...].astype(F32))
        mg_ref[...] = (sa * za + sb * zb).astype(BF16)
        za_ref[...] = za.astype(BF16)
        zb_ref[...] = zb.astype(BF16)

    yspec = pl.BlockSpec((tm, kdim), lambda i, j: (i, 0))
    wspec = pl.BlockSpec((None, kdim, ns), lambda i, j: (j, 0, 0))
    ospec = pl.BlockSpec((tm, ns), lambda i, j: (i, j))
    outs, couts = _pcall(body, name=name, grid=(t // tm, nsh),
                         in_specs=[yspec, yspec, wspec, wspec, ospec, pl.BlockSpec((tm, ns), lambda i, j: (i, nsh + j))],
                         out_specs=[ospec, ospec, ospec], out_shape=[jax.ShapeDtypeStruct((t, d), BF16)] * 3,
                         args=(ya, yb, wa, wb, gates, gates), sem=("parallel", "parallel"), comm=comm)
    return outs, couts


def _merge_bwd(dmg, gates, za, zb, *, name):
    t, d = dmg.shape
    tr = _tile(t, 256)

    def body(dm_ref, ga_ref, gb_ref, za_ref, zb_ref, dza_ref, dzb_ref, dg_ref):
        dm = dm_ref[...].astype(F32)
        sa = _sigmoid(ga_ref[...].astype(F32))
        sb = _sigmoid(gb_ref[...].astype(F32))
        dza_ref[...] = (dm * sa).astype(BF16)
        dzb_ref[...] = (dm * sb).astype(BF16)
        dg_ref[:, 0:d] = (dm * za_ref[...].astype(F32) * (sa * (1.0 - sa))).astype(BF16)
        dg_ref[:, d:2 * d] = (dm * zb_ref[...].astype(F32) * (sb * (1.0 - sb))).astype(BF16)

    row = pl.BlockSpec((tr, d), lambda i: (i, 0))
    return _pcall(body, name=name, grid=(t // tr,),
                  in_specs=[row, row, pl.BlockSpec((tr, d), lambda i: (i, 1)), row, row],
                  out_specs=[row, row, pl.BlockSpec((tr, 2 * d), lambda i: (i, 0))],
                  out_shape=[jax.ShapeDtypeStruct((t, d), BF16), jax.ShapeDtypeStruct((t, d), BF16),
                             jax.ShapeDtypeStruct((t, 2 * d), BF16)],
                  args=(dmg, gates, gates, za, zb), sem=("parallel",))[0]


def _shift_down(ext, k, rows):
    return pltpu.roll(ext, k, 0)[8:8 + rows]


def _conv_fwd(ab, cw, cb, *, name):
    t = ab.shape[0]
    dff = ab.shape[1] // 2
    tr, tc = _tile(t, 512), _tile(dff, 512)
    nc = dff // tc
    r8 = tr // 8

    def body(a_ref, ap_ref, b_ref, cw_ref, cb_ref, g_ref):
        i = pl.program_id(0)
        prev = ap_ref[...].astype(F32) * jnp.where(i > 0, 1.0, 0.0)
        a = a_ref[...].astype(F32)
        ext = jnp.concatenate([prev, a], axis=0)
        acc = cb_ref[...] + cw_ref[0:1, :] * _shift_down(ext, 2, tr) + cw_ref[1:2, :] * _shift_down(ext, 1, tr) \
            + cw_ref[2:3, :] * a
        g_ref[...] = (_gelu(acc) * b_ref[...].astype(F32)).astype(BF16)

    return _pcall(body, name=name, grid=(t // tr, nc),
                  in_specs=[pl.BlockSpec((tr, tc), lambda i, j: (i, j)),
                            pl.BlockSpec((8, tc), lambda i, j: (jnp.maximum(i * r8 - 1, 0), j)),
                            pl.BlockSpec((tr, tc), lambda i, j: (i, nc + j)),
                            pl.BlockSpec((CONV_WIDTH, tc), lambda i, j: (0, j)),
                            pl.BlockSpec((1, tc), lambda i, j: (0, j))],
                  out_specs=[pl.BlockSpec((tr, tc), lambda i, j: (i, j))],
                  out_shape=[jax.ShapeDtypeStruct((t, dff), BF16)], args=(ab, ab, ab, cw, cb),
                  sem=("parallel", "parallel"))[0][0]


def _conv_bwd(ab, cw, cb, dgg, *, name):
    t = ab.shape[0]
    dff = ab.shape[1] // 2
    tr, tc = _tile(t, 256), _tile(dff, 512)
    nc, nr = dff // tc, t // tr
    r8 = tr // 8
    ext_rows = tr + 16

    def body(a_ref, ap_ref, an_ref, b_ref, bn_ref, dg_ref, dgn_ref, cw_ref, cb_ref,
             da_ref, db_ref, dcw_ref, dcb_ref):
        i = pl.program_id(1)
        has_prev = jnp.where(i > 0, 1.0, 0.0)
        has_next = jnp.where(i < nr - 1, 1.0, 0.0)
        a_ext = jnp.concatenate([ap_ref[...].astype(F32) * has_prev, a_ref[...].astype(F32),
                                 an_ref[...].astype(F32)], axis=0)
        b_ext = jnp.concatenate([b_ref[0:8, :].astype(F32), b_ref[...].astype(F32), bn_ref[...].astype(F32)], axis=0)
        dg_ext = jnp.concatenate([dg_ref[0:8, :].astype(F32), dg_ref[...].astype(F32),
                                  dgn_ref[...].astype(F32) * has_next], axis=0)
        w0, w1, w2 = cw_ref[0:1, :], cw_ref[1:2, :], cw_ref[2:3, :]
        a_m2 = pltpu.roll(a_ext, 2, 0)
        a_m1 = pltpu.roll(a_ext, 1, 0)
        acc = cb_ref[...] + w0 * a_m2 + w1 * a_m1 + w2 * a_ext
        gel, gel_grad = _gelu_and_grad(acc)
        dacc = dg_ext * b_ext * gel_grad
        d_a = w2 * dacc + w1 * pltpu.roll(dacc, ext_rows - 1, 0) + w0 * pltpu.roll(dacc, ext_rows - 2, 0)
        da_ref[...] = d_a[8:8 + tr].astype(BF16)
        db_ref[...] = (dg_ext[8:8 + tr] * gel[8:8 + tr]).astype(BF16)
        dm = dacc[8:8 + tr]
        dcw = jnp.concatenate([jnp.sum(dm * a_m2[8:8 + tr], axis=0, keepdims=True),
                               jnp.sum(dm * a_m1[8:8 + tr], axis=0, keepdims=True),
                               jnp.sum(dm * a_ext[8:8 + tr], axis=0, keepdims=True)], axis=0)
        dcb = jnp.sum(dm, axis=0, keepdims=True)

        @pl.when(i == 0)
        def _():
            dcw_ref[...] = dcw
            dcb_ref[...] = dcb

        @pl.when(i > 0)
        def _():
            dcw_ref[...] += dcw
            dcb_ref[...] += dcb

    cur = lambda off: pl.BlockSpec((tr, tc), lambda j, i: (i, off + j))
    prv = lambda off: pl.BlockSpec((8, tc), lambda j, i: (jnp.maximum(i * r8 - 1, 0), off + j))
    nxt = lambda off: pl.BlockSpec((8, tc), lambda j, i: (jnp.minimum((i + 1) * r8, nr * r8 - 1), off + j))
    return _pcall(body, name=name, grid=(nc, nr),
                  in_specs=[cur(0), prv(0), nxt(0), cur(nc), nxt(nc), cur(0), nxt(0),
                            pl.BlockSpec((CONV_WIDTH, tc), lambda j, i: (0, j)),
                            pl.BlockSpec((1, tc), lambda j, i: (0, j))],
                  out_specs=[cur(0), cur(0), pl.BlockSpec((CONV_WIDTH, tc), lambda j, i: (0, j)),
                             pl.BlockSpec((1, tc), lambda j, i: (0, j))],
                  out_shape=[jax.ShapeDtypeStruct((t, dff), BF16), jax.ShapeDtypeStruct((t, dff), BF16),
                             jax.ShapeDtypeStruct((CONV_WIDTH, dff), F32), jax.ShapeDtypeStruct((1, dff), F32)],
                  args=(ab, ab, ab, ab, ab, dgg, dgg, cw, cb), sem=("parallel", "arbitrary"))[0]


def _row_tile(rows, cols):
    for cand in (512, 256, 128, 64, 32, 16, 8):
        if rows % cand == 0 and cand * cols * 4 <= 2 * 1024 * 1024:
            return cand
    return rows


def _pair_add(grad8, recv4, cidx, *, name):
    _, rows, cols = grad8.shape
    tr = _row_tile(rows, cols)

    def body(c_ref, g_ref, r_ref, o_ref):
        o_ref[...] = (g_ref[...].astype(F32) + r_ref[...].astype(F32)).astype(BF16)

    blk = (None, tr, cols)
    return _pcall(body, name=name, grid=(4, rows // tr), prefetch=1,
                  in_specs=[pl.BlockSpec(blk, lambda k, i, c_ref: (2 * k + c_ref[0], i, 0)),
                            pl.BlockSpec(blk, lambda k, i, c_ref: (k, i, 0))],
                  out_specs=[pl.BlockSpec(blk, lambda k, i, c_ref: (k, i, 0))],
                  out_shape=[jax.ShapeDtypeStruct((4, rows, cols), BF16)], args=(cidx, grad8, recv4),
                  sem=("parallel", "parallel"))[0][0]


def _adamw_math(w, g, m, v):
    m = ADAM_B1 * m + (1.0 - ADAM_B1) * g
    v = ADAM_B2 * v + (1.0 - ADAM_B2) * (g * g)
    m_hat = m / (1.0 - ADAM_B1 ** ADAM_STEP)
    v_hat = v / (1.0 - ADAM_B2 ** ADAM_STEP)
    delta = -ADAM_LR * (m_hat / (jnp.sqrt(v_hat) + ADAM_EPS) + ADAM_WD * w)
    return delta, m, v


def _shard_adamw(part4, recv3, w, m, v, kidx, layer, bufs, *, name):
    depth, rows, cols = w.shape
    tr = _row_tile(rows, cols)

    def body(k_ref, p_ref, r0_ref, r1_ref, r2_ref, w_ref, m_ref, v_ref, *rest):
        g_out, d_out, m_out, v_out = rest[-4:]
        g = ((p_ref[...].astype(F32) + r0_ref[...].astype(F32)) + r1_ref[...].astype(F32)) + r2_ref[...].astype(F32)
        delta, mn, vn = _adamw_math(w_ref[...], g, m_ref[...], v_ref[...])
        g_out[...] = g
        d_out[...] = delta
        m_out[...] = mn
        v_out[...] = vn

    blk = (None, tr, cols)
    rspec = lambda j: pl.BlockSpec(blk, lambda i, k_ref: (j, i, 0))
    espec = pl.BlockSpec(blk, lambda i, k_ref: (layer, i, 0))
    in_specs = [pl.BlockSpec(blk, lambda i, k_ref: (k_ref[0], i, 0)), rspec(0), rspec(1), rspec(2), espec, espec, espec]
    args = [kidx, part4, recv3, recv3, recv3, w, m, v]
    aliases = {}
    if bufs is not None:
        in_specs += [ANY] * 4
        aliases = {8 + q: q for q in range(4)}
        args += list(bufs)
    return _pcall(body, name=name, grid=(rows // tr,), prefetch=1, in_specs=in_specs, out_specs=[espec] * 4,
                  out_shape=[jax.ShapeDtypeStruct((depth, rows, cols), F32)] * 4, args=args, sem=("parallel",),
                  aliases=aliases)[0]


def _small_adamw(gath, w, m, v, *, name):
    rows = w.shape[0]

    def body(g_ref, w_ref, m_ref, v_ref, g_out, d_out, m_out, v_out):
        g = g_ref[0]
        for dev in range(1, N_DEV):
            g = g + g_ref[dev]
        delta, mn, vn = _adamw_math(w_ref[...], g, m_ref[...], v_ref[...])
        g_out[...] = g
        d_out[...] = delta
        m_out[...] = mn
        v_out[...] = vn

    tr = _row_tile(rows, 128 * N_DEV)
    espec = pl.BlockSpec((tr, 128), lambda i: (i, 0))
    return _pcall(body, name=name, grid=(rows // tr,),
                  in_specs=[pl.BlockSpec((N_DEV, tr, 128), lambda i: (0, i, 0)), espec, espec, espec],
                  out_specs=[espec] * 4, out_shape=[jax.ShapeDtypeStruct((rows, 128), F32)] * 4,
                  args=(gath, w, m, v), sem=("parallel",))[0]


def _pack(arrs):
    flat = jnp.concatenate([a.reshape(-1) for a in arrs])
    total = flat.shape[0]
    rows = -(-total // (128 * 64)) * 64
    return jnp.pad(flat, (0, rows * 128 - total)).reshape(rows, 128)


def _unpack(packed, like):
    flat = packed.reshape(-1)
    out, off = [], 0
    for a in like:
        out.append(flat[off:off + a.size].reshape(a.shape))
        off += a.size
    return out


def kernel(x, g_mix, w_in, b_forget, g_sgu, w_spatial, b_spatial, w_branch_a, w_branch_b, w_out, g_ffn, w_up, conv_w, conv_b, w_down, g_final, loss_target, m_g_mix, m_w_in, m_b_forget, m_g_sgu, m_w_spatial, m_b_spatial, m_w_branch_a, m_w_branch_b, m_w_out, m_g_ffn, m_w_up, m_conv_w, m_conv_b, m_w_down, m_g_final, v_g_mix, v_w_in, v_b_forget, v_g_sgu, v_w_spatial, v_b_spatial, v_w_branch_a, v_w_branch_b, v_w_out, v_g_ffn, v_w_up, v_conv_w, v_conv_b, v_w_down, v_g_final):
    depth, d = g_mix.shape
    heads = b_forget.shape[1]
    fw = heads * HEAD_DIM
    sw = g_sgu.shape[1]
    dff = conv_b.shape[1]
    t = x.shape[1]
    nsi = w_in.shape[2]
    nsu = w_up.shape[2]
    o_f, o_u, o_g = 3 * fw, 3 * fw + heads, 3 * fw + heads + 2 * sw

    bpad = jnp.pad(b_forget, ((0, 0), (0, 128 - heads)))
    bst = jnp.pad(jnp.swapaxes(b_spatial, 1, 2), ((0, 0), (0, 0), (0, 128 - b_spatial.shape[1])))

    def mixer_shards(l):
        return [w_in[l].astype(BF16), w_branch_a[l].astype(BF16), w_branch_b[l].astype(BF16),
                w_out[l].astype(BF16), conv_w[l]]

    def ffn_shards(l):
        return [w_up[l].astype(BF16), w_down[l].astype(BF16)]

    def unpack_mixer(bufs):
        g_in, g_wa, g_wb, g_wo, g_cw = bufs
        f_in = jnp.moveaxis(g_in, 0, 1).reshape(d, N_DEV * nsi)
        return dict(wqkv=f_in[:, :o_f], wf=jnp.pad(f_in[:, o_f:o_u], ((0, 0), (0, 128 - heads))),
                    wuv=f_in[:, o_u:o_g], wg=f_in[:, o_g:], wa=g_wa, wb=g_wb,
                    wo=g_wo.reshape(d, d), cw=jnp.moveaxis(g_cw, 0, 1).reshape(CONV_WIDTH, dff))

    mixer_w = [None] * depth
    ffn_w = [None] * depth
    first = _comm_only(_gather_first(mixer_shards(0)), name="gather_mixer_first_0")
    mixer_w[0] = unpack_mixer(_comm_only(_gather_second(first), name="gather_mixer_second_0"))

    xs = x[0]
    saved = []
    for l in range(depth):
        n = lambda s: f"{s}_{l}"
        mw = mixer_w[l]
        h = _rms_fwd(xs, g_mix[l][None], name=n("rms_mix"))
        qkv = _mm(h, mw["wqkv"], name=n("proj_qkv"))
        uv = _mm(h, mw["wuv"], name=n("proj_uv"))
        gates = _mm(h, mw["wg"], name=n("proj_gates"))
        flog = _mm(h, mw["wf"], out_dtype=F32, name=n("proj_forget"))
        c3 = _forget_fwd(flog, bpad[l][None], name=n("forget_fwd")).reshape(128, 1, t)
        (ya, ya32, lse), ffn_first = _fox_fwd(qkv, c3, heads, name=n("fox_fwd"), comm=_gather_first(ffn_shards(l)))
        yb = _sgu_fwd(uv, g_sgu[l][None], w_spatial[l], bst[l], name=n("sgu_fwd"))
        (merged, za, zb), _ = _merge_fwd(ya, yb, mw["wa"], mw["wb"], gates, name=n("merge_fwd"))
        x1, ffn_bufs = _mm(merged, mw["wo"], out_dtype=F32, res=xs, name=n("out_proj"), comm=_gather_second(ffn_first))
        g_wu, g_wd = ffn_bufs
        ffn_w[l] = dict(wu=g_wu, wd=g_wd.reshape(dff, d))
        h2 = _rms_fwd(x1, g_ffn[l][None], name=n("rms_ffn"))
        if l + 1 < depth:
            ab, nxt_first = _mm(h2, g_wu, b_cols=True, name=n("ffn_up"), comm=_gather_first(mixer_shards(l + 1)))
        else:
            ab = _mm(h2, g_wu, b_cols=True, name=n("ffn_up"))
        gg = _conv_fwd(ab, mw["cw"], conv_b[l][None], name=n("conv_fwd"))
        if l + 1 < depth:
            x2, nxt = _mm(gg, ffn_w[l]["wd"], out_dtype=F32, res=x1, name=n("ffn_down"), comm=_gather_second(nxt_first))
            mixer_w[l + 1] = unpack_mixer(nxt)
        else:
            x2 = _mm(gg, ffn_w[l]["wd"], out_dtype=F32, res=x1, name=n("ffn_down"))
        saved.append((xs, h, qkv, uv, gates, flog, c3, ya, ya32, lse, yb, merged, za, zb, x1, h2, ab, gg))
        xs = x2

    loss_row, dx, dxb, d_g_final = _final_loss(xs, g_final[None], loss_target[0], name="final_loss")

    cidx = lax.axis_index("c").astype(jnp.int32).reshape(1)
    kidx = (2 * lax.axis_index("x") + lax.axis_index("y")).astype(jnp.int32).reshape(1)
    small_g = {k: [None] * depth for k in ("g_mix", "b_forget", "g_sgu", "w_spatial", "b_spatial", "g_ffn", "conv_b")}
    parts = {}
    recvs = {}
    mixer_names = ["w_in", "w_branch_a", "w_branch_b", "w_out"]
    ffn_names = ["w_up", "w_down", "conv_w"]
    for nm in mixer_names + ffn_names:
        parts[nm], recvs[nm] = [None] * depth, [None] * depth
    pending_mixer = None
    for l in reversed(range(depth)):
        n = lambda s: f"{s}_{l}"
        mw, fw_ = mixer_w[l], ffn_w[l]
        xs, h, qkv, uv, gates, flog, c3, ya, ya32, lse, yb, merged, za, zb, x1, h2, ab, gg = saved[l]
        g_wd = _mm(gg, dxb, ta=True, name=n("d_w_down"))
        dgg = _mm(dxb, fw_["wd"], tb=True, name=n("d_gg"))
        d_a, d_b, d_cw, d_cb = _conv_bwd(ab, mw["cw"], conv_b[l][None], dgg, name=n("conv_bwd"))
        small_g["conv_b"][l] = d_cb[0]
        g_wu = _mm(h2, d_a, ta=True, out_cols=(N_DEV, 0, nsu), name=n("d_w_up_a"))
        if pending_mixer is not None:
            g_wu, got = _mm(h2, d_b, ta=True, out_cols=(N_DEV, N_DEV // 2, nsu), out_alias=g_wu, name=n("d_w_up_b"),
                            comm=_scatter_second(pending_mixer))
            for nm, r in zip(mixer_names, got):
                recvs[nm][l + 1] = r
        else:
            g_wu = _mm(h2, d_b, ta=True, out_cols=(N_DEV, N_DEV // 2, nsu), out_alias=g_wu, name=n("d_w_up_b"))
        ffn_grads = [g_wu, g_wd.reshape(N_DEV, dff // N_DEV, d),
                     jnp.moveaxis(d_cw.reshape(CONV_WIDTH, N_DEV, dff // N_DEV), 1, 0)]
        dh2 = _mm(d_a, fw_["wu"], tb=True, b_cols=True, out_dtype=F32, name=n("d_h2_a"))
        dh2, from_sib = _mm(d_b, fw_["wu"], tb=True, b_cols=True, b_off=N_DEV // 2, out_dtype=F32, res=dh2,
                            name=n("d_h2_b"), comm=_scatter_first(ffn_grads))
        ffn_parts = [_pair_add(g8, r4, cidx, name=n(f"pair_add_{nm}")) for nm, g8, r4 in zip(ffn_names, ffn_grads, from_sib)]
        for nm, p in zip(ffn_names, ffn_parts):
            parts[nm][l] = p
        dx, dxb, dg = _rms_bwd(x1, g_ffn[l][None], dh2, dx, name=n("rms_ffn_bwd"))
        small_g["g_ffn"][l] = dg[0]
        g_wo = _mm(merged, dxb, ta=True, name=n("d_w_out"))
        dmg = _mm(dxb, mw["wo"], tb=True, name=n("d_merged"))
        dza, dzb, dgates = _merge_bwd(dmg, gates, za, zb, name=n("merge_bwd"))
        g_wa = _mm(ya, dza, ta=True, out_cols=(N_DEV, 0, d // N_DEV), name=n("d_w_a"))
        g_wb = _mm(yb, dzb, ta=True, out_cols=(N_DEV, 0, d // N_DEV), name=n("d_w_b"))
        dya = _mm(dza, mw["wa"], tb=True, b_cols=True, name=n("d_ya"))
        dyb = _mm(dzb, mw["wb"], tb=True, b_cols=True, name=n("d_yb"))
        duv, d_ws, d_bst, d_gs = _sgu_bwd(uv, g_sgu[l][None], w_spatial[l], bst[l], dyb, name=n("sgu_bwd"))
        small_g["w_spatial"][l], small_g["g_sgu"][l] = d_ws, d_gs[0]
        small_g["b_spatial"][l] = d_bst[:, :b_spatial.shape[1]].T
        (dq, dk, dv, dc3), got = _fox_bwd(qkv, c3, ya32, dya, lse, heads, name=n("fox_bwd"),
                                          comm=_scatter_second(ffn_parts))
        for nm, r in zip(ffn_names, got):
            recvs[nm][l] = r
        dct = jnp.pad(dc3.reshape(heads, t), ((0, 128 - heads), (0, 0)))
        dflog, d_bf = _forget_bwd(flog, bpad[l][None], dct, name=n("forget_bwd"))
        small_g["b_forget"][l] = d_bf[0, :heads]
        gw = [_mm(h, dq, ta=True, name=n("d_w_q")), _mm(h, dk, ta=True, name=n("d_w_k")),
              _mm(h, dv, ta=True, name=n("d_w_v")), _mm(h, dflog, ta=True, name=n("d_w_forget"))[:, :heads],
              _mm(h, duv, ta=True, name=n("d_w_uv")), _mm(h, dgates, ta=True, name=n("d_w_gates"))]
        g_in = jnp.moveaxis(jnp.concatenate(gw, axis=1).reshape(d, N_DEV, nsi), 1, 0)
        mixer_grads = [g_in, g_wa, g_wb, g_wo.reshape(N_DEV, d // N_DEV, d)]
        dh = _mm(dflog, mw["wf"], tb=True, out_dtype=F32, name=n("d_h_forget"))
        dh = _mm(dq, mw["wqkv"][:, :fw], tb=True, out_dtype=F32, res=dh, name=n("d_h_q"))
        dh = _mm(dk, mw["wqkv"][:, fw:2 * fw], tb=True, out_dtype=F32, res=dh, name=n("d_h_k"))
        dh = _mm(dv, mw["wqkv"][:, 2 * fw:], tb=True, out_dtype=F32, res=dh, name=n("d_h_v"))
        dh = _mm(duv, mw["wuv"], tb=True, out_dtype=F32, res=dh, name=n("d_h_uv"))
        dh, from_sib = _mm(dgates, mw["wg"], tb=True, out_dtype=F32, res=dh, name=n("d_h_gates"),
                           comm=_scatter_first(mixer_grads))
        pending_mixer = [_pair_add(g8, r4, cidx, name=n(f"pair_add_{nm}"))
                         for nm, g8, r4 in zip(mixer_names, mixer_grads, from_sib)]
        for nm, p in zip(mixer_names, pending_mixer):
            parts[nm][l] = p
        dx, dxb, dg = _rms_bwd(xs, g_mix[l][None], dh, dx, name=n("rms_mix_bwd"))
        small_g["g_mix"][l] = dg[0]
    grad_x = dx[None]
    for nm, r in zip(mixer_names, _comm_only(_scatter_second(pending_mixer), name="scatter_mixer_second_0")):
        recvs[nm][0] = r

    weights = {"w_in": (w_in, m_w_in, v_w_in), "w_branch_a": (w_branch_a, m_w_branch_a, v_w_branch_a),
               "w_branch_b": (w_branch_b, m_w_branch_b, v_w_branch_b), "w_out": (w_out, m_w_out, v_w_out),
               "w_up": (w_up, m_w_up, v_w_up), "conv_w": (conv_w, m_conv_w, v_conv_w),
               "w_down": (w_down, m_w_down, v_w_down)}
    res = {}
    for nm, (w, m, v) in weights.items():
        bufs = None
        for l in range(depth):
            bufs = _shard_adamw(parts[nm][l], recvs[nm][l], w, m, v, kidx, l, bufs, name=f"adamw_{nm}_{l}")
        res[nm] = bufs

    small = ["g_mix", "b_forget", "g_sgu", "w_spatial", "b_spatial", "g_ffn", "conv_b", "g_final", "loss"]
    zero = jnp.zeros((1,), F32)
    small_w = [g_mix, b_forget, g_sgu, w_spatial, b_spatial, g_ffn, conv_b, g_final, zero]
    small_m = [m_g_mix, m_b_forget, m_g_sgu, m_w_spatial, m_b_spatial, m_g_ffn, m_conv_b, m_g_final, zero]
    small_v = [v_g_mix, v_b_forget, v_g_sgu, v_w_spatial, v_b_spatial, v_g_ffn, v_conv_b, v_g_final, zero]
    small_grads = [jnp.stack(small_g[nm]) for nm in small[:-2]] + [d_g_final[0], loss_row[0, 0:1]]
    first = _comm_only(_gather_first([_pack(small_grads)]), name="gather_small_first")
    (gath,) = _comm_only(_gather_second(first), name="gather_small_second")
    outs = _small_adamw(gath, _pack(small_w), _pack(small_m), _pack(small_v), name="adamw_replicated")
    for nm, vals in zip(small, zip(*[_unpack(o, small_w) for o in outs])):
        res[nm] = list(vals)
    loss = res["loss"][0][0]

    order = ["g_mix", "w_in", "b_forget", "g_sgu", "w_spatial", "b_spatial", "w_branch_a", "w_branch_b", "w_out",
             "g_ffn", "w_up", "conv_w", "conv_b", "w_down", "g_final"]
    return (loss, grad_x, *[res[nm][0] for nm in order], *[res[nm][1] for nm in order],
            *[res[nm][2] for nm in order], *[res[nm][3] for nm in order])
```

```python
import numpy as np

import jax
import jax.numpy as jnp
from jax import lax
from jax.experimental import pallas as pl
from jax.experimental.pallas import tpu as pltpu

F32 = jnp.float32
BF16 = jnp.bfloat16

RMS_EPS = 1e-6
HEAD_DIM = 128
CONV_WIDTH = 3
ADAM_LR = 0.001
ADAM_B1 = 0.9
ADAM_B2 = 0.999
ADAM_EPS = 1e-08
ADAM_WD = 0.01
ADAM_STEP = 10
N_DEV = 8
V7X_VMEM_LIMIT = 58 * 1024 * 1024
MM_VMEM_BUDGET = 46 * 1024 * 1024
NEG = -1e30
ANY = pl.BlockSpec(memory_space=pl.ANY)
MESH = pl.DeviceIdType.MESH


def _tile(dim, pref):
    for t in (2048, 1024, 512, 256, 128):
        if t <= pref and dim % t == 0:
            return t
    return dim


def _gelu(x):
    t = jnp.tanh(0.7978845608028654 * (x + 0.044715 * (x * x * x)))
    return x * (0.5 * (1.0 + t))


def _gelu_and_grad(x):
    x2 = x * x
    t = jnp.tanh(0.7978845608028654 * (x + 0.044715 * (x2 * x)))
    cdf = 0.5 * (1.0 + t)
    dt = (1.0 - t * t) * (0.7978845608028654 * (1.0 + 3.0 * 0.044715 * x2))
    return x * cdf, cdf + 0.5 * x * dt


def _sigmoid(x):
    return 1.0 / (1.0 + jnp.exp(-x))


class _Comm:
    def __init__(self, srcs, new, alias, n_copies, emit):
        self.srcs = list(srcs)
        self.new = list(new)
        self.alias = list(alias)
        self.n_copies = n_copies
        self.emit = emit

    def split(self, couts):
        return [couts]


class _Join(_Comm):
    def __init__(self, comms):
        self.comms = comms
        srcs = [s for cm in comms for s in cm.srcs]
        new = [s for cm in comms for s in cm.new]
        alias = [s for cm in comms for s in cm.alias]

        def emit(src_refs, new_refs, alias_refs, sems):
            copies, s0, n0, a0, k0 = [], 0, 0, 0, sems[2]
            for cm in comms:
                copies += cm.emit(src_refs[s0:s0 + len(cm.srcs)], new_refs[n0:n0 + len(cm.new)],
                                  alias_refs[a0:a0 + len(cm.alias)], (sems[0], sems[1], k0))
                s0, n0, a0, k0 = s0 + len(cm.srcs), n0 + len(cm.new), a0 + len(cm.alias), k0 + cm.n_copies
            return copies

        super().__init__(srcs, new, alias, sum(cm.n_copies for cm in comms), emit)

    def split(self, couts):
        n_new = len(self.new)
        out, n0, a0 = [], 0, 0
        for cm in self.comms:
            out.append(couts[n0:n0 + len(cm.new)] + couts[n_new + a0:n_new + a0 + len(cm.alias)])
            n0, a0 = n0 + len(cm.new), a0 + len(cm.alias)
        return out


def _place():
    x, y, c = lax.axis_index("x"), lax.axis_index("y"), lax.axis_index("c")
    chips = [(1 - x, y), (x, 1 - y), (1 - x, 1 - y)]
    return x, y, c, chips


def _remote(src, dst, sems, k, to):
    return pltpu.make_async_remote_copy(src_ref=src, dst_ref=dst, send_sem=sems[0].at[sems[2] + k],
                                        recv_sem=sems[1].at[sems[2] + k], device_id=to, device_id_type=MESH)


def _gather_first(shards):
    n = len(shards)

    def emit(srcs, new, alias, sems):
        x, y, c, chips = _place()
        me = 4 * x + 2 * y + c
        copies = []
        for a in range(n):
            copies.append(pltpu.make_async_copy(srcs[a], new[a].at[me], sems[0].at[sems[2] + 5 * a + 4]))
            copies.append(_remote(srcs[a], new[a].at[me], sems, 5 * a, (x, y, 1 - c)))
            for j, chip in enumerate(chips):
                copies.append(_remote(srcs[a], new[a].at[me], sems, 5 * a + 1 + j, (*chip, c)))
        return copies

    new = [jax.ShapeDtypeStruct((N_DEV,) + s.shape, s.dtype) for s in shards]
    return _Comm(shards, new, [], 5 * n, emit)


def _gather_second(bufs):
    n = len(bufs)

    def emit(srcs, new, alias, sems):
        x, y, c, chips = _place()
        copies = []
        for a in range(n):
            for j, chip in enumerate(chips):
                blk = alias[a].at[4 * chip[0] + 2 * chip[1] + c]
                copies.append(_remote(blk, blk, sems, 3 * a + j, (x, y, 1 - c)))
        return copies

    return _Comm([], [], bufs, 3 * n, emit)


def _scatter_first(grads):
    n = len(grads)

    def emit(srcs, new, alias, sems):
        x, y, c, _ = _place()
        return [_remote(srcs[a].at[2 * k + 1 - c], new[a].at[k], sems, 4 * a + k, (x, y, 1 - c))
                for a in range(n) for k in range(4)]

    new = [jax.ShapeDtypeStruct((4,) + g.shape[1:], g.dtype) for g in grads]
    return _Comm(grads, new, [], 4 * n, emit)


def _scatter_second(parts):
    n = len(parts)

    def emit(srcs, new, alias, sems):
        x, y, c, chips = _place()
        return [_remote(srcs[a].at[2 * chip[0] + chip[1]], new[a].at[j], sems, 3 * a + j, (*chip, c))
                for a in range(n) for j, chip in enumerate(chips)]

    new = [jax.ShapeDtypeStruct((3,) + p.shape[1:], p.dtype) for p in parts]
    return _Comm(parts, new, [], 3 * n, emit)


def _pcall(body, *, name, grid, in_specs, out_specs, out_shape, args, sem, scratch=(), aliases=None, comm=None,
           prefetch=0):
    in_specs, out_specs, out_shape, scratch = list(in_specs), list(out_specs), list(out_shape), list(scratch)
    aliases = dict(aliases or {})
    n_in, n_out, n_scr = len(in_specs), len(out_shape), len(scratch)

    def make(body_fn, ins, outs, shapes, scr, sem_):
        params = pltpu.CompilerParams(dimension_semantics=sem_, vmem_limit_bytes=V7X_VMEM_LIMIT)
        if prefetch:
            spec = pltpu.PrefetchScalarGridSpec(num_scalar_prefetch=prefetch, grid=grid, in_specs=ins,
                                                out_specs=outs, scratch_shapes=scr)
            return pl.pallas_call(body_fn, name=name, grid_spec=spec, out_shape=shapes,
                                  input_output_aliases=aliases, compiler_params=params)
        return pl.pallas_call(body_fn, name=name, grid=grid, in_specs=ins, out_specs=outs, out_shape=shapes,
                              scratch_shapes=scr, input_output_aliases=aliases, compiler_params=params)

    if comm is None:
        return list(make(body, in_specs, out_specs, out_shape, scratch, sem)(*args)), []

    n_src, n_new, n_al = len(comm.srcs), len(comm.new), len(comm.alias)
    for a in range(n_al):
        aliases[prefetch + n_in + n_src + a] = n_out + n_new + a

    def wrapped(*refs):
        pre, refs = refs[:prefetch], refs[prefetch:]
        ins = refs[:n_in]
        src_refs = refs[n_in:n_in + n_src]
        o0 = n_in + n_src + n_al
        outs = refs[o0:o0 + n_out]
        new_refs = refs[o0 + n_out:o0 + n_out + n_new]
        alias_refs = refs[o0 + n_out + n_new:o0 + n_out + n_new + n_al]
        s0 = o0 + n_out + n_new + n_al
        scr = refs[s0:s0 + n_scr]
        send_sems, recv_sems = refs[s0 + n_scr], refs[s0 + n_scr + 1]
        first = pl.program_id(0) == 0
        last = pl.program_id(0) == grid[0] - 1
        for dim in range(1, len(grid)):
            first = first & (pl.program_id(dim) == 0)
            last = last & (pl.program_id(dim) == grid[dim] - 1)

        @pl.when(first)
        def _():
            for cp in comm.emit(src_refs, new_refs, alias_refs, (send_sems, recv_sems, 0)):
                cp.start()

        body(*pre, *ins, *outs, *scr)

        @pl.when(last)
        def _():
            for cp in comm.emit(src_refs, new_refs, alias_refs, (send_sems, recv_sems, 0)):
                cp.wait()

    call = make(wrapped, in_specs + [ANY] * (n_src + n_al), out_specs + [ANY] * (n_new + n_al),
                out_shape + comm.new + [jax.ShapeDtypeStruct(b.shape, b.dtype) for b in comm.alias],
                scratch + [pltpu.SemaphoreType.DMA((comm.n_copies,)), pltpu.SemaphoreType.DMA((comm.n_copies,))],
                ("arbitrary",) * len(grid))
    res = list(call(*args, *comm.srcs, *comm.alias))
    return res[:n_out], res[n_out:]


def _comm_only(comm, *, name):
    def body(o_ref):
        o_ref[...] = jnp.zeros_like(o_ref)

    _, couts = _pcall(body, name=name, grid=(1,), in_specs=[], out_specs=[pl.BlockSpec((8, 128), lambda i: (0, 0))],
                      out_shape=[jax.ShapeDtypeStruct((8, 128), F32)], args=(), sem=("arbitrary",), comm=comm)
    return couts


def _divisor_tiles(dim, cap):
    tiles = [t for t in range(128, min(dim, cap) + 1, 128) if dim % t == 0]
    return sorted(tiles, reverse=True) or [dim]


def _mm_tiles(m, n, k, obytes, has_res, tn_fixed=None, tk_fixed=None):
    tms = _divisor_tiles(m, 1408)
    tns = [tn_fixed] if tn_fixed else _divisor_tiles(n, 1408)
    tks = [tk_fixed] if tk_fixed else [k] + [tt for tt in _divisor_tiles(k, 2048) if tt != k]
    best, best_score = None, None
    for tk in tks:
        for tm in tms:
            for tn in tns:
                nk = k // tk
                use = 4 * tm * tk + 4 * tk * tn + 2 * tm * tn * obytes
                use += (8 * tm * tn if has_res else 0) + (4 * tm * tn if nk > 1 else 0)
                score = (nk == 1, min(tm, 1024), tn, tm, tk)
                if use <= MM_VMEM_BUDGET and (best is None or score > best_score):
                    best, best_score = (tm, tn, tk), score
    assert best is not None, (m, n, k)
    return best


def _mm(a, b, *, ta=False, tb=False, out_dtype=BF16, res=None, name, b_cols=False, b_off=0,
        out_cols=None, out_alias=None, comm=None):
    m, k = (a.shape[1], a.shape[0]) if ta else a.shape
    obytes = jnp.dtype(out_dtype).itemsize
    if b_cols and not tb:
        ns = b.shape[2]
        assert b.shape[1] == k
        n = b.shape[0] * ns
        tm, tn, tk = _mm_tiles(m, n, k, obytes, res is not None, tn_fixed=ns)
        b_spec = pl.BlockSpec((None, tk, ns), lambda i, j, kk: (j, kk, 0))
    elif b_cols:
        ns = b.shape[2]
        assert k % ns == 0
        n = b.shape[1]
        tm, tn, tk = _mm_tiles(m, n, k, obytes, res is not None, tk_fixed=ns)
        b_spec = pl.BlockSpec((None, tn, ns), lambda i, j, kk: (b_off + kk, j, 0))
    else:
        n = b.shape[0] if tb else b.shape[1]
        assert (b.shape[1] if tb else b.shape[0]) == k
        tm, tn, tk = _mm_tiles(m, n, k, obytes, res is not None, tn_fixed=out_cols[2] if out_cols is not None else None)
        b_spec = pl.BlockSpec((tn, tk), lambda i, j, kk: (j, kk)) if tb else pl.BlockSpec((tk, tn), lambda i, j, kk: (kk, j))
    nk = k // tk
    dn = (((0,) if ta else (1,), (1,) if tb else (0,)), ((), ()))
    n_extra = (res is not None) + (out_alias is not None)

    def body(*refs):
        a_ref, b_ref = refs[0], refs[1]
        r_ref = refs[2] if res is not None else None
        o_ref = refs[2 + n_extra]
        part = lax.dot_general(a_ref[...], b_ref[...], dn, preferred_element_type=F32)

        def finish(r):
            if r_ref is not None:
                r = r + r_ref[...]
            o_ref[...] = r.astype(out_dtype)

        if nk == 1:
            finish(part)
        else:
            acc_ref = refs[3 + n_extra]
            kk = pl.program_id(2)

            @pl.when(kk == 0)
            def _():
                acc_ref[...] = part

            @pl.when(kk > 0)
            def _():
                acc_ref[...] += part

            @pl.when(kk == nk - 1)
            def _():
                finish(acc_ref[...])

    a_spec = pl.BlockSpec((tk, tm), lambda i, j, kk: (kk, i)) if ta else pl.BlockSpec((tm, tk), lambda i, j, kk: (i, kk))
    in_specs, args, aliases = [a_spec, b_spec], [a, b], {}
    if res is not None:
        in_specs.append(pl.BlockSpec((tm, tn), lambda i, j, kk: (i, j)))
        args.append(res)
    if out_cols is not None:
        s_total, o_off, ns_o = out_cols
        assert tn == ns_o and n % ns_o == 0
        o_spec = pl.BlockSpec((None, tm, tn), lambda i, j, kk: (o_off + j, i, 0))
        o_shape = jax.ShapeDtypeStruct((s_total, m, tn), out_dtype)
        if out_alias is not None:
            in_specs.append(ANY)
            args.append(out_alias)
            aliases[len(args) - 1] = 0
    else:
        o_spec = pl.BlockSpec((tm, tn), lambda i, j, kk: (i, j))
        o_shape = jax.ShapeDtypeStruct((m, n), out_dtype)
    outs, couts = _pcall(body, name=name, grid=(m // tm, n // tn, nk), in_specs=in_specs, out_specs=[o_spec],
                         out_shape=[o_shape], args=args, sem=("parallel", "parallel", "arbitrary"),
                         scratch=[pltpu.VMEM((tm, tn), F32)] if nk > 1 else [], aliases=aliases, comm=comm)
    return (outs[0], couts) if comm is not None else outs[0]


def _rms_fwd(x, g, *, name):
    t, d = x.shape
    tr = _tile(t, 256)

    def body(x_ref, g_ref, h_ref):
        xf = x_ref[...]
        inv = lax.rsqrt(jnp.mean(xf * xf, axis=-1, keepdims=True) + RMS_EPS)
        h_ref[...] = ((xf * inv) * g_ref[...]).astype(BF16)

    row = pl.BlockSpec((tr, d), lambda i: (i, 0))
    return _pcall(body, name=name, grid=(t // tr,), in_specs=[row, pl.BlockSpec((1, d), lambda i: (0, 0))],
                  out_specs=[row], out_shape=[jax.ShapeDtypeStruct((t, d), BF16)], args=(x, g),
                  sem=("parallel",))[0][0]


def _rms_bwd(x, g, dh, dres, *, name):
    t, d = x.shape
    tr = _tile(t, 256)

    def body(x_ref, g_ref, dh_ref, dres_ref, dx_ref, dxb_ref, dg_ref):
        xf = x_ref[...]
        inv = lax.rsqrt(jnp.mean(xf * xf, axis=-1, keepdims=True) + RMS_EPS)
        xn = xf * inv
        dh_f = dh_ref[...].astype(F32)
        dxn = dh_f * g_ref[...]
        dx = dres_ref[...] + inv * (dxn - xn * jnp.mean(dxn * xn, axis=-1, keepdims=True))
        dx_ref[...] = dx
        dxb_ref[...] = dx.astype(BF16)
        part = jnp.sum(dh_f * xn, axis=0, keepdims=True)

        @pl.when(pl.program_id(0) == 0)
        def _():
            dg_ref[...] = part

        @pl.when(pl.program_id(0) > 0)
        def _():
            dg_ref[...] += part

    row = pl.BlockSpec((tr, d), lambda i: (i, 0))
    vec = pl.BlockSpec((1, d), lambda i: (0, 0))
    return _pcall(body, name=name, grid=(t // tr,), in_specs=[row, vec, row, row], out_specs=[row, row, vec],
                  out_shape=[jax.ShapeDtypeStruct((t, d), F32), jax.ShapeDtypeStruct((t, d), BF16),
                             jax.ShapeDtypeStruct((1, d), F32)], args=(x, g, dh, dres), sem=("arbitrary",))[0]


def _final_loss(x, g, target, *, name):
    t, d = x.shape
    tr = _tile(t, 256)

    def body(x_ref, g_ref, tg_ref, loss_ref, dx_ref, dxb_ref, dg_ref):
        xf = x_ref[...]
        gv = g_ref[...]
        inv = lax.rsqrt(jnp.mean(xf * xf, axis=-1, keepdims=True) + RMS_EPS)
        xn = xf * inv
        err = xn * gv - tg_ref[...]
        lpart = 0.5 * jnp.sum(jnp.mean(err * err, axis=-1, keepdims=True), axis=0, keepdims=True)
        dy = err * (1.0 / d)
        dxn = dy * gv
        dx = inv * (dxn - xn * jnp.mean(dxn * xn, axis=-1, keepdims=True))
        dx_ref[...] = dx
        dxb_ref[...] = dx.astype(BF16)
        gpart = jnp.sum(dy * xn, axis=0, keepdims=True)
        lrow = jnp.broadcast_to(lpart, (1, 128))

        @pl.when(pl.program_id(0) == 0)
        def _():
            dg_ref[...] = gpart
            loss_ref[...] = lrow

        @pl.when(pl.program_id(0) > 0)
        def _():
            dg_ref[...] += gpart
            loss_ref[...] += lrow

    row = pl.BlockSpec((tr, d), lambda i: (i, 0))
    vec = pl.BlockSpec((1, d), lambda i: (0, 0))
    lspec = pl.BlockSpec((1, 128), lambda i: (0, 0))
    return _pcall(body, name=name, grid=(t // tr,), in_specs=[row, vec, row], out_specs=[lspec, row, row, vec],
                  out_shape=[jax.ShapeDtypeStruct((1, 128), F32), jax.ShapeDtypeStruct((t, d), F32),
                             jax.ShapeDtypeStruct((t, d), BF16), jax.ShapeDtypeStruct((1, d), F32)],
                  args=(x, g, target), sem=("arbitrary",))[0]


def _tri_ones(n, upper):
    r = lax.broadcasted_iota(jnp.int32, (n, n), 0)
    c = lax.broadcasted_iota(jnp.int32, (n, n), 1)
    return jnp.where((r <= c) if upper else (r >= c), 1.0, 0.0).astype(F32)


def _forget_fwd(flog, bpad, *, name):
    t = flog.shape[0]
    tb = _tile(t, 512)

    def body(f_ref, b_ref, c_ref, carry):
        z = f_ref[...] + b_ref[...]
        lf = jnp.minimum(z, 0.0) - jnp.log(1.0 + jnp.exp(-jnp.abs(z)))
        lft = lf.T
        tri = _tri_ones(tb, upper=True)

        @pl.when(pl.program_id(0) == 0)
        def _():
            carry[...] = jnp.zeros_like(carry)

        cs = jnp.dot(lft, tri, preferred_element_type=F32, precision=lax.Precision.HIGHEST) + carry[:, 0:1]
        c_ref[...] = cs
        carry[...] = jnp.broadcast_to(cs[:, tb - 1:tb], carry.shape)

    return _pcall(body, name=name, grid=(t // tb,),
                  in_specs=[pl.BlockSpec((tb, 128), lambda i: (i, 0)), pl.BlockSpec((1, 128), lambda i: (0, 0))],
                  out_specs=[pl.BlockSpec((128, tb), lambda i: (0, i))],
                  out_shape=[jax.ShapeDtypeStruct((128, t), F32)], args=(flog, bpad), sem=("arbitrary",),
                  scratch=[pltpu.VMEM((128, 128), F32)])[0][0]


def _forget_bwd(flog, bpad, dct, *, name):
    t = flog.shape[0]
    tb = _tile(t, 512)
    nb = t // tb

    def body(f_ref, b_ref, dc_ref, df_ref, db_ref, carry):
        i = pl.program_id(0)

        @pl.when(i == 0)
        def _():
            carry[...] = jnp.zeros_like(carry)

        tri = _tri_ones(tb, upper=False)
        dl = jnp.dot(dc_ref[...], tri, preferred_element_type=F32, precision=lax.Precision.HIGHEST) + carry[:, 0:1]
        carry[...] = jnp.broadcast_to(dl[:, 0:1], carry.shape)
        z = f_ref[...] + b_ref[...]
        df = dl.T * _sigmoid(-z)
        df_ref[...] = df.astype(BF16)
        part = jnp.sum(df, axis=0, keepdims=True)

        @pl.when(i == 0)
        def _():
            db_ref[...] = part

        @pl.when(i > 0)
        def _():
            db_ref[...] += part

    rev = lambda i: (nb - 1 - i, 0)
    return _pcall(body, name=name, grid=(nb,),
                  in_specs=[pl.BlockSpec((tb, 128), rev), pl.BlockSpec((1, 128), lambda i: (0, 0)),
                            pl.BlockSpec((128, tb), lambda i: (0, nb - 1 - i))],
                  out_specs=[pl.BlockSpec((tb, 128), rev), pl.BlockSpec((1, 128), lambda i: (0, 0))],
                  out_shape=[jax.ShapeDtypeStruct((t, 128), BF16), jax.ShapeDtypeStruct((1, 128), F32)],
                  args=(flog, bpad, dct), sem=("arbitrary",), scratch=[pltpu.VMEM((128, 128), F32)])[0]


def _causal_pairs(nq, k_major):
    if k_major:
        pairs = [(i, j) for j in range(nq) for i in range(j, nq)]
    else:
        pairs = [(i, j) for i in range(nq) for j in range(i + 1)]
    return (jnp.asarray(np.array([p[0] for p in pairs], np.int32)),
            jnp.asarray(np.array([p[1] for p in pairs], np.int32)), len(pairs))


def _logits(q, k, cq, ck, scale, masked):
    s = lax.dot_general(q, k, (((1,), (1,)), ((), ())), preferred_element_type=F32)
    s = s * scale + (cq[:, 0:1] - ck)
    if masked:
        row = lax.broadcasted_iota(jnp.int32, s.shape, 0)
        col = lax.broadcasted_iota(jnp.int32, s.shape, 1)
        s = jnp.where(col <= row, s, NEG)
    return s


def _head_group(heads):
    return 4 if heads % 4 == 0 else (2 if heads % 2 == 0 else 1)


def _fox_fwd(qkv, c3, heads, *, name, comm=None):
    t = qkv.shape[0]
    tq = _tile(t, 512)
    nq = t // tq
    hb = _head_group(heads)
    ng = heads // hb
    scale = HEAD_DIM ** -0.5
    i_tab, j_tab, npairs = _causal_pairs(nq, k_major=False)

    def body(it_ref, jt_ref, q_ref, k_ref, v_ref, cq_ref, ck_ref, o_ref, o32_ref, lse_ref, m_s, l_s, acc_s):
        p_id = pl.program_id(1)
        i, j = it_ref[p_id], jt_ref[p_id]

        @pl.when(j == 0)
        def _():
            m_s[...] = jnp.full_like(m_s, NEG)
            l_s[...] = jnp.zeros_like(l_s)
            acc_s[...] = jnp.zeros_like(acc_s)

        def update(masked):
            for hh in range(hb):
                ls = slice(hh * 128, (hh + 1) * 128)
                s = _logits(q_ref[:, ls], k_ref[:, ls], cq_ref[hh], ck_ref[hh], scale, masked)
                m_prev = m_s[hh, :, 0:1]
                m_new = jnp.maximum(m_prev, jnp.max(s, axis=1, keepdims=True))
                alpha = jnp.exp(m_prev - m_new)
                p = jnp.exp(s - m_new)
                l_s[hh, :, 0:1] = alpha * l_s[hh, :, 0:1] + jnp.sum(p, axis=1, keepdims=True)
                p_hi = p.astype(BF16)
                p_lo = (p - p_hi.astype(F32)).astype(BF16)
                vb = v_ref[:, ls]
                pv = jnp.dot(p_hi, vb, preferred_element_type=F32) + jnp.dot(p_lo, vb, preferred_element_type=F32)
                acc_s[hh] = alpha * acc_s[hh] + pv
                m_s[hh, :, 0:1] = m_new

        @pl.when(j < i)
        def _():
            update(False)

        @pl.when(j == i)
        def _():
            update(True)
            for hh in range(hb):
                ls = slice(hh * 128, (hh + 1) * 128)
                l = l_s[hh, :, 0:1]
                o = acc_s[hh] / l
                o_ref[:, ls] = o.astype(BF16)
                o32_ref[:, ls] = o
                lse_ref[hh] = jnp.broadcast_to(m_s[hh, :, 0:1] + jnp.log(l), (tq, 128))

    w = hb * 128
    qb = lambda g, p, it, jt: (it[p], g)
    outs, couts = _pcall(
        body, name=name, grid=(ng, npairs), prefetch=2,
        in_specs=[pl.BlockSpec((tq, w), qb),
                  pl.BlockSpec((tq, w), lambda g, p, it, jt: (jt[p], ng + g)),
                  pl.BlockSpec((tq, w), lambda g, p, it, jt: (jt[p], 2 * ng + g)),
                  pl.BlockSpec((hb, 1, tq), lambda g, p, it, jt: (g, 0, it[p])),
                  pl.BlockSpec((hb, 1, tq), lambda g, p, it, jt: (g, 0, jt[p]))],
        out_specs=[pl.BlockSpec((tq, w), qb), pl.BlockSpec((tq, w), qb),
                   pl.BlockSpec((hb, tq, 128), lambda g, p, it, jt: (g, it[p], 0))],
        out_shape=[jax.ShapeDtypeStruct((t, heads * 128), BF16), jax.ShapeDtypeStruct((t, heads * 128), F32),
                   jax.ShapeDtypeStruct((heads, t, 128), F32)],
        args=(i_tab, j_tab, qkv, qkv, qkv, c3, c3), sem=("parallel", "arbitrary"),
        scratch=[pltpu.VMEM((hb, tq, 128), F32), pltpu.VMEM((hb, tq, 128), F32), pltpu.VMEM((hb, tq, 128), F32)],
        comm=comm)
    return outs, couts


def _fox_bwd(qkv, c3, o, do, lse, heads, *, name, comm=None):
    t = qkv.shape[0]
    tq = _tile(t, 512)
    nq = t // tq
    hb = _head_group(heads)
    ng = heads // hb
    scale = HEAD_DIM ** -0.5
    i_tab, j_tab, npairs = _causal_pairs(nq, k_major=True)

    def body(it_ref, jt_ref, q_ref, k_ref, v_ref, o_ref, do_ref, lse_ref, cq_ref, ck_ref,
             dq_ref, dk_ref, dv_ref, dc_ref, dq_s, dk_s, dv_s, dc_s):
        p_id = pl.program_id(1)
        i, j = it_ref[p_id], jt_ref[p_id]

        @pl.when(p_id == 0)
        def _():
            dq_s[...] = jnp.zeros_like(dq_s)

        @pl.when(i == j)
        def _():
            dk_s[...] = jnp.zeros_like(dk_s)
            dv_s[...] = jnp.zeros_like(dv_s)
            dc_s[...] = jnp.zeros_like(dc_s)

        def update(masked):
            r0 = pl.multiple_of(i * tq, tq)
            for hh in range(hb):
                ls = slice(hh * 128, (hh + 1) * 128)
                q, k, v, dob = q_ref[:, ls], k_ref[:, ls], v_ref[:, ls], do_ref[:, ls]
                s = _logits(q, k, cq_ref[hh], ck_ref[hh], scale, masked)
                p = jnp.exp(s - lse_ref[hh, :, 0:1])
                delta = jnp.sum(dob.astype(F32) * o_ref[:, ls], axis=1, keepdims=True)
                dp = lax.dot_general(dob, v, (((1,), (1,)), ((), ())), preferred_element_type=F32)
                ds = p * (dp - delta)
                pb, dsb = p.astype(BF16), ds.astype(BF16)
                dv_s[hh] += lax.dot_general(pb, dob, (((0,), (0,)), ((), ())), preferred_element_type=F32)
                dk_s[hh] += lax.dot_general(dsb, q, (((0,), (0,)), ((), ())), preferred_element_type=F32)
                dq_s[hh, pl.ds(r0, tq), :] += jnp.dot(dsb, k, preferred_element_type=F32) * scale
                dc_s[hh] -= jnp.sum(ds, axis=0, keepdims=True)

        @pl.when(i > j)
        def _():
            update(False)

        @pl.when(i == j)
        def _():
            update(True)

        @pl.when(i == nq - 1)
        def _():
            for hh in range(hb):
                ls = slice(hh * 128, (hh + 1) * 128)
                dk_ref[:, ls] = (dk_s[hh] * scale).astype(BF16)
                dv_ref[:, ls] = dv_s[hh].astype(BF16)
            dc_ref[...] = dc_s[...]

        @pl.when(p_id == npairs - 1)
        def _():
            for hh in range(hb):
                dq_ref[:, hh * 128:(hh + 1) * 128] = dq_s[hh].astype(BF16)

    w = hb * 128
    qb = lambda g, p, it, jt: (it[p], g)
    kb = lambda g, p, it, jt: (jt[p], g)
    outs, couts = _pcall(
        body, name=name, grid=(ng, npairs), prefetch=2,
        in_specs=[pl.BlockSpec((tq, w), qb),
                  pl.BlockSpec((tq, w), lambda g, p, it, jt: (jt[p], ng + g)),
                  pl.BlockSpec((tq, w), lambda g, p, it, jt: (jt[p], 2 * ng + g)),
                  pl.BlockSpec((tq, w), qb), pl.BlockSpec((tq, w), qb),
                  pl.BlockSpec((hb, tq, 128), lambda g, p, it, jt: (g, it[p], 0)),
                  pl.BlockSpec((hb, 1, tq), lambda g, p, it, jt: (g, 0, it[p])),
                  pl.BlockSpec((hb, 1, tq), lambda g, p, it, jt: (g, 0, jt[p]))],
        out_specs=[pl.BlockSpec((t, w), lambda g, p, it, jt: (0, g)), pl.BlockSpec((tq, w), kb),
                   pl.BlockSpec((tq, w), kb), pl.BlockSpec((hb, 1, tq), lambda g, p, it, jt: (g, 0, jt[p]))],
        out_shape=[jax.ShapeDtypeStruct((t, heads * 128), BF16), jax.ShapeDtypeStruct((t, heads * 128), BF16),
                   jax.ShapeDtypeStruct((t, heads * 128), BF16), jax.ShapeDtypeStruct((heads, 1, t), F32)],
        args=(i_tab, j_tab, qkv, qkv, qkv, o, do, lse, c3, c3), sem=("arbitrary", "arbitrary"),
        scratch=[pltpu.VMEM((hb, t, 128), F32), pltpu.VMEM((hb, tq, 128), F32), pltpu.VMEM((hb, tq, 128), F32),
                 pltpu.VMEM((hb, 1, tq), F32)], comm=comm)
    return outs, couts


def _tril_mask():
    r = lax.broadcasted_iota(jnp.int32, (128, 128), 0)
    c = lax.broadcasted_iota(jnp.int32, (128, 128), 1)
    return r >= c


def _sgu_fwd(uv, g, w, bst, *, name):
    t = uv.shape[0]
    sw = uv.shape[1] // 2
    groups = sw // 128
    tr = _tile(t, 512)

    def body(u_ref, v_ref, g_ref, w_ref, b_ref, y_ref):
        gv = _gelu(v_ref[...].astype(F32))
        inv = lax.rsqrt(jnp.mean(gv * gv, axis=-1, keepdims=True) + RMS_EPS)
        vn = ((gv * inv) * g_ref[...]).astype(BF16)
        gu = _gelu(u_ref[...].astype(F32))
        mask = _tril_mask()
        for gi in range(groups):
            wg = jnp.where(mask, w_ref[gi], 0.0).astype(BF16)
            bcol = b_ref[:, gi:gi + 1]
            cs = slice(gi * 128, (gi + 1) * 128)
            for ci in range(tr // 128):
                rs = slice(ci * 128, (ci + 1) * 128)
                mixed = jnp.dot(wg, vn[rs, cs], preferred_element_type=F32) + bcol
                y_ref[rs, cs] = (gu[rs, cs] * mixed).astype(BF16)

    return _pcall(body, name=name, grid=(t // tr,),
                  in_specs=[pl.BlockSpec((tr, sw), lambda i: (i, 0)), pl.BlockSpec((tr, sw), lambda i: (i, 1)),
                            pl.BlockSpec((1, sw), lambda i: (0, 0)),
                            pl.BlockSpec((groups, 128, 128), lambda i: (0, 0, 0)),
                            pl.BlockSpec((128, 128), lambda i: (0, 0))],
                  out_specs=[pl.BlockSpec((tr, sw), lambda i: (i, 0))],
                  out_shape=[jax.ShapeDtypeStruct((t, sw), BF16)], args=(uv, uv, g, w, bst),
                  sem=("parallel",))[0][0]


def _sgu_bwd(uv, g, w, bst, dy, *, name):
    t = uv.shape[0]
    sw = uv.shape[1] // 2
    groups = sw // 128
    tr = _tile(t, 256)
    nsteps = t // tr

    def body(u_ref, v_ref, g_ref, w_ref, b_ref, dy_ref, duv_ref, dw_ref, db_ref, dg_ref, dvn_s, dgu_s):
        step = pl.program_id(0)

        @pl.when(step == 0)
        def _():
            dw_ref[...] = jnp.zeros_like(dw_ref)
            db_ref[...] = jnp.zeros_like(db_ref)
            dg_ref[...] = jnp.zeros_like(dg_ref)

        vf = v_ref[...].astype(F32)
        gv, gv_grad = _gelu_and_grad(vf)
        inv = lax.rsqrt(jnp.mean(gv * gv, axis=-1, keepdims=True) + RMS_EPS)
        xn = gv * inv
        gvec = g_ref[...]
        vn = (xn * gvec).astype(BF16)
        uf = u_ref[...].astype(F32)
        gu, gu_grad = _gelu_and_grad(uf)
        dyf = dy_ref[...].astype(F32)
        mask = _tril_mask()
        lane = lax.broadcasted_iota(jnp.int32, (128, 128), 1)
        dball = jnp.zeros((128, 128), F32)
        for gi in range(groups):
            wg = jnp.where(mask, w_ref[gi], 0.0).astype(BF16)
            wgt = wg.T
            bcol = b_ref[:, gi:gi + 1]
            cs = slice(gi * 128, (gi + 1) * 128)
            dwg = jnp.zeros((128, 128), F32)
            dbg = jnp.zeros((128, 1), F32)
            for ci in range(tr // 128):
                rs = slice(ci * 128, (ci + 1) * 128)
                vnb = vn[rs, cs]
                mixed = jnp.dot(wg, vnb, preferred_element_type=F32) + bcol
                dgu_s[rs, cs] = dyf[rs, cs] * mixed
                dmix = dyf[rs, cs] * gu[rs, cs]
                dmb = dmix.astype(BF16)
                dvn_s[rs, cs] = jnp.dot(wgt, dmb, preferred_element_type=F32)
                dwg = dwg + lax.dot_general(dmb, vnb, (((1,), (1,)), ((), ())), preferred_element_type=F32)
                dbg = dbg + jnp.sum(dmix, axis=1, keepdims=True)
            dw_ref[gi] += dwg
            dball = dball + jnp.where(lane == gi, dbg, 0.0)
        db_ref[...] += dball
        dvn = dvn_s[...]
        dg_ref[...] += jnp.sum(dvn * xn, axis=0, keepdims=True)
        dxn = dvn * gvec
        dgv = inv * (dxn - xn * jnp.mean(dxn * xn, axis=-1, keepdims=True))
        duv_ref[:, 0:sw] = (dgu_s[...] * gu_grad).astype(BF16)
        duv_ref[:, sw:2 * sw] = (dgv * gv_grad).astype(BF16)

        @pl.when(step == nsteps - 1)
        def _():
            for gi in range(groups):
                dw_ref[gi] = jnp.where(mask, dw_ref[gi], 0.0)

    return _pcall(body, name=name, grid=(nsteps,),
                  in_specs=[pl.BlockSpec((tr, sw), lambda i: (i, 0)), pl.BlockSpec((tr, sw), lambda i: (i, 1)),
                            pl.BlockSpec((1, sw), lambda i: (0, 0)),
                            pl.BlockSpec((groups, 128, 128), lambda i: (0, 0, 0)),
                            pl.BlockSpec((128, 128), lambda i: (0, 0)), pl.BlockSpec((tr, sw), lambda i: (i, 0))],
                  out_specs=[pl.BlockSpec((tr, 2 * sw), lambda i: (i, 0)),
                             pl.BlockSpec((groups, 128, 128), lambda i: (0, 0, 0)),
                             pl.BlockSpec((128, 128), lambda i: (0, 0)), pl.BlockSpec((1, sw), lambda i: (0, 0))],
                  out_shape=[jax.ShapeDtypeStruct((t, 2 * sw), BF16), jax.ShapeDtypeStruct((groups, 128, 128), F32),
                             jax.ShapeDtypeStruct((128, 128), F32), jax.ShapeDtypeStruct((1, sw), F32)],
                  args=(uv, uv, g, w, bst, dy), sem=("arbitrary",),
                  scratch=[pltpu.VMEM((tr, sw), F32), pltpu.VMEM((tr, sw), F32)])[0]


def _merge_fwd(ya, yb, wa, wb, gates, *, name, comm=None):
    t, kdim = ya.shape
    nsh, _, ns = wa.shape
    d = nsh * ns
    tm = _tile(t, 1024)

    def body(ya_ref, yb_ref, wa_ref, wb_ref, ga_ref, gb_ref, mg_ref, za_ref, zb_ref):
        za = jnp.dot(ya_ref[...], wa_ref[...], preferred_element_type=F32)
        zb = jnp.dot(yb_ref[...], wb_ref[...], preferred_element_type=F32)
        sa = _sigmoid(ga_ref[...].astype(F32))
        sb = _sigmoid(gb_ref[...].astype(F32))
        mg_ref[...] = (sa * za + sb * zb).astype(BF16)
        za_ref[...] = za.astype(BF16)
        zb_ref[...] = zb.astype(BF16)

    yspec = pl.BlockSpec((tm, kdim), lambda i, j: (i, 0))
    wspec = pl.BlockSpec((None, kdim, ns), lambda i, j: (j, 0, 0))
    ospec = pl.BlockSpec((tm, ns), lambda i, j: (i, j))
    outs, couts = _pcall(body, name=name, grid=(t // tm, nsh),
                         in_specs=[yspec, yspec, wspec, wspec, ospec, pl.BlockSpec((tm, ns), lambda i, j: (i, nsh + j))],
                         out_specs=[ospec, ospec, ospec], out_shape=[jax.ShapeDtypeStruct((t, d), BF16)] * 3,
                         args=(ya, yb, wa, wb, gates, gates), sem=("parallel", "parallel"), comm=comm)
    return outs, couts


def _merge_bwd(dmg, gates, za, zb, *, name):
    t, d = dmg.shape
    tr = _tile(t, 256)

    def body(dm_ref, ga_ref, gb_ref, za_ref, zb_ref, dza_ref, dzb_ref, dg_ref):
        dm = dm_ref[...].astype(F32)
        sa = _sigmoid(ga_ref[...].astype(F32))
        sb = _sigmoid(gb_ref[...].astype(F32))
        dza_ref[...] = (dm * sa).astype(BF16)
        dzb_ref[...] = (dm * sb).astype(BF16)
        dg_ref[:, 0:d] = (dm * za_ref[...].astype(F32) * (sa * (1.0 - sa))).astype(BF16)
        dg_ref[:, d:2 * d] = (dm * zb_ref[...].astype(F32) * (sb * (1.0 - sb))).astype(BF16)

    row = pl.BlockSpec((tr, d), lambda i: (i, 0))
    return _pcall(body, name=name, grid=(t // tr,),
                  in_specs=[row, row, pl.BlockSpec((tr, d), lambda i: (i, 1)), row, row],
                  out_specs=[row, row, pl.BlockSpec((tr, 2 * d), lambda i: (i, 0))],
                  out_shape=[jax.ShapeDtypeStruct((t, d), BF16), jax.ShapeDtypeStruct((t, d), BF16),
                             jax.ShapeDtypeStruct((t, 2 * d), BF16)],
                  args=(dmg, gates, gates, za, zb), sem=("parallel",))[0]


def _shift_down(ext, k, rows):
    return pltpu.roll(ext, k, 0)[8:8 + rows]


def _conv_fwd(ab, cw, cb, *, name):
    t = ab.shape[0]
    dff = ab.shape[1] // 2
    tr, tc = _tile(t, 512), _tile(dff, 512)
    nc = dff // tc
    r8 = tr // 8

    def body(a_ref, ap_ref, b_ref, cw_ref, cb_ref, g_ref):
        i = pl.program_id(0)
        prev = ap_ref[...].astype(F32) * jnp.where(i > 0, 1.0, 0.0)
        a = a_ref[...].astype(F32)
        ext = jnp.concatenate([prev, a], axis=0)
        acc = cb_ref[...] + cw_ref[0:1, :] * _shift_down(ext, 2, tr) + cw_ref[1:2, :] * _shift_down(ext, 1, tr) \
            + cw_ref[2:3, :] * a
        g_ref[...] = (_gelu(acc) * b_ref[...].astype(F32)).astype(BF16)

    return _pcall(body, name=name, grid=(t // tr, nc),
                  in_specs=[pl.BlockSpec((tr, tc), lambda i, j: (i, j)),
                            pl.BlockSpec((8, tc), lambda i, j: (jnp.maximum(i * r8 - 1, 0), j)),
                            pl.BlockSpec((tr, tc), lambda i, j: (i, nc + j)),
                            pl.BlockSpec((CONV_WIDTH, tc), lambda i, j: (0, j)),
                            pl.BlockSpec((1, tc), lambda i, j: (0, j))],
                  out_specs=[pl.BlockSpec((tr, tc), lambda i, j: (i, j))],
                  out_shape=[jax.ShapeDtypeStruct((t, dff), BF16)], args=(ab, ab, ab, cw, cb),
                  sem=("parallel", "parallel"))[0][0]


def _conv_bwd(ab, cw, cb, dgg, *, name, comm=None):
    t = ab.shape[0]
    dff = ab.shape[1] // 2
    tr, tc = _tile(t, 256), _tile(dff, 512)
    nc, nr = dff // tc, t // tr
    r8 = tr // 8
    ext_rows = tr + 16

    def body(a_ref, ap_ref, an_ref, b_ref, bn_ref, dg_ref, dgn_ref, cw_ref, cb_ref,
             da_ref, db_ref, dcw_ref, dcb_ref):
        i = pl.program_id(1)
        has_prev = jnp.where(i > 0, 1.0, 0.0)
        has_next = jnp.where(i < nr - 1, 1.0, 0.0)
        a_ext = jnp.concatenate([ap_ref[...].astype(F32) * has_prev, a_ref[...].astype(F32),
                                 an_ref[...].astype(F32)], axis=0)
        b_ext = jnp.concatenate([b_ref[0:8, :].astype(F32), b_ref[...].astype(F32), bn_ref[...].astype(F32)], axis=0)
        dg_ext = jnp.concatenate([dg_ref[0:8, :].astype(F32), dg_ref[...].astype(F32),
                                  dgn_ref[...].astype(F32) * has_next], axis=0)
        w0, w1, w2 = cw_ref[0:1, :], cw_ref[1:2, :], cw_ref[2:3, :]
        a_m2 = pltpu.roll(a_ext, 2, 0)
        a_m1 = pltpu.roll(a_ext, 1, 0)
        acc = cb_ref[...] + w0 * a_m2 + w1 * a_m1 + w2 * a_ext
        gel, gel_grad = _gelu_and_grad(acc)
        dacc = dg_ext * b_ext * gel_grad
        d_a = w2 * dacc + w1 * pltpu.roll(dacc, ext_rows - 1, 0) + w0 * pltpu.roll(dacc, ext_rows - 2, 0)
        da_ref[...] = d_a[8:8 + tr].astype(BF16)
        db_ref[...] = (dg_ext[8:8 + tr] * gel[8:8 + tr]).astype(BF16)
        dm = dacc[8:8 + tr]
        dcw = jnp.concatenate([jnp.sum(dm * a_m2[8:8 + tr], axis=0, keepdims=True),
                               jnp.sum(dm * a_m1[8:8 + tr], axis=0, keepdims=True),
                               jnp.sum(dm * a_ext[8:8 + tr], axis=0, keepdims=True)], axis=0)
        dcb = jnp.sum(dm, axis=0, keepdims=True)

        @pl.when(i == 0)
        def _():
            dcw_ref[...] = dcw
            dcb_ref[...] = dcb

        @pl.when(i > 0)
        def _():
            dcw_ref[...] += dcw
            dcb_ref[...] += dcb

    cur = lambda off: pl.BlockSpec((tr, tc), lambda j, i: (i, off + j))
    prv = lambda off: pl.BlockSpec((8, tc), lambda j, i: (jnp.maximum(i * r8 - 1, 0), off + j))
    nxt = lambda off: pl.BlockSpec((8, tc), lambda j, i: (jnp.minimum((i + 1) * r8, nr * r8 - 1), off + j))
    return _pcall(body, name=name, grid=(nc, nr),
                  in_specs=[cur(0), prv(0), nxt(0), cur(nc), nxt(nc), cur(0), nxt(0),
                            pl.BlockSpec((CONV_WIDTH, tc), lambda j, i: (0, j)),
                            pl.BlockSpec((1, tc), lambda j, i: (0, j))],
                  out_specs=[cur(0), cur(0), pl.BlockSpec((CONV_WIDTH, tc), lambda j, i: (0, j)),
                             pl.BlockSpec((1, tc), lambda j, i: (0, j))],
                  out_shape=[jax.ShapeDtypeStruct((t, dff), BF16), jax.ShapeDtypeStruct((t, dff), BF16),
                             jax.ShapeDtypeStruct((CONV_WIDTH, dff), F32), jax.ShapeDtypeStruct((1, dff), F32)],
                  args=(ab, ab, ab, ab, ab, dgg, dgg, cw, cb), sem=("parallel", "arbitrary"), comm=comm)


def _row_tile(rows, cols):
    for cand in (512, 256, 128, 64, 32, 16, 8):
        if rows % cand == 0 and cand * cols * 4 <= 2 * 1024 * 1024:
            return cand
    return rows


def _pair_add(grad8, recv4, cidx, *, name):
    _, rows, cols = grad8.shape
    tr = _row_tile(rows, cols)

    def body(c_ref, g_ref, r_ref, o_ref):
        o_ref[...] = (g_ref[...].astype(F32) + r_ref[...].astype(F32)).astype(BF16)

    blk = (None, tr, cols)
    return _pcall(body, name=name, grid=(4, rows // tr), prefetch=1,
                  in_specs=[pl.BlockSpec(blk, lambda k, i, c_ref: (2 * k + c_ref[0], i, 0)),
                            pl.BlockSpec(blk, lambda k, i, c_ref: (k, i, 0))],
                  out_specs=[pl.BlockSpec(blk, lambda k, i, c_ref: (k, i, 0))],
                  out_shape=[jax.ShapeDtypeStruct((4, rows, cols), BF16)], args=(cidx, grad8, recv4),
                  sem=("parallel", "parallel"))[0][0]


def _adamw_math(w, g, m, v):
    m = ADAM_B1 * m + (1.0 - ADAM_B1) * g
    v = ADAM_B2 * v + (1.0 - ADAM_B2) * (g * g)
    m_hat = m / (1.0 - ADAM_B1 ** ADAM_STEP)
    v_hat = v / (1.0 - ADAM_B2 ** ADAM_STEP)
    delta = -ADAM_LR * (m_hat / (jnp.sqrt(v_hat) + ADAM_EPS) + ADAM_WD * w)
    return delta, m, v


def _shard_adamw(part4, recv3, w, m, v, kidx, layer, bufs, *, name):
    depth, rows, cols = w.shape
    tr = _row_tile(rows, cols)

    def body(k_ref, p_ref, r0_ref, r1_ref, r2_ref, w_ref, m_ref, v_ref, *rest):
        g_out, d_out, m_out, v_out = rest[-4:]
        g = ((p_ref[...].astype(F32) + r0_ref[...].astype(F32)) + r1_ref[...].astype(F32)) + r2_ref[...].astype(F32)
        delta, mn, vn = _adamw_math(w_ref[...], g, m_ref[...], v_ref[...])
        g_out[...] = g
        d_out[...] = delta
        m_out[...] = mn
        v_out[...] = vn

    blk = (None, tr, cols)
    rspec = lambda j: pl.BlockSpec(blk, lambda i, k_ref: (j, i, 0))
    espec = pl.BlockSpec(blk, lambda i, k_ref: (layer, i, 0))
    in_specs = [pl.BlockSpec(blk, lambda i, k_ref: (k_ref[0], i, 0)), rspec(0), rspec(1), rspec(2), espec, espec, espec]
    args = [kidx, part4, recv3, recv3, recv3, w, m, v]
    aliases = {}
    if bufs is not None:
        in_specs += [ANY] * 4
        aliases = {8 + q: q for q in range(4)}
        args += list(bufs)
    return _pcall(body, name=name, grid=(rows // tr,), prefetch=1, in_specs=in_specs, out_specs=[espec] * 4,
                  out_shape=[jax.ShapeDtypeStruct((depth, rows, cols), F32)] * 4, args=args, sem=("parallel",),
                  aliases=aliases)[0]


def _small_adamw(gath, w, m, v, *, name):
    rows = w.shape[0]

    def body(g_ref, w_ref, m_ref, v_ref, g_out, d_out, m_out, v_out):
        g = g_ref[0]
        for dev in range(1, N_DEV):
            g = g + g_ref[dev]
        delta, mn, vn = _adamw_math(w_ref[...], g, m_ref[...], v_ref[...])
        g_out[...] = g
        d_out[...] = delta
        m_out[...] = mn
        v_out[...] = vn

    tr = _row_tile(rows, 128 * N_DEV)
    espec = pl.BlockSpec((tr, 128), lambda i: (i, 0))
    return _pcall(body, name=name, grid=(rows // tr,),
                  in_specs=[pl.BlockSpec((N_DEV, tr, 128), lambda i: (0, i, 0)), espec, espec, espec],
                  out_specs=[espec] * 4, out_shape=[jax.ShapeDtypeStruct((rows, 128), F32)] * 4,
                  args=(gath, w, m, v), sem=("parallel",))[0]


def _pack(arrs):
    flat = jnp.concatenate([a.reshape(-1) for a in arrs])
    total = flat.shape[0]
    rows = -(-total // (128 * 64)) * 64
    return jnp.pad(flat, (0, rows * 128 - total)).reshape(rows, 128)


def _unpack(packed, like):
    flat = packed.reshape(-1)
    out, off = [], 0
    for a in like:
        out.append(flat[off:off + a.size].reshape(a.shape))
        off += a.size
    return out


def kernel(x, g_mix, w_in, b_forget, g_sgu, w_spatial, b_spatial, w_branch_a, w_branch_b, w_out, g_ffn, w_up, conv_w, conv_b, w_down, g_final, loss_target, m_g_mix, m_w_in, m_b_forget, m_g_sgu, m_w_spatial, m_b_spatial, m_w_branch_a, m_w_branch_b, m_w_out, m_g_ffn, m_w_up, m_conv_w, m_conv_b, m_w_down, m_g_final, v_g_mix, v_w_in, v_b_forget, v_g_sgu, v_w_spatial, v_b_spatial, v_w_branch_a, v_w_branch_b, v_w_out, v_g_ffn, v_w_up, v_conv_w, v_conv_b, v_w_down, v_g_final):
    depth, d = g_mix.shape
    heads = b_forget.shape[1]
    fw = heads * HEAD_DIM
    sw = g_sgu.shape[1]
    dff = conv_b.shape[1]
    t = x.shape[1]
    nsi = w_in.shape[2]
    nsu = w_up.shape[2]
    o_f, o_u, o_g = 3 * fw, 3 * fw + heads, 3 * fw + heads + 2 * sw

    bpad = jnp.pad(b_forget, ((0, 0), (0, 128 - heads)))
    bst = jnp.pad(jnp.swapaxes(b_spatial, 1, 2), ((0, 0), (0, 0), (0, 128 - b_spatial.shape[1])))

    def in_shards(l):
        return [w_in[l].astype(BF16)]

    def small_shards(l):
        return [w_branch_a[l].astype(BF16), w_branch_b[l].astype(BF16), w_out[l].astype(BF16), conv_w[l]]

    def ffn_shards(l):
        return [w_up[l].astype(BF16), w_down[l].astype(BF16)]

    def unpack_in(bufs):
        (g_in,) = bufs
        f_in = jnp.moveaxis(g_in, 0, 1).reshape(d, N_DEV * nsi)
        return dict(wqkv=f_in[:, :o_f], wf=jnp.pad(f_in[:, o_f:o_u], ((0, 0), (0, 128 - heads))),
                    wuv=f_in[:, o_u:o_g], wg=f_in[:, o_g:])

    def unpack_small(bufs):
        g_wa, g_wb, g_wo, g_cw = bufs
        return dict(wa=g_wa, wb=g_wb, wo=g_wo.reshape(d, d), cw=jnp.moveaxis(g_cw, 0, 1).reshape(CONV_WIDTH, dff))

    mixer_w = [None] * depth
    ffn_w = [None] * depth
    first = _comm_only(_gather_first(in_shards(0)), name="gather_in_first_0")
    mixer_w[0] = unpack_in(_comm_only(_gather_second(first), name="gather_in_second_0"))
    small_first = None

    xs = x[0]
    saved = []
    for l in range(depth):
        n = lambda s: f"{s}_{l}"
        mw = mixer_w[l]
        h = _rms_fwd(xs, g_mix[l][None], name=n("rms_mix"))
        if small_first is None:
            qkv, small_first = _mm(h, mw["wqkv"], name=n("proj_qkv"), comm=_gather_first(small_shards(l)))
        else:
            qkv = _mm(h, mw["wqkv"], name=n("proj_qkv"))
        uv = _mm(h, mw["wuv"], name=n("proj_uv"))
        gates, small_bufs = _mm(h, mw["wg"], name=n("proj_gates"), comm=_gather_second(small_first))
        mw.update(unpack_small(small_bufs))
        flog = _mm(h, mw["wf"], out_dtype=F32, name=n("proj_forget"))
        c3 = _forget_fwd(flog, bpad[l][None], name=n("forget_fwd")).reshape(128, 1, t)
        (ya, ya32, lse), ffn_first = _fox_fwd(qkv, c3, heads, name=n("fox_fwd"), comm=_gather_first(ffn_shards(l)))
        yb = _sgu_fwd(uv, g_sgu[l][None], w_spatial[l], bst[l], name=n("sgu_fwd"))
        (merged, za, zb), _ = _merge_fwd(ya, yb, mw["wa"], mw["wb"], gates, name=n("merge_fwd"))
        x1, ffn_bufs = _mm(merged, mw["wo"], out_dtype=F32, res=xs, name=n("out_proj"), comm=_gather_second(ffn_first))
        g_wu, g_wd = ffn_bufs
        ffn_w[l] = dict(wu=g_wu, wd=g_wd.reshape(dff, d))
        h2 = _rms_fwd(x1, g_ffn[l][None], name=n("rms_ffn"))
        if l + 1 < depth:
            ab, in_first = _mm(h2, g_wu, b_cols=True, name=n("ffn_up"), comm=_gather_first(in_shards(l + 1)))
        else:
            ab = _mm(h2, g_wu, b_cols=True, name=n("ffn_up"))
        gg = _conv_fwd(ab, mw["cw"], conv_b[l][None], name=n("conv_fwd"))
        if l + 1 < depth:
            both = _Join([_gather_second(in_first), _gather_first(small_shards(l + 1))])
            x2, nxt = _mm(gg, ffn_w[l]["wd"], out_dtype=F32, res=x1, name=n("ffn_down"), comm=both)
            in_bufs, small_first = both.split(nxt)
            mixer_w[l + 1] = unpack_in(in_bufs)
        else:
            x2 = _mm(gg, ffn_w[l]["wd"], out_dtype=F32, res=x1, name=n("ffn_down"))
        saved.append((xs, h, qkv, uv, gates, flog, c3, ya, ya32, lse, yb, merged, za, zb, x1, h2, ab, gg))
        xs = x2

    loss_row, dx, dxb, d_g_final = _final_loss(xs, g_final[None], loss_target[0], name="final_loss")

    cidx = lax.axis_index("c").astype(jnp.int32).reshape(1)
    kidx = (2 * lax.axis_index("x") + lax.axis_index("y")).astype(jnp.int32).reshape(1)
    small_g = {k: [None] * depth for k in ("g_mix", "b_forget", "g_sgu", "w_spatial", "b_spatial", "g_ffn", "conv_b")}
    parts = {}
    recvs = {}
    mixer_names = ["w_in", "w_branch_a", "w_branch_b", "w_out"]
    ffn_names = ["w_up", "w_down", "conv_w"]
    for nm in mixer_names + ffn_names:
        parts[nm], recvs[nm] = [None] * depth, [None] * depth
    pending_mixer = None
    for l in reversed(range(depth)):
        n = lambda s: f"{s}_{l}"
        mw, fw_ = mixer_w[l], ffn_w[l]
        xs, h, qkv, uv, gates, flog, c3, ya, ya32, lse, yb, merged, za, zb, x1, h2, ab, gg = saved[l]
        g_wd = _mm(gg, dxb, ta=True, name=n("d_w_down"))
        if pending_mixer is not None:
            dgg, got = _mm(dxb, fw_["wd"], tb=True, name=n("d_gg"), comm=_scatter_second(pending_mixer[1:]))
            for nm, r in zip(mixer_names[1:], got):
                recvs[nm][l + 1] = r
            (d_a, d_b, d_cw, d_cb), got = _conv_bwd(ab, mw["cw"], conv_b[l][None], dgg, name=n("conv_bwd"),
                                                    comm=_scatter_second(pending_mixer[:1]))
            recvs[mixer_names[0]][l + 1] = got[0]
        else:
            dgg = _mm(dxb, fw_["wd"], tb=True, name=n("d_gg"))
            (d_a, d_b, d_cw, d_cb), _ = _conv_bwd(ab, mw["cw"], conv_b[l][None], dgg, name=n("conv_bwd"))
        small_g["conv_b"][l] = d_cb[0]
        g_wu = _mm(h2, d_a, ta=True, out_cols=(N_DEV, 0, nsu), name=n("d_w_up_a"))
        g_wu = _mm(h2, d_b, ta=True, out_cols=(N_DEV, N_DEV // 2, nsu), out_alias=g_wu, name=n("d_w_up_b"))
        ffn_grads = [g_wu, g_wd.reshape(N_DEV, dff // N_DEV, d),
                     jnp.moveaxis(d_cw.reshape(CONV_WIDTH, N_DEV, dff // N_DEV), 1, 0)]
        dh2 = _mm(d_a, fw_["wu"], tb=True, b_cols=True, out_dtype=F32, name=n("d_h2_a"))
        dh2, from_sib = _mm(d_b, fw_["wu"], tb=True, b_cols=True, b_off=N_DEV // 2, out_dtype=F32, res=dh2,
                            name=n("d_h2_b"), comm=_scatter_first(ffn_grads))
        ffn_parts = [_pair_add(g8, r4, cidx, name=n(f"pair_add_{nm}")) for nm, g8, r4 in zip(ffn_names, ffn_grads, from_sib)]
        for nm, p in zip(ffn_names, ffn_parts):
            parts[nm][l] = p
        dx, dxb, dg = _rms_bwd(x1, g_ffn[l][None], dh2, dx, name=n("rms_ffn_bwd"))
        small_g["g_ffn"][l] = dg[0]
        g_wo = _mm(merged, dxb, ta=True, name=n("d_w_out"))
        dmg = _mm(dxb, mw["wo"], tb=True, name=n("d_merged"))
        dza, dzb, dgates = _merge_bwd(dmg, gates, za, zb, name=n("merge_bwd"))
        g_wa = _mm(ya, dza, ta=True, out_cols=(N_DEV, 0, d // N_DEV), name=n("d_w_a"))
        g_wb = _mm(yb, dzb, ta=True, out_cols=(N_DEV, 0, d // N_DEV), name=n("d_w_b"))
        dya = _mm(dza, mw["wa"], tb=True, b_cols=True, name=n("d_ya"))
        dyb = _mm(dzb, mw["wb"], tb=True, b_cols=True, name=n("d_yb"))
        duv, d_ws, d_bst, d_gs = _sgu_bwd(uv, g_sgu[l][None], w_spatial[l], bst[l], dyb, name=n("sgu_bwd"))
        small_g["w_spatial"][l], small_g["g_sgu"][l] = d_ws, d_gs[0]
        small_g["b_spatial"][l] = d_bst[:, :b_spatial.shape[1]].T
        (dq, dk, dv, dc3), got = _fox_bwd(qkv, c3, ya32, dya, lse, heads, name=n("fox_bwd"),
                                          comm=_scatter_second(ffn_parts))
        for nm, r in zip(ffn_names, got):
            recvs[nm][l] = r
        dct = jnp.pad(dc3.reshape(heads, t), ((0, 128 - heads), (0, 0)))
        dflog, d_bf = _forget_bwd(flog, bpad[l][None], dct, name=n("forget_bwd"))
        small_g["b_forget"][l] = d_bf[0, :heads]
        gw = [_mm(h, dq, ta=True, name=n("d_w_q")), _mm(h, dk, ta=True, name=n("d_w_k")),
              _mm(h, dv, ta=True, name=n("d_w_v")), _mm(h, dflog, ta=True, name=n("d_w_forget"))[:, :heads],
              _mm(h, duv, ta=True, name=n("d_w_uv")), _mm(h, dgates, ta=True, name=n("d_w_gates"))]
        g_in = jnp.moveaxis(jnp.concatenate(gw, axis=1).reshape(d, N_DEV, nsi), 1, 0)
        mixer_grads = [g_in, g_wa, g_wb, g_wo.reshape(N_DEV, d // N_DEV, d)]
        dh = _mm(dflog, mw["wf"], tb=True, out_dtype=F32, name=n("d_h_forget"))
        dh = _mm(dq, mw["wqkv"][:, :fw], tb=True, out_dtype=F32, res=dh, name=n("d_h_q"))
        dh = _mm(dk, mw["wqkv"][:, fw:2 * fw], tb=True, out_dtype=F32, res=dh, name=n("d_h_k"))
        dh = _mm(dv, mw["wqkv"][:, 2 * fw:], tb=True, out_dtype=F32, res=dh, name=n("d_h_v"))
        dh = _mm(duv, mw["wuv"], tb=True, out_dtype=F32, res=dh, name=n("d_h_uv"))
        dh, from_sib = _mm(dgates, mw["wg"], tb=True, out_dtype=F32, res=dh, name=n("d_h_gates"),
                           comm=_scatter_first(mixer_grads))
        pending_mixer = [_pair_add(g8, r4, cidx, name=n(f"pair_add_{nm}"))
                         for nm, g8, r4 in zip(mixer_names, mixer_grads, from_sib)]
        for nm, p in zip(mixer_names, pending_mixer):
            parts[nm][l] = p
        dx, dxb, dg = _rms_bwd(xs, g_mix[l][None], dh, dx, name=n("rms_mix_bwd"))
        small_g["g_mix"][l] = dg[0]
    grad_x = dx[None]
    for nm, r in zip(mixer_names, _comm_only(_scatter_second(pending_mixer), name="scatter_mixer_second_0")):
        recvs[nm][0] = r

    weights = {"w_in": (w_in, m_w_in, v_w_in), "w_branch_a": (w_branch_a, m_w_branch_a, v_w_branch_a),
               "w_branch_b": (w_branch_b, m_w_branch_b, v_w_branch_b), "w_out": (w_out, m_w_out, v_w_out),
               "w_up": (w_up, m_w_up, v_w_up), "conv_w": (conv_w, m_conv_w, v_conv_w),
               "w_down": (w_down, m_w_down, v_w_down)}
    res = {}
    for nm, (w, m, v) in weights.items():
        bufs = None
        for l in range(depth):
            bufs = _shard_adamw(parts[nm][l], recvs[nm][l], w, m, v, kidx, l, bufs, name=f"adamw_{nm}_{l}")
        res[nm] = bufs

    small = ["g_mix", "b_forget", "g_sgu", "w_spatial", "b_spatial", "g_ffn", "conv_b", "g_final", "loss"]
    zero = jnp.zeros((1,), F32)
    small_w = [g_mix, b_forget, g_sgu, w_spatial, b_spatial, g_ffn, conv_b, g_final, zero]
    small_m = [m_g_mix, m_b_forget, m_g_sgu, m_w_spatial, m_b_spatial, m_g_ffn, m_conv_b, m_g_final, zero]
    small_v = [v_g_mix, v_b_forget, v_g_sgu, v_w_spatial, v_b_spatial, v_g_ffn, v_conv_b, v_g_final, zero]
    small_grads = [jnp.stack(small_g[nm]) for nm in small[:-2]] + [d_g_final[0], loss_row[0, 0:1]]
    first = _comm_only(_gather_first([_pack(small_grads)]), name="gather_small_first")
    (gath,) = _comm_only(_gather_second(first), name="gather_small_second")
    outs = _small_adamw(gath, _pack(small_w), _pack(small_m), _pack(small_v), name="adamw_replicated")
    for nm, vals in zip(small, zip(*[_unpack(o, small_w) for o in outs])):
        res[nm] = list(vals)
    loss = res["loss"][0][0]

    order = ["g_mix", "w_in", "b_forget", "g_sgu", "w_spatial", "b_spatial", "w_branch_a", "w_branch_b", "w_out",
             "g_ffn", "w_up", "conv_w", "conv_b", "w_down", "g_final"]
    return (loss, grad_x, *[res[nm][0] for nm in order], *[res[nm][1] for nm in order],
            *[res[nm][2] for nm in order], *[res[nm][3] for nm in order])
```

```python
import numpy as np

import jax
import jax.numpy as jnp
from jax import lax
from jax.experimental import pallas as pl
from jax.experimental.pallas import tpu as pltpu

F32 = jnp.float32
BF16 = jnp.bfloat16

RMS_EPS = 1e-6
HEAD_DIM = 128
CONV_WIDTH = 3
ADAM_LR = 0.001
ADAM_B1 = 0.9
ADAM_B2 = 0.999
ADAM_EPS = 1e-08
ADAM_WD = 0.01
ADAM_STEP = 10
N_DEV = 8
V7X_VMEM_LIMIT = 58 * 1024 * 1024
MM_VMEM_BUDGET = 46 * 1024 * 1024
NEG = -1e30
ANY = pl.BlockSpec(memory_space=pl.ANY)
MESH = pl.DeviceIdType.MESH


def _tile(dim, pref):
    for t in (2048, 1024, 512, 256, 128):
        if t <= pref and dim % t == 0:
            return t
    return dim


def _gelu(x):
    t = jnp.tanh(0.7978845608028654 * (x + 0.044715 * (x * x * x)))
    return x * (0.5 * (1.0 + t))


def _gelu_and_grad(x):
    x2 = x * x
    t = jnp.tanh(0.7978845608028654 * (x + 0.044715 * (x2 * x)))
    cdf = 0.5 * (1.0 + t)
    dt = (1.0 - t * t) * (0.7978845608028654 * (1.0 + 3.0 * 0.044715 * x2))
    return x * cdf, cdf + 0.5 * x * dt


def _sigmoid(x):
    return 1.0 / (1.0 + jnp.exp(-x))


class _Comm:
    def __init__(self, srcs, new, alias, n_copies, emit):
        self.srcs = list(srcs)
        self.new = list(new)
        self.alias = list(alias)
        self.n_copies = n_copies
        self.emit = emit

    def split(self, couts):
        return [couts]


class _Join(_Comm):
    def __init__(self, comms):
        self.comms = comms
        srcs = [s for cm in comms for s in cm.srcs]
        new = [s for cm in comms for s in cm.new]
        alias = [s for cm in comms for s in cm.alias]

        def emit(src_refs, new_refs, alias_refs, sems):
            copies, s0, n0, a0, k0 = [], 0, 0, 0, sems[2]
            for cm in comms:
                copies += cm.emit(src_refs[s0:s0 + len(cm.srcs)], new_refs[n0:n0 + len(cm.new)],
                                  alias_refs[a0:a0 + len(cm.alias)], (sems[0], sems[1], k0))
                s0, n0, a0, k0 = s0 + len(cm.srcs), n0 + len(cm.new), a0 + len(cm.alias), k0 + cm.n_copies
            return copies

        super().__init__(srcs, new, alias, sum(cm.n_copies for cm in comms), emit)

    def split(self, couts):
        n_new = len(self.new)
        out, n0, a0 = [], 0, 0
        for cm in self.comms:
            out.append(couts[n0:n0 + len(cm.new)] + couts[n_new + a0:n_new + a0 + len(cm.alias)])
            n0, a0 = n0 + len(cm.new), a0 + len(cm.alias)
        return out


def _place():
    x, y, c = lax.axis_index("x"), lax.axis_index("y"), lax.axis_index("c")
    chips = [(1 - x, y), (x, 1 - y), (1 - x, 1 - y)]
    return x, y, c, chips


def _remote(src, dst, sems, k, to):
    return pltpu.make_async_remote_copy(src_ref=src, dst_ref=dst, send_sem=sems[0].at[sems[2] + k],
                                        recv_sem=sems[1].at[sems[2] + k], device_id=to, device_id_type=MESH)


def _gather_first(shards):
    n = len(shards)

    def emit(srcs, new, alias, sems):
        x, y, c, chips = _place()
        me = 4 * x + 2 * y + c
        copies = []
        for a in range(n):
            copies.append(pltpu.make_async_copy(srcs[a], new[a].at[me], sems[0].at[sems[2] + 5 * a + 4]))
            copies.append(_remote(srcs[a], new[a].at[me], sems, 5 * a, (x, y, 1 - c)))
            for j, chip in enumerate(chips):
                copies.append(_remote(srcs[a], new[a].at[me], sems, 5 * a + 1 + j, (*chip, c)))
        return copies

    new = [jax.ShapeDtypeStruct((N_DEV,) + s.shape, s.dtype) for s in shards]
    return _Comm(shards, new, [], 5 * n, emit)


def _gather_second(bufs):
    n = len(bufs)

    def emit(srcs, new, alias, sems):
        x, y, c, chips = _place()
        copies = []
        for a in range(n):
            for j, chip in enumerate(chips):
                blk = alias[a].at[4 * chip[0] + 2 * chip[1] + c]
                copies.append(_remote(blk, blk, sems, 3 * a + j, (x, y, 1 - c)))
        return copies

    return _Comm([], [], bufs, 3 * n, emit)


def _scatter_first(grads):
    n = len(grads)

    def emit(srcs, new, alias, sems):
        x, y, c, _ = _place()
        return [_remote(srcs[a].at[2 * k + 1 - c], new[a].at[k], sems, 4 * a + k, (x, y, 1 - c))
                for a in range(n) for k in range(4)]

    new = [jax.ShapeDtypeStruct((4,) + g.shape[1:], g.dtype) for g in grads]
    return _Comm(grads, new, [], 4 * n, emit)


def _scatter_second(parts):
    n = len(parts)

    def emit(srcs, new, alias, sems):
        x, y, c, chips = _place()
        return [_remote(srcs[a].at[2 * chip[0] + chip[1]], new[a].at[j], sems, 3 * a + j, (*chip, c))
                for a in range(n) for j, chip in enumerate(chips)]

    new = [jax.ShapeDtypeStruct((3,) + p.shape[1:], p.dtype) for p in parts]
    return _Comm(parts, new, [], 3 * n, emit)


def _pcall(body, *, name, grid, in_specs, out_specs, out_shape, args, sem, scratch=(), aliases=None, comm=None,
           prefetch=0):
    in_specs, out_specs, out_shape, scratch = list(in_specs), list(out_specs), list(out_shape), list(scratch)
    aliases = dict(aliases or {})
    n_in, n_out, n_scr = len(in_specs), len(out_shape), len(scratch)

    def make(body_fn, ins, outs, shapes, scr, sem_):
        params = pltpu.CompilerParams(dimension_semantics=sem_, vmem_limit_bytes=V7X_VMEM_LIMIT)
        if prefetch:
            spec = pltpu.PrefetchScalarGridSpec(num_scalar_prefetch=prefetch, grid=grid, in_specs=ins,
                                                out_specs=outs, scratch_shapes=scr)
            return pl.pallas_call(body_fn, name=name, grid_spec=spec, out_shape=shapes,
                                  input_output_aliases=aliases, compiler_params=params)
        return pl.pallas_call(body_fn, name=name, grid=grid, in_specs=ins, out_specs=outs, out_shape=shapes,
                              scratch_shapes=scr, input_output_aliases=aliases, compiler_params=params)

    if comm is None:
        return list(make(body, in_specs, out_specs, out_shape, scratch, sem)(*args)), []

    n_src, n_new, n_al = len(comm.srcs), len(comm.new), len(comm.alias)
    for a in range(n_al):
        aliases[prefetch + n_in + n_src + a] = n_out + n_new + a

    def wrapped(*refs):
        pre, refs = refs[:prefetch], refs[prefetch:]
        ins = refs[:n_in]
        src_refs = refs[n_in:n_in + n_src]
        o0 = n_in + n_src + n_al
        outs = refs[o0:o0 + n_out]
        new_refs = refs[o0 + n_out:o0 + n_out + n_new]
        alias_refs = refs[o0 + n_out + n_new:o0 + n_out + n_new + n_al]
        s0 = o0 + n_out + n_new + n_al
        scr = refs[s0:s0 + n_scr]
        send_sems, recv_sems = refs[s0 + n_scr], refs[s0 + n_scr + 1]
        first = pl.program_id(0) == 0
        last = pl.program_id(0) == grid[0] - 1
        for dim in range(1, len(grid)):
            first = first & (pl.program_id(dim) == 0)
            last = last & (pl.program_id(dim) == grid[dim] - 1)

        @pl.when(first)
        def _():
            for cp in comm.emit(src_refs, new_refs, alias_refs, (send_sems, recv_sems, 0)):
                cp.start()

        body(*pre, *ins, *outs, *scr)

        @pl.when(last)
        def _():
            for cp in comm.emit(src_refs, new_refs, alias_refs, (send_sems, recv_sems, 0)):
                cp.wait()

    call = make(wrapped, in_specs + [ANY] * (n_src + n_al), out_specs + [ANY] * (n_new + n_al),
                out_shape + comm.new + [jax.ShapeDtypeStruct(b.shape, b.dtype) for b in comm.alias],
                scratch + [pltpu.SemaphoreType.DMA((comm.n_copies,)), pltpu.SemaphoreType.DMA((comm.n_copies,))],
                ("arbitrary",) * len(grid))
    res = list(call(*args, *comm.srcs, *comm.alias))
    return res[:n_out], res[n_out:]


def _comm_only(comm, *, name):
    def body(o_ref):
        o_ref[...] = jnp.zeros_like(o_ref)

    _, couts = _pcall(body, name=name, grid=(1,), in_specs=[], out_specs=[pl.BlockSpec((8, 128), lambda i: (0, 0))],
                      out_shape=[jax.ShapeDtypeStruct((8, 128), F32)], args=(), sem=("arbitrary",), comm=comm)
    return couts


def _divisor_tiles(dim, cap):
    tiles = [t for t in range(128, min(dim, cap) + 1, 128) if dim % t == 0]
    return sorted(tiles, reverse=True) or [dim]


def _mm_tiles(m, n, k, obytes, has_res, tn_fixed=None, tk_fixed=None):
    tms = _divisor_tiles(m, 1408)
    tns = [tn_fixed] if tn_fixed else _divisor_tiles(n, 1408)
    tks = [tk_fixed] if tk_fixed else [k] + [tt for tt in _divisor_tiles(k, 2048) if tt != k]
    best, best_score = None, None
    for tk in tks:
        for tm in tms:
            for tn in tns:
                nk = k // tk
                use = 4 * tm * tk + 4 * tk * tn + 2 * tm * tn * obytes
                use += (8 * tm * tn if has_res else 0) + (4 * tm * tn if nk > 1 else 0)
                score = (nk == 1, min(tm, 1024), tn, tm, tk)
                if use <= MM_VMEM_BUDGET and (best is None or score > best_score):
                    best, best_score = (tm, tn, tk), score
    assert best is not None, (m, n, k)
    return best


def _mm(a, b, *, ta=False, tb=False, out_dtype=BF16, res=None, name, b_cols=False, b_off=0,
        out_cols=None, out_alias=None, comm=None):
    m, k = (a.shape[1], a.shape[0]) if ta else a.shape
    obytes = jnp.dtype(out_dtype).itemsize
    if b_cols and not tb:
        ns = b.shape[2]
        assert b.shape[1] == k
        n = b.shape[0] * ns
        tm, tn, tk = _mm_tiles(m, n, k, obytes, res is not None, tn_fixed=ns)
        b_spec = pl.BlockSpec((None, tk, ns), lambda i, j, kk: (j, kk, 0))
    elif b_cols:
        ns = b.shape[2]
        assert k % ns == 0
        n = b.shape[1]
        tm, tn, tk = _mm_tiles(m, n, k, obytes, res is not None, tk_fixed=ns)
        b_spec = pl.BlockSpec((None, tn, ns), lambda i, j, kk: (b_off + kk, j, 0))
    else:
        n = b.shape[0] if tb else b.shape[1]
        assert (b.shape[1] >= b_off + k) if tb else (b.shape[0] == k and b_off == 0)
        tm, tn, tk = _mm_tiles(m, n, k, obytes, res is not None, tn_fixed=out_cols[2] if out_cols is not None else None)
        assert b_off % tk == 0
        k0 = b_off // tk
        b_spec = pl.BlockSpec((tn, tk), lambda i, j, kk: (j, k0 + kk)) if tb else pl.BlockSpec((tk, tn), lambda i, j, kk: (kk, j))
    nk = k // tk
    dn = (((0,) if ta else (1,), (1,) if tb else (0,)), ((), ()))
    n_extra = (res is not None) + (out_alias is not None)

    def body(*refs):
        a_ref, b_ref = refs[0], refs[1]
        r_ref = refs[2] if res is not None else None
        o_ref = refs[2 + n_extra]
        part = lax.dot_general(a_ref[...], b_ref[...], dn, preferred_element_type=F32)

        def finish(r):
            if r_ref is not None:
                r = r + r_ref[...]
            o_ref[...] = r.astype(out_dtype)

        if nk == 1:
            finish(part)
        else:
            acc_ref = refs[3 + n_extra]
            kk = pl.program_id(2)

            @pl.when(kk == 0)
            def _():
                acc_ref[...] = part

            @pl.when(kk > 0)
            def _():
                acc_ref[...] += part

            @pl.when(kk == nk - 1)
            def _():
                finish(acc_ref[...])

    a_spec = pl.BlockSpec((tk, tm), lambda i, j, kk: (kk, i)) if ta else pl.BlockSpec((tm, tk), lambda i, j, kk: (i, kk))
    in_specs, args, aliases = [a_spec, b_spec], [a, b], {}
    if res is not None:
        in_specs.append(pl.BlockSpec((tm, tn), lambda i, j, kk: (i, j)))
        args.append(res)
    if out_cols is not None:
        s_total, o_off, ns_o = out_cols
        assert tn == ns_o and n % ns_o == 0
        o_spec = pl.BlockSpec((None, tm, tn), lambda i, j, kk: (o_off + j, i, 0))
        o_shape = jax.ShapeDtypeStruct((s_total, m, tn), out_dtype)
        if out_alias is not None:
            in_specs.append(ANY)
            args.append(out_alias)
            aliases[len(args) - 1] = 0
    else:
        o_spec = pl.BlockSpec((tm, tn), lambda i, j, kk: (i, j))
        o_shape = jax.ShapeDtypeStruct((m, n), out_dtype)
    outs, couts = _pcall(body, name=name, grid=(m // tm, n // tn, nk), in_specs=in_specs, out_specs=[o_spec],
                         out_shape=[o_shape], args=args, sem=("parallel", "parallel", "arbitrary"),
                         scratch=[pltpu.VMEM((tm, tn), F32)] if nk > 1 else [], aliases=aliases, comm=comm)
    return (outs[0], couts) if comm is not None else outs[0]


def _rms_fwd(x, g, *, name):
    t, d = x.shape
    tr = _tile(t, 256)

    def body(x_ref, g_ref, h_ref):
        xf = x_ref[...]
        inv = lax.rsqrt(jnp.mean(xf * xf, axis=-1, keepdims=True) + RMS_EPS)
        h_ref[...] = ((xf * inv) * g_ref[...]).astype(BF16)

    row = pl.BlockSpec((tr, d), lambda i: (i, 0))
    return _pcall(body, name=name, grid=(t // tr,), in_specs=[row, pl.BlockSpec((1, d), lambda i: (0, 0))],
                  out_specs=[row], out_shape=[jax.ShapeDtypeStruct((t, d), BF16)], args=(x, g),
                  sem=("parallel",))[0][0]


def _rms_bwd(x, g, dh, dres, *, name):
    t, d = x.shape
    tr = _tile(t, 256)

    def body(x_ref, g_ref, dh_ref, dres_ref, dx_ref, dxb_ref, dg_ref):
        xf = x_ref[...]
        inv = lax.rsqrt(jnp.mean(xf * xf, axis=-1, keepdims=True) + RMS_EPS)
        xn = xf * inv
        dh_f = dh_ref[...].astype(F32)
        dxn = dh_f * g_ref[...]
        dx = dres_ref[...] + inv * (dxn - xn * jnp.mean(dxn * xn, axis=-1, keepdims=True))
        dx_ref[...] = dx
        dxb_ref[...] = dx.astype(BF16)
        part = jnp.sum(dh_f * xn, axis=0, keepdims=True)

        @pl.when(pl.program_id(0) == 0)
        def _():
            dg_ref[...] = part

        @pl.when(pl.program_id(0) > 0)
        def _():
            dg_ref[...] += part

    row = pl.BlockSpec((tr, d), lambda i: (i, 0))
    vec = pl.BlockSpec((1, d), lambda i: (0, 0))
    return _pcall(body, name=name, grid=(t // tr,), in_specs=[row, vec, row, row], out_specs=[row, row, vec],
                  out_shape=[jax.ShapeDtypeStruct((t, d), F32), jax.ShapeDtypeStruct((t, d), BF16),
                             jax.ShapeDtypeStruct((1, d), F32)], args=(x, g, dh, dres), sem=("arbitrary",))[0]


def _final_loss(x, g, target, *, name):
    t, d = x.shape
    tr = _tile(t, 256)

    def body(x_ref, g_ref, tg_ref, loss_ref, dx_ref, dxb_ref, dg_ref):
        xf = x_ref[...]
        gv = g_ref[...]
        inv = lax.rsqrt(jnp.mean(xf * xf, axis=-1, keepdims=True) + RMS_EPS)
        xn = xf * inv
        err = xn * gv - tg_ref[...]
        lpart = 0.5 * jnp.sum(jnp.mean(err * err, axis=-1, keepdims=True), axis=0, keepdims=True)
        dy = err * (1.0 / d)
        dxn = dy * gv
        dx = inv * (dxn - xn * jnp.mean(dxn * xn, axis=-1, keepdims=True))
        dx_ref[...] = dx
        dxb_ref[...] = dx.astype(BF16)
        gpart = jnp.sum(dy * xn, axis=0, keepdims=True)
        lrow = jnp.broadcast_to(lpart, (1, 128))

        @pl.when(pl.program_id(0) == 0)
        def _():
            dg_ref[...] = gpart
            loss_ref[...] = lrow

        @pl.when(pl.program_id(0) > 0)
        def _():
            dg_ref[...] += gpart
            loss_ref[...] += lrow

    row = pl.BlockSpec((tr, d), lambda i: (i, 0))
    vec = pl.BlockSpec((1, d), lambda i: (0, 0))
    lspec = pl.BlockSpec((1, 128), lambda i: (0, 0))
    return _pcall(body, name=name, grid=(t // tr,), in_specs=[row, vec, row], out_specs=[lspec, row, row, vec],
                  out_shape=[jax.ShapeDtypeStruct((1, 128), F32), jax.ShapeDtypeStruct((t, d), F32),
                             jax.ShapeDtypeStruct((t, d), BF16), jax.ShapeDtypeStruct((1, d), F32)],
                  args=(x, g, target), sem=("arbitrary",))[0]


def _tri_ones(n, upper):
    r = lax.broadcasted_iota(jnp.int32, (n, n), 0)
    c = lax.broadcasted_iota(jnp.int32, (n, n), 1)
    return jnp.where((r <= c) if upper else (r >= c), 1.0, 0.0).astype(F32)


def _forget_fwd(flog, bpad, *, name):
    t = flog.shape[0]
    tb = _tile(t, 512)

    def body(f_ref, b_ref, c_ref, carry):
        z = f_ref[...] + b_ref[...]
        lf = jnp.minimum(z, 0.0) - jnp.log(1.0 + jnp.exp(-jnp.abs(z)))
        lft = lf.T
        tri = _tri_ones(tb, upper=True)

        @pl.when(pl.program_id(0) == 0)
        def _():
            carry[...] = jnp.zeros_like(carry)

        cs = jnp.dot(lft, tri, preferred_element_type=F32, precision=lax.Precision.HIGHEST) + carry[:, 0:1]
        c_ref[...] = cs
        carry[...] = jnp.broadcast_to(cs[:, tb - 1:tb], carry.shape)

    return _pcall(body, name=name, grid=(t // tb,),
                  in_specs=[pl.BlockSpec((tb, 128), lambda i: (i, 0)), pl.BlockSpec((1, 128), lambda i: (0, 0))],
                  out_specs=[pl.BlockSpec((128, tb), lambda i: (0, i))],
                  out_shape=[jax.ShapeDtypeStruct((128, t), F32)], args=(flog, bpad), sem=("arbitrary",),
                  scratch=[pltpu.VMEM((128, 128), F32)])[0][0]


def _forget_bwd(flog, bpad, dct, *, name):
    t = flog.shape[0]
    tb = _tile(t, 512)
    nb = t // tb

    def body(f_ref, b_ref, dc_ref, df_ref, db_ref, carry):
        i = pl.program_id(0)

        @pl.when(i == 0)
        def _():
            carry[...] = jnp.zeros_like(carry)

        tri = _tri_ones(tb, upper=False)
        dl = jnp.dot(dc_ref[...], tri, preferred_element_type=F32, precision=lax.Precision.HIGHEST) + carry[:, 0:1]
        carry[...] = jnp.broadcast_to(dl[:, 0:1], carry.shape)
        z = f_ref[...] + b_ref[...]
        df = dl.T * _sigmoid(-z)
        df_ref[...] = df.astype(BF16)
        part = jnp.sum(df, axis=0, keepdims=True)

        @pl.when(i == 0)
        def _():
            db_ref[...] = part

        @pl.when(i > 0)
        def _():
            db_ref[...] += part

    rev = lambda i: (nb - 1 - i, 0)
    return _pcall(body, name=name, grid=(nb,),
                  in_specs=[pl.BlockSpec((tb, 128), rev), pl.BlockSpec((1, 128), lambda i: (0, 0)),
                            pl.BlockSpec((128, tb), lambda i: (0, nb - 1 - i))],
                  out_specs=[pl.BlockSpec((tb, 128), rev), pl.BlockSpec((1, 128), lambda i: (0, 0))],
                  out_shape=[jax.ShapeDtypeStruct((t, 128), BF16), jax.ShapeDtypeStruct((1, 128), F32)],
                  args=(flog, bpad, dct), sem=("arbitrary",), scratch=[pltpu.VMEM((128, 128), F32)])[0]


def _causal_pairs(nq, k_major):
    if k_major:
        pairs = [(i, j) for j in range(nq) for i in range(j, nq)]
    else:
        pairs = [(i, j) for i in range(nq) for j in range(i + 1)]
    return (jnp.asarray(np.array([p[0] for p in pairs], np.int32)),
            jnp.asarray(np.array([p[1] for p in pairs], np.int32)), len(pairs))


def _logits(q, k, cq, ck, scale, masked):
    s = lax.dot_general(q, k, (((1,), (1,)), ((), ())), preferred_element_type=F32)
    s = s * scale + (cq[:, 0:1] - ck)
    if masked:
        row = lax.broadcasted_iota(jnp.int32, s.shape, 0)
        col = lax.broadcasted_iota(jnp.int32, s.shape, 1)
        s = jnp.where(col <= row, s, NEG)
    return s


def _head_group(heads):
    return 4 if heads % 4 == 0 else (2 if heads % 2 == 0 else 1)


def _fox_fwd(qkv, c3, heads, *, name, comm=None):
    t = qkv.shape[0]
    tq = _tile(t, 512)
    nq = t // tq
    hb = _head_group(heads)
    ng = heads // hb
    scale = HEAD_DIM ** -0.5
    i_tab, j_tab, npairs = _causal_pairs(nq, k_major=False)

    def body(it_ref, jt_ref, q_ref, k_ref, v_ref, cq_ref, ck_ref, o_ref, o32_ref, lse_ref, m_s, l_s, acc_s):
        p_id = pl.program_id(1)
        i, j = it_ref[p_id], jt_ref[p_id]

        @pl.when(j == 0)
        def _():
            m_s[...] = jnp.full_like(m_s, NEG)
            l_s[...] = jnp.zeros_like(l_s)
            acc_s[...] = jnp.zeros_like(acc_s)

        def update(masked):
            for hh in range(hb):
                ls = slice(hh * 128, (hh + 1) * 128)
                s = _logits(q_ref[:, ls], k_ref[:, ls], cq_ref[hh], ck_ref[hh], scale, masked)
                m_prev = m_s[hh, :, 0:1]
                m_new = jnp.maximum(m_prev, jnp.max(s, axis=1, keepdims=True))
                alpha = jnp.exp(m_prev - m_new)
                p = jnp.exp(s - m_new)
                l_s[hh, :, 0:1] = alpha * l_s[hh, :, 0:1] + jnp.sum(p, axis=1, keepdims=True)
                p_hi = p.astype(BF16)
                p_lo = (p - p_hi.astype(F32)).astype(BF16)
                vb = v_ref[:, ls]
                pv = jnp.dot(p_hi, vb, preferred_element_type=F32) + jnp.dot(p_lo, vb, preferred_element_type=F32)
                acc_s[hh] = alpha * acc_s[hh] + pv
                m_s[hh, :, 0:1] = m_new

        @pl.when(j < i)
        def _():
            update(False)

        @pl.when(j == i)
        def _():
            update(True)
            for hh in range(hb):
                ls = slice(hh * 128, (hh + 1) * 128)
                l = l_s[hh, :, 0:1]
                o = acc_s[hh] / l
                o_ref[:, ls] = o.astype(BF16)
                o32_ref[:, ls] = o
                lse_ref[hh] = jnp.broadcast_to(m_s[hh, :, 0:1] + jnp.log(l), (tq, 128))

    w = hb * 128
    qb = lambda g, p, it, jt: (it[p], g)
    outs, couts = _pcall(
        body, name=name, grid=(ng, npairs), prefetch=2,
        in_specs=[pl.BlockSpec((tq, w), qb),
                  pl.BlockSpec((tq, w), lambda g, p, it, jt: (jt[p], ng + g)),
                  pl.BlockSpec((tq, w), lambda g, p, it, jt: (jt[p], 2 * ng + g)),
                  pl.BlockSpec((hb, 1, tq), lambda g, p, it, jt: (g, 0, it[p])),
                  pl.BlockSpec((hb, 1, tq), lambda g, p, it, jt: (g, 0, jt[p]))],
        out_specs=[pl.BlockSpec((tq, w), qb), pl.BlockSpec((tq, w), qb),
                   pl.BlockSpec((hb, tq, 128), lambda g, p, it, jt: (g, it[p], 0))],
        out_shape=[jax.ShapeDtypeStruct((t, heads * 128), BF16), jax.ShapeDtypeStruct((t, heads * 128), F32),
                   jax.ShapeDtypeStruct((heads, t, 128), F32)],
        args=(i_tab, j_tab, qkv, qkv, qkv, c3, c3), sem=("parallel", "arbitrary"),
        scratch=[pltpu.VMEM((hb, tq, 128), F32), pltpu.VMEM((hb, tq, 128), F32), pltpu.VMEM((hb, tq, 128), F32)],
        comm=comm)
    return outs, couts


def _fox_bwd(qkv, c3, o, do, lse, heads, *, name, comm=None):
    t = qkv.shape[0]
    tq = _tile(t, 512)
    nq = t // tq
    hb = _head_group(heads)
    ng = heads // hb
    scale = HEAD_DIM ** -0.5
    i_tab, j_tab, npairs = _causal_pairs(nq, k_major=True)

    def body(it_ref, jt_ref, q_ref, k_ref, v_ref, o_ref, do_ref, lse_ref, cq_ref, ck_ref,
             dq_ref, dk_ref, dv_ref, dc_ref, dq_s, dk_s, dv_s, dc_s):
        p_id = pl.program_id(1)
        i, j = it_ref[p_id], jt_ref[p_id]

        @pl.when(p_id == 0)
        def _():
            dq_s[...] = jnp.zeros_like(dq_s)

        @pl.when(i == j)
        def _():
            dk_s[...] = jnp.zeros_like(dk_s)
            dv_s[...] = jnp.zeros_like(dv_s)
            dc_s[...] = jnp.zeros_like(dc_s)

        def update(masked):
            r0 = pl.multiple_of(i * tq, tq)
            for hh in range(hb):
                ls = slice(hh * 128, (hh + 1) * 128)
                q, k, v, dob = q_ref[:, ls], k_ref[:, ls], v_ref[:, ls], do_ref[:, ls]
                s = _logits(q, k, cq_ref[hh], ck_ref[hh], scale, masked)
                p = jnp.exp(s - lse_ref[hh, :, 0:1])
                delta = jnp.sum(dob.astype(F32) * o_ref[:, ls], axis=1, keepdims=True)
                dp = lax.dot_general(dob, v, (((1,), (1,)), ((), ())), preferred_element_type=F32)
                ds = p * (dp - delta)
                pb, dsb = p.astype(BF16), ds.astype(BF16)
                dv_s[hh] += lax.dot_general(pb, dob, (((0,), (0,)), ((), ())), preferred_element_type=F32)
                dk_s[hh] += lax.dot_general(dsb, q, (((0,), (0,)), ((), ())), preferred_element_type=F32)
                dq_s[hh, pl.ds(r0, tq), :] += jnp.dot(dsb, k, preferred_element_type=F32) * scale
                dc_s[hh] -= jnp.sum(ds, axis=0, keepdims=True)

        @pl.when(i > j)
        def _():
            update(False)

        @pl.when(i == j)
        def _():
            update(True)

        @pl.when(i == nq - 1)
        def _():
            for hh in range(hb):
                ls = slice(hh * 128, (hh + 1) * 128)
                dk_ref[:, ls] = (dk_s[hh] * scale).astype(BF16)
                dv_ref[:, ls] = dv_s[hh].astype(BF16)
            dc_ref[...] = dc_s[...]

        @pl.when(p_id == npairs - 1)
        def _():
            for hh in range(hb):
                dq_ref[:, hh * 128:(hh + 1) * 128] = dq_s[hh].astype(BF16)

    w = hb * 128
    qb = lambda g, p, it, jt: (it[p], g)
    kb = lambda g, p, it, jt: (jt[p], g)
    outs, couts = _pcall(
        body, name=name, grid=(ng, npairs), prefetch=2,
        in_specs=[pl.BlockSpec((tq, w), qb),
                  pl.BlockSpec((tq, w), lambda g, p, it, jt: (jt[p], ng + g)),
                  pl.BlockSpec((tq, w), lambda g, p, it, jt: (jt[p], 2 * ng + g)),
                  pl.BlockSpec((tq, w), qb), pl.BlockSpec((tq, w), qb),
                  pl.BlockSpec((hb, tq, 128), lambda g, p, it, jt: (g, it[p], 0)),
                  pl.BlockSpec((hb, 1, tq), lambda g, p, it, jt: (g, 0, it[p])),
                  pl.BlockSpec((hb, 1, tq), lambda g, p, it, jt: (g, 0, jt[p]))],
        out_specs=[pl.BlockSpec((t, w), lambda g, p, it, jt: (0, g)), pl.BlockSpec((tq, w), kb),
                   pl.BlockSpec((tq, w), kb), pl.BlockSpec((hb, 1, tq), lambda g, p, it, jt: (g, 0, jt[p]))],
        out_shape=[jax.ShapeDtypeStruct((t, heads * 128), BF16), jax.ShapeDtypeStruct((t, heads * 128), BF16),
                   jax.ShapeDtypeStruct((t, heads * 128), BF16), jax.ShapeDtypeStruct((heads, 1, t), F32)],
        args=(i_tab, j_tab, qkv, qkv, qkv, o, do, lse, c3, c3), sem=("arbitrary", "arbitrary"),
        scratch=[pltpu.VMEM((hb, t, 128), F32), pltpu.VMEM((hb, tq, 128), F32), pltpu.VMEM((hb, tq, 128), F32),
                 pltpu.VMEM((hb, 1, tq), F32)], comm=comm)
    return outs, couts


def _tril_mask():
    r = lax.broadcasted_iota(jnp.int32, (128, 128), 0)
    c = lax.broadcasted_iota(jnp.int32, (128, 128), 1)
    return r >= c


def _sgu_fwd(uv, g, w, bst, *, name):
    t = uv.shape[0]
    sw = uv.shape[1] // 2
    groups = sw // 128
    tr = _tile(t, 512)

    def body(u_ref, v_ref, g_ref, w_ref, b_ref, y_ref):
        gv = _gelu(v_ref[...].astype(F32))
        inv = lax.rsqrt(jnp.mean(gv * gv, axis=-1, keepdims=True) + RMS_EPS)
        vn = ((gv * inv) * g_ref[...]).astype(BF16)
        gu = _gelu(u_ref[...].astype(F32))
        mask = _tril_mask()
        for gi in range(groups):
            wg = jnp.where(mask, w_ref[gi], 0.0).astype(BF16)
            bcol = b_ref[:, gi:gi + 1]
            cs = slice(gi * 128, (gi + 1) * 128)
            for ci in range(tr // 128):
                rs = slice(ci * 128, (ci + 1) * 128)
                mixed = jnp.dot(wg, vn[rs, cs], preferred_element_type=F32) + bcol
                y_ref[rs, cs] = (gu[rs, cs] * mixed).astype(BF16)

    return _pcall(body, name=name, grid=(t // tr,),
                  in_specs=[pl.BlockSpec((tr, sw), lambda i: (i, 0)), pl.BlockSpec((tr, sw), lambda i: (i, 1)),
                            pl.BlockSpec((1, sw), lambda i: (0, 0)),
                            pl.BlockSpec((groups, 128, 128), lambda i: (0, 0, 0)),
                            pl.BlockSpec((128, 128), lambda i: (0, 0))],
                  out_specs=[pl.BlockSpec((tr, sw), lambda i: (i, 0))],
                  out_shape=[jax.ShapeDtypeStruct((t, sw), BF16)], args=(uv, uv, g, w, bst),
                  sem=("parallel",))[0][0]


def _sgu_bwd(uv, g, w, bst, dy, *, name):
    t = uv.shape[0]
    sw = uv.shape[1] // 2
    groups = sw // 128
    tr = _tile(t, 256)
    nsteps = t // tr

    def body(u_ref, v_ref, g_ref, w_ref, b_ref, dy_ref, duv_ref, dw_ref, db_ref, dg_ref, dvn_s, dgu_s):
        step = pl.program_id(0)

        @pl.when(step == 0)
        def _():
            dw_ref[...] = jnp.zeros_like(dw_ref)
            db_ref[...] = jnp.zeros_like(db_ref)
            dg_ref[...] = jnp.zeros_like(dg_ref)

        vf = v_ref[...].astype(F32)
        gv, gv_grad = _gelu_and_grad(vf)
        inv = lax.rsqrt(jnp.mean(gv * gv, axis=-1, keepdims=True) + RMS_EPS)
        xn = gv * inv
        gvec = g_ref[...]
        vn = (xn * gvec).astype(BF16)
        uf = u_ref[...].astype(F32)
        gu, gu_grad = _gelu_and_grad(uf)
        dyf = dy_ref[...].astype(F32)
        mask = _tril_mask()
        lane = lax.broadcasted_iota(jnp.int32, (128, 128), 1)
        dball = jnp.zeros((128, 128), F32)
        for gi in range(groups):
            wg = jnp.where(mask, w_ref[gi], 0.0).astype(BF16)
            wgt = wg.T
            bcol = b_ref[:, gi:gi + 1]
            cs = slice(gi * 128, (gi + 1) * 128)
            dwg = jnp.zeros((128, 128), F32)
            dbg = jnp.zeros((128, 1), F32)
            for ci in range(tr // 128):
                rs = slice(ci * 128, (ci + 1) * 128)
                vnb = vn[rs, cs]
                mixed = jnp.dot(wg, vnb, preferred_element_type=F32) + bcol
                dgu_s[rs, cs] = dyf[rs, cs] * mixed
                dmix = dyf[rs, cs] * gu[rs, cs]
                dmb = dmix.astype(BF16)
                dvn_s[rs, cs] = jnp.dot(wgt, dmb, preferred_element_type=F32)
                dwg = dwg + lax.dot_general(dmb, vnb, (((1,), (1,)), ((), ())), preferred_element_type=F32)
                dbg = dbg + jnp.sum(dmix, axis=1, keepdims=True)
            dw_ref[gi] += dwg
            dball = dball + jnp.where(lane == gi, dbg, 0.0)
        db_ref[...] += dball
        dvn = dvn_s[...]
        dg_ref[...] += jnp.sum(dvn * xn, axis=0, keepdims=True)
        dxn = dvn * gvec
        dgv = inv * (dxn - xn * jnp.mean(dxn * xn, axis=-1, keepdims=True))
        duv_ref[:, 0:sw] = (dgu_s[...] * gu_grad).astype(BF16)
        duv_ref[:, sw:2 * sw] = (dgv * gv_grad).astype(BF16)

        @pl.when(step == nsteps - 1)
        def _():
            for gi in range(groups):
                dw_ref[gi] = jnp.where(mask, dw_ref[gi], 0.0)

    return _pcall(body, name=name, grid=(nsteps,),
                  in_specs=[pl.BlockSpec((tr, sw), lambda i: (i, 0)), pl.BlockSpec((tr, sw), lambda i: (i, 1)),
                            pl.BlockSpec((1, sw), lambda i: (0, 0)),
                            pl.BlockSpec((groups, 128, 128), lambda i: (0, 0, 0)),
                            pl.BlockSpec((128, 128), lambda i: (0, 0)), pl.BlockSpec((tr, sw), lambda i: (i, 0))],
                  out_specs=[pl.BlockSpec((tr, 2 * sw), lambda i: (i, 0)),
                             pl.BlockSpec((groups, 128, 128), lambda i: (0, 0, 0)),
                             pl.BlockSpec((128, 128), lambda i: (0, 0)), pl.BlockSpec((1, sw), lambda i: (0, 0))],
                  out_shape=[jax.ShapeDtypeStruct((t, 2 * sw), BF16), jax.ShapeDtypeStruct((groups, 128, 128), F32),
                             jax.ShapeDtypeStruct((128, 128), F32), jax.ShapeDtypeStruct((1, sw), F32)],
                  args=(uv, uv, g, w, bst, dy), sem=("arbitrary",),
                  scratch=[pltpu.VMEM((tr, sw), F32), pltpu.VMEM((tr, sw), F32)])[0]


def _merge_fwd(ya, yb, wa, wb, gates, *, name, comm=None):
    t, kdim = ya.shape
    nsh, _, ns = wa.shape
    d = nsh * ns
    tm = _tile(t, 1024)

    def body(ya_ref, yb_ref, wa_ref, wb_ref, ga_ref, gb_ref, mg_ref, za_ref, zb_ref):
        za = jnp.dot(ya_ref[...], wa_ref[...], preferred_element_type=F32)
        zb = jnp.dot(yb_ref[...], wb_ref[...], preferred_element_type=F32)
        sa = _sigmoid(ga_ref[...].astype(F32))
        sb = _sigmoid(gb_ref[...].astype(F32))
        mg_ref[...] = (sa * za + sb * zb).astype(BF16)
        za_ref[...] = za.astype(BF16)
        zb_ref[...] = zb.astype(BF16)

    yspec = pl.BlockSpec((tm, kdim), lambda i, j: (i, 0))
    wspec = pl.BlockSpec((None, kdim, ns), lambda i, j: (j, 0, 0))
    ospec = pl.BlockSpec((tm, ns), lambda i, j: (i, j))
    outs, couts = _pcall(body, name=name, grid=(t // tm, nsh),
                         in_specs=[yspec, yspec, wspec, wspec, ospec, pl.BlockSpec((tm, ns), lambda i, j: (i, nsh + j))],
                         out_specs=[ospec, ospec, ospec], out_shape=[jax.ShapeDtypeStruct((t, d), BF16)] * 3,
                         args=(ya, yb, wa, wb, gates, gates), sem=("parallel", "parallel"), comm=comm)
    return outs, couts


def _merge_bwd(dmg, gates, za, zb, *, name):
    t, d = dmg.shape
    tr = _tile(t, 256)

    def body(dm_ref, ga_ref, gb_ref, za_ref, zb_ref, dza_ref, dzb_ref, dg_ref):
        dm = dm_ref[...].astype(F32)
        sa = _sigmoid(ga_ref[...].astype(F32))
        sb = _sigmoid(gb_ref[...].astype(F32))
        dza_ref[...] = (dm * sa).astype(BF16)
        dzb_ref[...] = (dm * sb).astype(BF16)
        dg_ref[:, 0:d] = (dm * za_ref[...].astype(F32) * (sa * (1.0 - sa))).astype(BF16)
        dg_ref[:, d:2 * d] = (dm * zb_ref[...].astype(F32) * (sb * (1.0 - sb))).astype(BF16)

    row = pl.BlockSpec((tr, d), lambda i: (i, 0))
    return _pcall(body, name=name, grid=(t // tr,),
                  in_specs=[row, row, pl.BlockSpec((tr, d), lambda i: (i, 1)), row, row],
                  out_specs=[row, row, pl.BlockSpec((tr, 2 * d), lambda i: (i, 0))],
                  out_shape=[jax.ShapeDtypeStruct((t, d), BF16), jax.ShapeDtypeStruct((t, d), BF16),
                             jax.ShapeDtypeStruct((t, 2 * d), BF16)],
                  args=(dmg, gates, gates, za, zb), sem=("parallel",))[0]


def _shift_down(ext, k, rows):
    return pltpu.roll(ext, k, 0)[8:8 + rows]


def _conv_fwd(ab, cw, cb, *, name):
    t = ab.shape[0]
    dff = ab.shape[1] // 2
    tr, tc = _tile(t, 512), _tile(dff, 512)
    nc = dff // tc
    r8 = tr // 8

    def body(a_ref, ap_ref, b_ref, cw_ref, cb_ref, g_ref):
        i = pl.program_id(0)
        prev = ap_ref[...].astype(F32) * jnp.where(i > 0, 1.0, 0.0)
        a = a_ref[...].astype(F32)
        ext = jnp.concatenate([prev, a], axis=0)
        acc = cb_ref[...] + cw_ref[0:1, :] * _shift_down(ext, 2, tr) + cw_ref[1:2, :] * _shift_down(ext, 1, tr) \
            + cw_ref[2:3, :] * a
        g_ref[...] = (_gelu(acc) * b_ref[...].astype(F32)).astype(BF16)

    return _pcall(body, name=name, grid=(t // tr, nc),
                  in_specs=[pl.BlockSpec((tr, tc), lambda i, j: (i, j)),
                            pl.BlockSpec((8, tc), lambda i, j: (jnp.maximum(i * r8 - 1, 0), j)),
                            pl.BlockSpec((tr, tc), lambda i, j: (i, nc + j)),
                            pl.BlockSpec((CONV_WIDTH, tc), lambda i, j: (0, j)),
                            pl.BlockSpec((1, tc), lambda i, j: (0, j))],
                  out_specs=[pl.BlockSpec((tr, tc), lambda i, j: (i, j))],
                  out_shape=[jax.ShapeDtypeStruct((t, dff), BF16)], args=(ab, ab, ab, cw, cb),
                  sem=("parallel", "parallel"))[0][0]


def _conv_bwd(ab, cw, cb, dgg, *, name, comm=None):
    t = ab.shape[0]
    dff = ab.shape[1] // 2
    tr, tc = _tile(t, 256), _tile(dff, 512)
    nc, nr = dff // tc, t // tr
    r8 = tr // 8
    ext_rows = tr + 16

    def body(a_ref, ap_ref, an_ref, b_ref, bn_ref, dg_ref, dgn_ref, cw_ref, cb_ref,
             da_ref, db_ref, dcw_ref, dcb_ref):
        i = pl.program_id(1)
        has_prev = jnp.where(i > 0, 1.0, 0.0)
        has_next = jnp.where(i < nr - 1, 1.0, 0.0)
        a_ext = jnp.concatenate([ap_ref[...].astype(F32) * has_prev, a_ref[...].astype(F32),
                                 an_ref[...].astype(F32)], axis=0)
        b_ext = jnp.concatenate([b_ref[0:8, :].astype(F32), b_ref[...].astype(F32), bn_ref[...].astype(F32)], axis=0)
        dg_ext = jnp.concatenate([dg_ref[0:8, :].astype(F32), dg_ref[...].astype(F32),
                                  dgn_ref[...].astype(F32) * has_next], axis=0)
        w0, w1, w2 = cw_ref[0:1, :], cw_ref[1:2, :], cw_ref[2:3, :]
        a_m2 = pltpu.roll(a_ext, 2, 0)
        a_m1 = pltpu.roll(a_ext, 1, 0)
        acc = cb_ref[...] + w0 * a_m2 + w1 * a_m1 + w2 * a_ext
        gel, gel_grad = _gelu_and_grad(acc)
        dacc = dg_ext * b_ext * gel_grad
        d_a = w2 * dacc + w1 * pltpu.roll(dacc, ext_rows - 1, 0) + w0 * pltpu.roll(dacc, ext_rows - 2, 0)
        da_ref[...] = d_a[8:8 + tr].astype(BF16)
        db_ref[...] = (dg_ext[8:8 + tr] * gel[8:8 + tr]).astype(BF16)
        dm = dacc[8:8 + tr]
        dcw = jnp.concatenate([jnp.sum(dm * a_m2[8:8 + tr], axis=0, keepdims=True),
                               jnp.sum(dm * a_m1[8:8 + tr], axis=0, keepdims=True),
                               jnp.sum(dm * a_ext[8:8 + tr], axis=0, keepdims=True)], axis=0)
        dcb = jnp.sum(dm, axis=0, keepdims=True)

        @pl.when(i == 0)
        def _():
            dcw_ref[...] = dcw
            dcb_ref[...] = dcb

        @pl.when(i > 0)
        def _():
            dcw_ref[...] += dcw
            dcb_ref[...] += dcb

    cur = lambda off: pl.BlockSpec((tr, tc), lambda j, i: (i, off + j))
    prv = lambda off: pl.BlockSpec((8, tc), lambda j, i: (jnp.maximum(i * r8 - 1, 0), off + j))
    nxt = lambda off: pl.BlockSpec((8, tc), lambda j, i: (jnp.minimum((i + 1) * r8, nr * r8 - 1), off + j))
    return _pcall(body, name=name, grid=(nc, nr),
                  in_specs=[cur(0), prv(0), nxt(0), cur(nc), nxt(nc), cur(0), nxt(0),
                            pl.BlockSpec((CONV_WIDTH, tc), lambda j, i: (0, j)),
                            pl.BlockSpec((1, tc), lambda j, i: (0, j))],
                  out_specs=[cur(0), cur(0), pl.BlockSpec((CONV_WIDTH, tc), lambda j, i: (0, j)),
                             pl.BlockSpec((1, tc), lambda j, i: (0, j))],
                  out_shape=[jax.ShapeDtypeStruct((t, dff), BF16), jax.ShapeDtypeStruct((t, dff), BF16),
                             jax.ShapeDtypeStruct((CONV_WIDTH, dff), F32), jax.ShapeDtypeStruct((1, dff), F32)],
                  args=(ab, ab, ab, ab, ab, dgg, dgg, cw, cb), sem=("parallel", "arbitrary"), comm=comm)


class _InLayout:
    def __init__(self, nsi, heads, fw, sw, d):
        self.nsi = nsi
        self.pw = -(-(nsi + 127) // 128) * 128
        self.o_u = 3 * fw + heads
        self.ins = 128 - heads
        self.widths = [3 * fw, 128, 2 * sw, 2 * d]
        self.total = sum(self.widths)

    def regions(self, s):
        g0 = self.nsi * s
        out = []
        lo, hi = g0, min(g0 + self.nsi, self.o_u)
        if lo < hi:
            out.append((lo, hi, g0 % 128, g0 // 128, 0, hi - g0))
        lo, hi = max(g0, self.o_u), g0 + self.nsi
        if lo < hi:
            gi = g0 + self.ins
            out.append((lo + self.ins, hi + self.ins, gi % 128, gi // 128, lo - g0, self.nsi))
        return out


def _lane_mix(lane, pieces):
    val = None
    for piece, lo, hi in pieces:
        if lo <= 0 and hi >= 128:
            val = piece
        else:
            val = jnp.where((lane >= lo) & (lane < hi), piece, jnp.zeros_like(piece) if val is None else val)
    return val


def _w_in_unpack(g, lay, *, name):
    _, d, pw = g.shape
    tr = _tile(d, 128)
    ntiles = lay.total // 128
    plan = [[] for _ in range(ntiles)]
    for s in range(N_DEV):
        for r, (lo, hi, _, tile0, _, _) in enumerate(lay.regions(s)):
            for tt in range(lo // 128, (hi - 1) // 128 + 1):
                plan[tt].append((s, r, tt - tile0, lo - 128 * tt, hi - 128 * tt))
    bounds = np.cumsum([0] + [w // 128 for w in lay.widths])

    def body(g_ref, *o_refs):
        lane = lax.broadcasted_iota(jnp.int32, (tr, 128), 1)
        rolled = {}

        def src(s, r):
            if (s, r) not in rolled:
                shift = lay.regions(s)[r][2]
                xs = g_ref[s].astype(F32)
                rolled[(s, r)] = pltpu.roll(xs, shift, 1) if shift else xs
            return rolled[(s, r)]

        for tt in range(ntiles):
            val = _lane_mix(lane, [(src(s, r)[:, 128 * st:128 * (st + 1)], lo, hi) for s, r, st, lo, hi in plan[tt]])
            o = int(np.searchsorted(bounds, tt, side="right")) - 1
            lt = tt - int(bounds[o])
            o_refs[o][:, 128 * lt:128 * (lt + 1)] = val.astype(BF16)

    return _pcall(body, name=name, grid=(d // tr,), in_specs=[pl.BlockSpec((N_DEV, tr, pw), lambda i: (0, i, 0))],
                  out_specs=[pl.BlockSpec((tr, w), lambda i: (i, 0)) for w in lay.widths],
                  out_shape=[jax.ShapeDtypeStruct((d, w), BF16) for w in lay.widths], args=(g,),
                  sem=("parallel",))[0]


def _w_in_grad_pack(pieces, lay, *, name):
    d = pieces[0].shape[0]
    tr = _tile(d, 128)
    assert sum(p.shape[1] for p in pieces) == lay.total
    starts = np.cumsum([0] + [p.shape[1] // 128 for p in pieces])
    ntot = lay.total // 128
    nfull, rem = lay.nsi // 128, lay.nsi % 128
    npc = len(pieces)

    def body(*refs):
        p_refs, o_ref = refs[:npc], refs[npc]
        lane = lax.broadcasted_iota(jnp.int32, (tr, 128), 1)

        def padded_tile(tau):
            pi = int(np.searchsorted(starts, tau, side="right")) - 1
            lt = tau - int(starts[pi])
            return p_refs[pi][:, 128 * lt:128 * (lt + 1)].astype(F32)

        for s in range(N_DEV):
            unrolled = []
            for _, _, shift, tile0, j_lo, j_hi in lay.regions(s):
                win = jnp.concatenate([padded_tile(min(tile0 + q, ntot - 1)) for q in range(lay.pw // 128)], axis=1)
                unrolled.append((pltpu.roll(win, lay.pw - shift, 1) if shift else win, j_lo, j_hi))
            for kt in range(nfull + (1 if rem else 0)):
                val = _lane_mix(lane, [(u[:, 128 * kt:128 * (kt + 1)], j_lo - 128 * kt, j_hi - 128 * kt)
                                       for u, j_lo, j_hi in unrolled if j_lo < 128 * (kt + 1) and j_hi > 128 * kt])
                if kt < nfull:
                    o_ref[s, :, 128 * kt:128 * (kt + 1)] = val.astype(BF16)
                else:
                    o_ref[s, :, 128 * nfull:lay.nsi] = val[:, :rem].astype(BF16)

    return _pcall(body, name=name, grid=(d // tr,),
                  in_specs=[pl.BlockSpec((tr, p.shape[1]), lambda i: (i, 0)) for p in pieces],
                  out_specs=[pl.BlockSpec((N_DEV, tr, lay.nsi), lambda i: (0, i, 0))],
                  out_shape=[jax.ShapeDtypeStruct((N_DEV, d, lay.nsi), BF16)], args=tuple(pieces),
                  sem=("parallel",))[0][0]


def _row_tile(rows, cols):
    for cand in (512, 256, 128, 64, 32, 16, 8):
        if rows % cand == 0 and cand * cols * 4 <= 2 * 1024 * 1024:
            return cand
    return rows


def _pair_add(grad8, recv4, cidx, *, name):
    _, rows, cols = grad8.shape
    tr = _row_tile(rows, cols)

    def body(c_ref, g_ref, r_ref, o_ref):
        o_ref[...] = (g_ref[...].astype(F32) + r_ref[...].astype(F32)).astype(BF16)

    blk = (None, tr, cols)
    return _pcall(body, name=name, grid=(4, rows // tr), prefetch=1,
                  in_specs=[pl.BlockSpec(blk, lambda k, i, c_ref: (2 * k + c_ref[0], i, 0)),
                            pl.BlockSpec(blk, lambda k, i, c_ref: (k, i, 0))],
                  out_specs=[pl.BlockSpec(blk, lambda k, i, c_ref: (k, i, 0))],
                  out_shape=[jax.ShapeDtypeStruct((4, rows, cols), BF16)], args=(cidx, grad8, recv4),
                  sem=("parallel", "parallel"))[0][0]


def _adamw_math(w, g, m, v):
    m = ADAM_B1 * m + (1.0 - ADAM_B1) * g
    v = ADAM_B2 * v + (1.0 - ADAM_B2) * (g * g)
    m_hat = m / (1.0 - ADAM_B1 ** ADAM_STEP)
    v_hat = v / (1.0 - ADAM_B2 ** ADAM_STEP)
    delta = -ADAM_LR * (m_hat / (jnp.sqrt(v_hat) + ADAM_EPS) + ADAM_WD * w)
    return delta, m, v


def _shard_adamw(part4, recv3, w, m, v, kidx, layer, bufs, *, name):
    depth, rows, cols = w.shape
    tr = _row_tile(rows, cols)

    def body(k_ref, p_ref, r0_ref, r1_ref, r2_ref, w_ref, m_ref, v_ref, *rest):
        g_out, d_out, m_out, v_out = rest[-4:]
        g = ((p_ref[...].astype(F32) + r0_ref[...].astype(F32)) + r1_ref[...].astype(F32)) + r2_ref[...].astype(F32)
        delta, mn, vn = _adamw_math(w_ref[...], g, m_ref[...], v_ref[...])
        g_out[...] = g
        d_out[...] = delta
        m_out[...] = mn
        v_out[...] = vn

    blk = (None, tr, cols)
    rspec = lambda j: pl.BlockSpec(blk, lambda i, k_ref: (j, i, 0))
    espec = pl.BlockSpec(blk, lambda i, k_ref: (layer, i, 0))
    in_specs = [pl.BlockSpec(blk, lambda i, k_ref: (k_ref[0], i, 0)), rspec(0), rspec(1), rspec(2), espec, espec, espec]
    args = [kidx, part4, recv3, recv3, recv3, w, m, v]
    aliases = {}
    if bufs is not None:
        in_specs += [ANY] * 4
        aliases = {8 + q: q for q in range(4)}
        args += list(bufs)
    return _pcall(body, name=name, grid=(rows // tr,), prefetch=1, in_specs=in_specs, out_specs=[espec] * 4,
                  out_shape=[jax.ShapeDtypeStruct((depth, rows, cols), F32)] * 4, args=args, sem=("parallel",),
                  aliases=aliases)[0]


def _small_adamw(gath, w, m, v, *, name):
    rows = w.shape[0]

    def body(g_ref, w_ref, m_ref, v_ref, g_out, d_out, m_out, v_out):
        g = g_ref[0]
        for dev in range(1, N_DEV):
            g = g + g_ref[dev]
        delta, mn, vn = _adamw_math(w_ref[...], g, m_ref[...], v_ref[...])
        g_out[...] = g
        d_out[...] = delta
        m_out[...] = mn
        v_out[...] = vn

    tr = _row_tile(rows, 128 * N_DEV)
    espec = pl.BlockSpec((tr, 128), lambda i: (i, 0))
    return _pcall(body, name=name, grid=(rows // tr,),
                  in_specs=[pl.BlockSpec((N_DEV, tr, 128), lambda i: (0, i, 0)), espec, espec, espec],
                  out_specs=[espec] * 4, out_shape=[jax.ShapeDtypeStruct((rows, 128), F32)] * 4,
                  args=(gath, w, m, v), sem=("parallel",))[0]


def _pack(arrs):
    flat = jnp.concatenate([a.reshape(-1) for a in arrs])
    total = flat.shape[0]
    rows = -(-total // (128 * 64)) * 64
    return jnp.pad(flat, (0, rows * 128 - total)).reshape(rows, 128)


def _unpack(packed, like):
    flat = packed.reshape(-1)
    out, off = [], 0
    for a in like:
        out.append(flat[off:off + a.size].reshape(a.shape))
        off += a.size
    return out


def kernel(x, g_mix, w_in, b_forget, g_sgu, w_spatial, b_spatial, w_branch_a, w_branch_b, w_out, g_ffn, w_up, conv_w, conv_b, w_down, g_final, loss_target, m_g_mix, m_w_in, m_b_forget, m_g_sgu, m_w_spatial, m_b_spatial, m_w_branch_a, m_w_branch_b, m_w_out, m_g_ffn, m_w_up, m_conv_w, m_conv_b, m_w_down, m_g_final, v_g_mix, v_w_in, v_b_forget, v_g_sgu, v_w_spatial, v_b_spatial, v_w_branch_a, v_w_branch_b, v_w_out, v_g_ffn, v_w_up, v_conv_w, v_conv_b, v_w_down, v_g_final):
    depth, d = g_mix.shape
    heads = b_forget.shape[1]
    fw = heads * HEAD_DIM
    sw = g_sgu.shape[1]
    dff = conv_b.shape[1]
    t = x.shape[1]
    nsi = w_in.shape[2]
    nsu = w_up.shape[2]
    o_f, o_u, o_g = 3 * fw, 3 * fw + heads, 3 * fw + heads + 2 * sw

    bpad = jnp.pad(b_forget, ((0, 0), (0, 128 - heads)))
    bst = jnp.pad(jnp.swapaxes(b_spatial, 1, 2), ((0, 0), (0, 0), (0, 128 - b_spatial.shape[1])))

    lay = _InLayout(nsi, heads, fw, sw, d)

    def in_shards(l):
        return [jnp.pad(w_in[l].astype(BF16), ((0, 0), (0, lay.pw - nsi)))]

    def small_shards(l):
        return [w_branch_a[l].astype(BF16), w_branch_b[l].astype(BF16), w_out[l].astype(BF16), conv_w[l]]

    def ffn_shards(l):
        return [w_up[l].astype(BF16), w_down[l].astype(BF16)]

    def unpack_in(bufs, l):
        wqkv, wf, wuv, wg = _w_in_unpack(bufs[0], lay, name=f"unpack_w_in_{l}")
        return dict(wqkv=wqkv, wf=wf, wuv=wuv, wg=wg)

    def unpack_small(bufs):
        g_wa, g_wb, g_wo, g_cw = bufs
        return dict(wa=g_wa, wb=g_wb, wo=g_wo.reshape(d, d), cw=jnp.moveaxis(g_cw, 0, 1).reshape(CONV_WIDTH, dff))

    mixer_w = [None] * depth
    ffn_w = [None] * depth
    first = _comm_only(_gather_first(in_shards(0)), name="gather_in_first_0")
    mixer_w[0] = unpack_in(_comm_only(_gather_second(first), name="gather_in_second_0"), 0)
    small_first = None

    xs = x[0]
    saved = []
    for l in range(depth):
        n = lambda s: f"{s}_{l}"
        mw = mixer_w[l]
        h = _rms_fwd(xs, g_mix[l][None], name=n("rms_mix"))
        if small_first is None:
            qkv, small_first = _mm(h, mw["wqkv"], name=n("proj_qkv"), comm=_gather_first(small_shards(l)))
        else:
            qkv = _mm(h, mw["wqkv"], name=n("proj_qkv"))
        uv = _mm(h, mw["wuv"], name=n("proj_uv"))
        gates, small_bufs = _mm(h, mw["wg"], name=n("proj_gates"), comm=_gather_second(small_first))
        mw.update(unpack_small(small_bufs))
        flog = _mm(h, mw["wf"], out_dtype=F32, name=n("proj_forget"))
        c3 = _forget_fwd(flog, bpad[l][None], name=n("forget_fwd")).reshape(128, 1, t)
        (ya, ya32, lse), ffn_first = _fox_fwd(qkv, c3, heads, name=n("fox_fwd"), comm=_gather_first(ffn_shards(l)))
        yb = _sgu_fwd(uv, g_sgu[l][None], w_spatial[l], bst[l], name=n("sgu_fwd"))
        (merged, za, zb), _ = _merge_fwd(ya, yb, mw["wa"], mw["wb"], gates, name=n("merge_fwd"))
        x1, ffn_bufs = _mm(merged, mw["wo"], out_dtype=F32, res=xs, name=n("out_proj"), comm=_gather_second(ffn_first))
        g_wu, g_wd = ffn_bufs
        ffn_w[l] = dict(wu=g_wu, wd=g_wd.reshape(dff, d))
        h2 = _rms_fwd(x1, g_ffn[l][None], name=n("rms_ffn"))
        if l + 1 < depth:
            ab, in_first = _mm(h2, g_wu, b_cols=True, name=n("ffn_up"), comm=_gather_first(in_shards(l + 1)))
        else:
            ab = _mm(h2, g_wu, b_cols=True, name=n("ffn_up"))
        gg = _conv_fwd(ab, mw["cw"], conv_b[l][None], name=n("conv_fwd"))
        if l + 1 < depth:
            both = _Join([_gather_second(in_first), _gather_first(small_shards(l + 1))])
            x2, nxt = _mm(gg, ffn_w[l]["wd"], out_dtype=F32, res=x1, name=n("ffn_down"), comm=both)
            in_bufs, small_first = both.split(nxt)
            mixer_w[l + 1] = unpack_in(in_bufs, l + 1)
        else:
            x2 = _mm(gg, ffn_w[l]["wd"], out_dtype=F32, res=x1, name=n("ffn_down"))
        saved.append((xs, h, qkv, uv, gates, flog, c3, ya, ya32, lse, yb, merged, za, zb, x1, h2, ab, gg))
        xs = x2

    loss_row, dx, dxb, d_g_final = _final_loss(xs, g_final[None], loss_target[0], name="final_loss")

    cidx = lax.axis_index("c").astype(jnp.int32).reshape(1)
    kidx = (2 * lax.axis_index("x") + lax.axis_index("y")).astype(jnp.int32).reshape(1)
    small_g = {k: [None] * depth for k in ("g_mix", "b_forget", "g_sgu", "w_spatial", "b_spatial", "g_ffn", "conv_b")}
    parts = {}
    recvs = {}
    mixer_names = ["w_in", "w_branch_a", "w_branch_b", "w_out"]
    ffn_names = ["w_up", "w_down", "conv_w"]
    for nm in mixer_names + ffn_names:
        parts[nm], recvs[nm] = [None] * depth, [None] * depth
    pending_mixer = None
    for l in reversed(range(depth)):
        n = lambda s: f"{s}_{l}"
        mw, fw_ = mixer_w[l], ffn_w[l]
        xs, h, qkv, uv, gates, flog, c3, ya, ya32, lse, yb, merged, za, zb, x1, h2, ab, gg = saved[l]
        g_wd = _mm(gg, dxb, ta=True, name=n("d_w_down"))
        if pending_mixer is not None:
            dgg, got = _mm(dxb, fw_["wd"], tb=True, name=n("d_gg"), comm=_scatter_second(pending_mixer[1:]))
            for nm, r in zip(mixer_names[1:], got):
                recvs[nm][l + 1] = r
            (d_a, d_b, d_cw, d_cb), got = _conv_bwd(ab, mw["cw"], conv_b[l][None], dgg, name=n("conv_bwd"),
                                                    comm=_scatter_second(pending_mixer[:1]))
            recvs[mixer_names[0]][l + 1] = got[0]
        else:
            dgg = _mm(dxb, fw_["wd"], tb=True, name=n("d_gg"))
            (d_a, d_b, d_cw, d_cb), _ = _conv_bwd(ab, mw["cw"], conv_b[l][None], dgg, name=n("conv_bwd"))
        small_g["conv_b"][l] = d_cb[0]
        g_wu = _mm(h2, d_a, ta=True, out_cols=(N_DEV, 0, nsu), name=n("d_w_up_a"))
        g_wu = _mm(h2, d_b, ta=True, out_cols=(N_DEV, N_DEV // 2, nsu), out_alias=g_wu, name=n("d_w_up_b"))
        ffn_grads = [g_wu, g_wd.reshape(N_DEV, dff // N_DEV, d),
                     jnp.moveaxis(d_cw.reshape(CONV_WIDTH, N_DEV, dff // N_DEV), 1, 0)]
        dh2 = _mm(d_a, fw_["wu"], tb=True, b_cols=True, out_dtype=F32, name=n("d_h2_a"))
        dh2, from_sib = _mm(d_b, fw_["wu"], tb=True, b_cols=True, b_off=N_DEV // 2, out_dtype=F32, res=dh2,
                            name=n("d_h2_b"), comm=_scatter_first(ffn_grads))
        ffn_parts = [_pair_add(g8, r4, cidx, name=n(f"pair_add_{nm}")) for nm, g8, r4 in zip(ffn_names, ffn_grads, from_sib)]
        for nm, p in zip(ffn_names, ffn_parts):
            parts[nm][l] = p
        dx, dxb, dg = _rms_bwd(x1, g_ffn[l][None], dh2, dx, name=n("rms_ffn_bwd"))
        small_g["g_ffn"][l] = dg[0]
        g_wo = _mm(merged, dxb, ta=True, name=n("d_w_out"))
        dmg = _mm(dxb, mw["wo"], tb=True, name=n("d_merged"))
        dza, dzb, dgates = _merge_bwd(dmg, gates, za, zb, name=n("merge_bwd"))
        g_wa = _mm(ya, dza, ta=True, out_cols=(N_DEV, 0, d // N_DEV), name=n("d_w_a"))
        g_wb = _mm(yb, dzb, ta=True, out_cols=(N_DEV, 0, d // N_DEV), name=n("d_w_b"))
        dya = _mm(dza, mw["wa"], tb=True, b_cols=True, name=n("d_ya"))
        dyb = _mm(dzb, mw["wb"], tb=True, b_cols=True, name=n("d_yb"))
        duv, d_ws, d_bst, d_gs = _sgu_bwd(uv, g_sgu[l][None], w_spatial[l], bst[l], dyb, name=n("sgu_bwd"))
        small_g["w_spatial"][l], small_g["g_sgu"][l] = d_ws, d_gs[0]
        small_g["b_spatial"][l] = d_bst[:, :b_spatial.shape[1]].T
        (dq, dk, dv, dc3), got = _fox_bwd(qkv, c3, ya32, dya, lse, heads, name=n("fox_bwd"),
                                          comm=_scatter_second(ffn_parts))
        for nm, r in zip(ffn_names, got):
            recvs[nm][l] = r
        dct = jnp.pad(dc3.reshape(heads, t), ((0, 128 - heads), (0, 0)))
        dflog, d_bf = _forget_bwd(flog, bpad[l][None], dct, name=n("forget_bwd"))
        small_g["b_forget"][l] = d_bf[0, :heads]
        gw = [_mm(h, dq, ta=True, name=n("d_w_q")), _mm(h, dk, ta=True, name=n("d_w_k")),
              _mm(h, dv, ta=True, name=n("d_w_v")), _mm(h, dflog, ta=True, name=n("d_w_forget")),
              _mm(h, duv, ta=True, name=n("d_w_uv")), _mm(h, dgates, ta=True, name=n("d_w_gates"))]
        g_in = _w_in_grad_pack(gw, lay, name=n("pack_d_w_in"))
        mixer_grads = [g_in, g_wa, g_wb, g_wo.reshape(N_DEV, d // N_DEV, d)]
        dh = _mm(dflog, mw["wf"], tb=True, out_dtype=F32, name=n("d_h_forget"))
        dh = _mm(dq, mw["wqkv"], tb=True, out_dtype=F32, res=dh, name=n("d_h_q"))
        dh = _mm(dk, mw["wqkv"], tb=True, b_off=fw, out_dtype=F32, res=dh, name=n("d_h_k"))
        dh = _mm(dv, mw["wqkv"], tb=True, b_off=2 * fw, out_dtype=F32, res=dh, name=n("d_h_v"))
        dh = _mm(duv, mw["wuv"], tb=True, out_dtype=F32, res=dh, name=n("d_h_uv"))
        dh, from_sib = _mm(dgates, mw["wg"], tb=True, out_dtype=F32, res=dh, name=n("d_h_gates"),
                           comm=_scatter_first(mixer_grads))
        pending_mixer = [_pair_add(g8, r4, cidx, name=n(f"pair_add_{nm}"))
                         for nm, g8, r4 in zip(mixer_names, mixer_grads, from_sib)]
        for nm, p in zip(mixer_names, pending_mixer):
            parts[nm][l] = p
        dx, dxb, dg = _rms_bwd(xs, g_mix[l][None], dh, dx, name=n("rms_mix_bwd"))
        small_g["g_mix"][l] = dg[0]
    grad_x = dx[None]
    for nm, r in zip(mixer_names, _comm_only(_scatter_second(pending_mixer), name="scatter_mixer_second_0")):
        recvs[nm][0] = r

    weights = {"w_in": (w_in, m_w_in, v_w_in), "w_branch_a": (w_branch_a, m_w_branch_a, v_w_branch_a),
               "w_branch_b": (w_branch_b, m_w_branch_b, v_w_branch_b), "w_out": (w_out, m_w_out, v_w_out),
               "w_up": (w_up, m_w_up, v_w_up), "conv_w": (conv_w, m_conv_w, v_conv_w),
               "w_down": (w_down, m_w_down, v_w_down)}
    res = {}
    for nm, (w, m, v) in weights.items():
        bufs = None
        for l in range(depth):
            bufs = _shard_adamw(parts[nm][l], recvs[nm][l], w, m, v, kidx, l, bufs, name=f"adamw_{nm}_{l}")
        res[nm] = bufs

    small = ["g_mix", "b_forget", "g_sgu", "w_spatial", "b_spatial", "g_ffn", "conv_b", "g_final", "loss"]
    zero = jnp.zeros((1,), F32)
    small_w = [g_mix, b_forget, g_sgu, w_spatial, b_spatial, g_ffn, conv_b, g_final, zero]
    small_m = [m_g_mix, m_b_forget, m_g_sgu, m_w_spatial, m_b_spatial, m_g_ffn, m_conv_b, m_g_final, zero]
    small_v = [v_g_mix, v_b_forget, v_g_sgu, v_w_spatial, v_b_spatial, v_g_ffn, v_conv_b, v_g_final, zero]
    small_grads = [jnp.stack(small_g[nm]) for nm in small[:-2]] + [d_g_final[0], loss_row[0, 0:1]]
    first = _comm_only(_gather_first([_pack(small_grads)]), name="gather_small_first")
    (gath,) = _comm_only(_gather_second(first), name="gather_small_second")
    outs = _small_adamw(gath, _pack(small_w), _pack(small_m), _pack(small_v), name="adamw_replicated")
    for nm, vals in zip(small, zip(*[_unpack(o, small_w) for o in outs])):
        res[nm] = list(vals)
    loss = res["loss"][0][0]

    order = ["g_mix", "w_in", "b_forget", "g_sgu", "w_spatial", "b_spatial", "w_branch_a", "w_branch_b", "w_out",
             "g_ffn", "w_up", "conv_w", "conv_b", "w_down", "g_final"]
    return (loss, grad_x, *[res[nm][0] for nm in order], *[res[nm][1] for nm in order],
            *[res[nm][2] for nm in order], *[res[nm][3] for nm in order])
```

```python
import numpy as np

import jax
import jax.numpy as jnp
from jax import lax
from jax.experimental import pallas as pl
from jax.experimental.pallas import tpu as pltpu

F32 = jnp.float32
BF16 = jnp.bfloat16

RMS_EPS = 1e-6
HEAD_DIM = 128
CONV_WIDTH = 3
ADAM_LR = 0.001
ADAM_B1 = 0.9
ADAM_B2 = 0.999
ADAM_EPS = 1e-08
ADAM_WD = 0.01
ADAM_STEP = 10
N_DEV = 8
V7X_VMEM_LIMIT = 58 * 1024 * 1024
MM_VMEM_BUDGET = 46 * 1024 * 1024
NEG = -1e30
ANY = pl.BlockSpec(memory_space=pl.ANY)
MESH = pl.DeviceIdType.MESH


def _tile(dim, pref):
    for t in (2048, 1024, 512, 256, 128):
        if t <= pref and dim % t == 0:
            return t
    return dim


def _gelu(x):
    t = jnp.tanh(0.7978845608028654 * (x + 0.044715 * (x * x * x)))
    return x * (0.5 * (1.0 + t))


def _gelu_and_grad(x):
    x2 = x * x
    t = jnp.tanh(0.7978845608028654 * (x + 0.044715 * (x2 * x)))
    cdf = 0.5 * (1.0 + t)
    dt = (1.0 - t * t) * (0.7978845608028654 * (1.0 + 3.0 * 0.044715 * x2))
    return x * cdf, cdf + 0.5 * x * dt


def _sigmoid(x):
    return 1.0 / (1.0 + jnp.exp(-x))


class _Comm:
    def __init__(self, srcs, new, alias, n_copies, emit):
        self.srcs = list(srcs)
        self.new = list(new)
        self.alias = list(alias)
        self.n_copies = n_copies
        self.emit = emit

    def split(self, couts):
        return [couts]


class _Join(_Comm):
    def __init__(self, comms):
        self.comms = comms
        srcs = [s for cm in comms for s in cm.srcs]
        new = [s for cm in comms for s in cm.new]
        alias = [s for cm in comms for s in cm.alias]

        def emit(src_refs, new_refs, alias_refs, sems):
            copies, s0, n0, a0, k0 = [], 0, 0, 0, sems[2]
            for cm in comms:
                copies += cm.emit(src_refs[s0:s0 + len(cm.srcs)], new_refs[n0:n0 + len(cm.new)],
                                  alias_refs[a0:a0 + len(cm.alias)], (sems[0], sems[1], k0))
                s0, n0, a0, k0 = s0 + len(cm.srcs), n0 + len(cm.new), a0 + len(cm.alias), k0 + cm.n_copies
            return copies

        super().__init__(srcs, new, alias, sum(cm.n_copies for cm in comms), emit)

    def split(self, couts):
        n_new = len(self.new)
        out, n0, a0 = [], 0, 0
        for cm in self.comms:
            out.append(couts[n0:n0 + len(cm.new)] + couts[n_new + a0:n_new + a0 + len(cm.alias)])
            n0, a0 = n0 + len(cm.new), a0 + len(cm.alias)
        return out


def _place():
    x, y, c = lax.axis_index("x"), lax.axis_index("y"), lax.axis_index("c")
    chips = [(1 - x, y), (x, 1 - y), (1 - x, 1 - y)]
    return x, y, c, chips


def _remote(src, dst, sems, k, to):
    return pltpu.make_async_remote_copy(src_ref=src, dst_ref=dst, send_sem=sems[0].at[sems[2] + k],
                                        recv_sem=sems[1].at[sems[2] + k], device_id=to, device_id_type=MESH)


def _gather_first(shards):
    n = len(shards)

    def emit(srcs, new, alias, sems):
        x, y, c, chips = _place()
        me = 4 * x + 2 * y + c
        copies = []
        for a in range(n):
            copies.append(pltpu.make_async_copy(srcs[a], new[a].at[me], sems[0].at[sems[2] + 5 * a + 4]))
            copies.append(_remote(srcs[a], new[a].at[me], sems, 5 * a, (x, y, 1 - c)))
            for j, chip in enumerate(chips):
                copies.append(_remote(srcs[a], new[a].at[me], sems, 5 * a + 1 + j, (*chip, c)))
        return copies

    new = [jax.ShapeDtypeStruct((N_DEV,) + s.shape, s.dtype) for s in shards]
    return _Comm(shards, new, [], 5 * n, emit)


def _gather_second(bufs):
    n = len(bufs)

    def emit(srcs, new, alias, sems):
        x, y, c, chips = _place()
        copies = []
        for a in range(n):
            for j, chip in enumerate(chips):
                blk = alias[a].at[4 * chip[0] + 2 * chip[1] + c]
                copies.append(_remote(blk, blk, sems, 3 * a + j, (x, y, 1 - c)))
        return copies

    return _Comm([], [], bufs, 3 * n, emit)


def _scatter_first(grads):
    n = len(grads)

    def emit(srcs, new, alias, sems):
        x, y, c, _ = _place()
        return [_remote(srcs[a].at[2 * k + 1 - c], new[a].at[k], sems, 4 * a + k, (x, y, 1 - c))
                for a in range(n) for k in range(4)]

    new = [jax.ShapeDtypeStruct((4,) + g.shape[1:], g.dtype) for g in grads]
    return _Comm(grads, new, [], 4 * n, emit)


def _scatter_second(parts):
    n = len(parts)

    def emit(srcs, new, alias, sems):
        x, y, c, chips = _place()
        return [_remote(srcs[a].at[2 * chip[0] + chip[1]], new[a].at[j], sems, 3 * a + j, (*chip, c))
                for a in range(n) for j, chip in enumerate(chips)]

    new = [jax.ShapeDtypeStruct((3,) + p.shape[1:], p.dtype) for p in parts]
    return _Comm(parts, new, [], 3 * n, emit)


def _pcall(body, *, name, grid, in_specs, out_specs, out_shape, args, sem, scratch=(), aliases=None, comm=None,
           prefetch=0):
    in_specs, out_specs, out_shape, scratch = list(in_specs), list(out_specs), list(out_shape), list(scratch)
    aliases = dict(aliases or {})
    n_in, n_out, n_scr = len(in_specs), len(out_shape), len(scratch)

    def make(body_fn, ins, outs, shapes, scr, sem_):
        params = pltpu.CompilerParams(dimension_semantics=sem_, vmem_limit_bytes=V7X_VMEM_LIMIT)
        if prefetch:
            spec = pltpu.PrefetchScalarGridSpec(num_scalar_prefetch=prefetch, grid=grid, in_specs=ins,
                                                out_specs=outs, scratch_shapes=scr)
            return pl.pallas_call(body_fn, name=name, grid_spec=spec, out_shape=shapes,
                                  input_output_aliases=aliases, compiler_params=params)
        return pl.pallas_call(body_fn, name=name, grid=grid, in_specs=ins, out_specs=outs, out_shape=shapes,
                              scratch_shapes=scr, input_output_aliases=aliases, compiler_params=params)

    if comm is None:
        return list(make(body, in_specs, out_specs, out_shape, scratch, sem)(*args)), []

    n_src, n_new, n_al = len(comm.srcs), len(comm.new), len(comm.alias)
    for a in range(n_al):
        aliases[prefetch + n_in + n_src + a] = n_out + n_new + a

    def wrapped(*refs):
        pre, refs = refs[:prefetch], refs[prefetch:]
        ins = refs[:n_in]
        src_refs = refs[n_in:n_in + n_src]
        o0 = n_in + n_src + n_al
        outs = refs[o0:o0 + n_out]
        new_refs = refs[o0 + n_out:o0 + n_out + n_new]
        alias_refs = refs[o0 + n_out + n_new:o0 + n_out + n_new + n_al]
        s0 = o0 + n_out + n_new + n_al
        scr = refs[s0:s0 + n_scr]
        send_sems, recv_sems = refs[s0 + n_scr], refs[s0 + n_scr + 1]
        first = pl.program_id(0) == 0
        last = pl.program_id(0) == grid[0] - 1
        for dim in range(1, len(grid)):
            first = first & (pl.program_id(dim) == 0)
            last = last & (pl.program_id(dim) == grid[dim] - 1)

        @pl.when(first)
        def _():
            for cp in comm.emit(src_refs, new_refs, alias_refs, (send_sems, recv_sems, 0)):
                cp.start()

        body(*pre, *ins, *outs, *scr)

        @pl.when(last)
        def _():
            for cp in comm.emit(src_refs, new_refs, alias_refs, (send_sems, recv_sems, 0)):
                cp.wait()

    call = make(wrapped, in_specs + [ANY] * (n_src + n_al), out_specs + [ANY] * (n_new + n_al),
                out_shape + comm.new + [jax.ShapeDtypeStruct(b.shape, b.dtype) for b in comm.alias],
                scratch + [pltpu.SemaphoreType.DMA((comm.n_copies,)), pltpu.SemaphoreType.DMA((comm.n_copies,))],
                ("arbitrary",) * len(grid))
    res = list(call(*args, *comm.srcs, *comm.alias))
    return res[:n_out], res[n_out:]


def _comm_only(comm, *, name):
    def body(o_ref):
        o_ref[...] = jnp.zeros_like(o_ref)

    _, couts = _pcall(body, name=name, grid=(1,), in_specs=[], out_specs=[pl.BlockSpec((8, 128), lambda i: (0, 0))],
                      out_shape=[jax.ShapeDtypeStruct((8, 128), F32)], args=(), sem=("arbitrary",), comm=comm)
    return couts


def _divisor_tiles(dim, cap):
    tiles = [t for t in range(128, min(dim, cap) + 1, 128) if dim % t == 0]
    return sorted(tiles, reverse=True) or [dim]


def _mm_tiles(m, n, k, obytes, has_res, tn_fixed=None, tk_fixed=None):
    tms = _divisor_tiles(m, 1408)
    tns = [tn_fixed] if tn_fixed else _divisor_tiles(n, 1408)
    tks = [tk_fixed] if tk_fixed else [k] + [tt for tt in _divisor_tiles(k, 2048) if tt != k]
    best, best_score = None, None
    for tk in tks:
        for tm in tms:
            for tn in tns:
                nk = k // tk
                use = 4 * tm * tk + 4 * tk * tn + 2 * tm * tn * obytes
                use += (8 * tm * tn if has_res else 0) + (4 * tm * tn if nk > 1 else 0)
                score = (nk == 1, min(tm, 1024), tn, tm, tk)
                if use <= MM_VMEM_BUDGET and (best is None or score > best_score):
                    best, best_score = (tm, tn, tk), score
    assert best is not None, (m, n, k)
    return best


def _mm(a, b, *, ta=False, tb=False, out_dtype=BF16, res=None, name, b_cols=False, b_off=0,
        out_cols=None, out_alias=None, comm=None):
    m, k = (a.shape[1], a.shape[0]) if ta else a.shape
    obytes = jnp.dtype(out_dtype).itemsize
    if b_cols and not tb:
        ns = b.shape[2]
        assert b.shape[1] == k
        n = b.shape[0] * ns
        tm, tn, tk = _mm_tiles(m, n, k, obytes, res is not None, tn_fixed=ns)
        b_spec = pl.BlockSpec((None, tk, ns), lambda i, j, kk: (j, kk, 0))
    elif b_cols:
        ns = b.shape[2]
        assert k % ns == 0
        n = b.shape[1]
        tm, tn, tk = _mm_tiles(m, n, k, obytes, res is not None, tk_fixed=ns)
        b_spec = pl.BlockSpec((None, tn, ns), lambda i, j, kk: (b_off + kk, j, 0))
    else:
        n = b.shape[0] if tb else b.shape[1]
        assert (b.shape[1] >= b_off + k) if tb else (b.shape[0] == k and b_off == 0)
        tm, tn, tk = _mm_tiles(m, n, k, obytes, res is not None, tn_fixed=out_cols[2] if out_cols is not None else None)
        assert b_off % tk == 0
        k0 = b_off // tk
        b_spec = pl.BlockSpec((tn, tk), lambda i, j, kk: (j, k0 + kk)) if tb else pl.BlockSpec((tk, tn), lambda i, j, kk: (kk, j))
    nk = k // tk
    dn = (((0,) if ta else (1,), (1,) if tb else (0,)), ((), ()))
    n_extra = (res is not None) + (out_alias is not None)

    def body(*refs):
        a_ref, b_ref = refs[0], refs[1]
        r_ref = refs[2] if res is not None else None
        o_ref = refs[2 + n_extra]
        part = lax.dot_general(a_ref[...], b_ref[...], dn, preferred_element_type=F32)

        def finish(r):
            if r_ref is not None:
                r = r + r_ref[...]
            o_ref[...] = r.astype(out_dtype)

        if nk == 1:
            finish(part)
        else:
            acc_ref = refs[3 + n_extra]
            kk = pl.program_id(2)

            @pl.when(kk == 0)
            def _():
                acc_ref[...] = part

            @pl.when(kk > 0)
            def _():
                acc_ref[...] += part

            @pl.when(kk == nk - 1)
            def _():
                finish(acc_ref[...])

    a_spec = pl.BlockSpec((tk, tm), lambda i, j, kk: (kk, i)) if ta else pl.BlockSpec((tm, tk), lambda i, j, kk: (i, kk))
    in_specs, args, aliases = [a_spec, b_spec], [a, b], {}
    if res is not None:
        in_specs.append(pl.BlockSpec((tm, tn), lambda i, j, kk: (i, j)))
        args.append(res)
    if out_cols is not None:
        s_total, o_off, ns_o = out_cols
        assert tn == ns_o and n % ns_o == 0
        o_spec = pl.BlockSpec((None, tm, tn), lambda i, j, kk: (o_off + j, i, 0))
        o_shape = jax.ShapeDtypeStruct((s_total, m, tn), out_dtype)
        if out_alias is not None:
            in_specs.append(ANY)
            args.append(out_alias)
            aliases[len(args) - 1] = 0
    else:
        o_spec = pl.BlockSpec((tm, tn), lambda i, j, kk: (i, j))
        o_shape = jax.ShapeDtypeStruct((m, n), out_dtype)
    outs, couts = _pcall(body, name=name, grid=(m // tm, n // tn, nk), in_specs=in_specs, out_specs=[o_spec],
                         out_shape=[o_shape], args=args, sem=("parallel", "parallel", "arbitrary"),
                         scratch=[pltpu.VMEM((tm, tn), F32)] if nk > 1 else [], aliases=aliases, comm=comm)
    return (outs[0], couts) if comm is not None else outs[0]


def _rms_fwd(x, g, *, name):
    t, d = x.shape
    tr = _tile(t, 256)

    def body(x_ref, g_ref, h_ref):
        xf = x_ref[...]
        inv = lax.rsqrt(jnp.mean(xf * xf, axis=-1, keepdims=True) + RMS_EPS)
        h_ref[...] = ((xf * inv) * g_ref[...]).astype(BF16)

    row = pl.BlockSpec((tr, d), lambda i: (i, 0))
    return _pcall(body, name=name, grid=(t // tr,), in_specs=[row, pl.BlockSpec((1, d), lambda i: (0, 0))],
                  out_specs=[row], out_shape=[jax.ShapeDtypeStruct((t, d), BF16)], args=(x, g),
                  sem=("parallel",))[0][0]


def _rms_bwd(x, g, dh, dres, *, name):
    t, d = x.shape
    tr = _tile(t, 256)

    def body(x_ref, g_ref, dh_ref, dres_ref, dx_ref, dxb_ref, dg_ref):
        xf = x_ref[...]
        inv = lax.rsqrt(jnp.mean(xf * xf, axis=-1, keepdims=True) + RMS_EPS)
        xn = xf * inv
        dh_f = dh_ref[...].astype(F32)
        dxn = dh_f * g_ref[...]
        dx = dres_ref[...] + inv * (dxn - xn * jnp.mean(dxn * xn, axis=-1, keepdims=True))
        dx_ref[...] = dx
        dxb_ref[...] = dx.astype(BF16)
        part = jnp.sum(dh_f * xn, axis=0, keepdims=True)

        @pl.when(pl.program_id(0) == 0)
        def _():
            dg_ref[...] = part

        @pl.when(pl.program_id(0) > 0)
        def _():
            dg_ref[...] += part

    row = pl.BlockSpec((tr, d), lambda i: (i, 0))
    vec = pl.BlockSpec((1, d), lambda i: (0, 0))
    return _pcall(body, name=name, grid=(t // tr,), in_specs=[row, vec, row, row], out_specs=[row, row, vec],
                  out_shape=[jax.ShapeDtypeStruct((t, d), F32), jax.ShapeDtypeStruct((t, d), BF16),
                             jax.ShapeDtypeStruct((1, d), F32)], args=(x, g, dh, dres), sem=("arbitrary",))[0]


def _final_loss(x, g, target, *, name):
    t, d = x.shape
    tr = _tile(t, 256)

    def body(x_ref, g_ref, tg_ref, loss_ref, dx_ref, dxb_ref, dg_ref):
        xf = x_ref[...]
        gv = g_ref[...]
        inv = lax.rsqrt(jnp.mean(xf * xf, axis=-1, keepdims=True) + RMS_EPS)
        xn = xf * inv
        err = xn * gv - tg_ref[...]
        lpart = 0.5 * jnp.sum(jnp.mean(err * err, axis=-1, keepdims=True), axis=0, keepdims=True)
        dy = err * (1.0 / d)
        dxn = dy * gv
        dx = inv * (dxn - xn * jnp.mean(dxn * xn, axis=-1, keepdims=True))
        dx_ref[...] = dx
        dxb_ref[...] = dx.astype(BF16)
        gpart = jnp.sum(dy * xn, axis=0, keepdims=True)
        lrow = jnp.broadcast_to(lpart, (1, 128))

        @pl.when(pl.program_id(0) == 0)
        def _():
            dg_ref[...] = gpart
            loss_ref[...] = lrow

        @pl.when(pl.program_id(0) > 0)
        def _():
            dg_ref[...] += gpart
            loss_ref[...] += lrow

    row = pl.BlockSpec((tr, d), lambda i: (i, 0))
    vec = pl.BlockSpec((1, d), lambda i: (0, 0))
    lspec = pl.BlockSpec((1, 128), lambda i: (0, 0))
    return _pcall(body, name=name, grid=(t // tr,), in_specs=[row, vec, row], out_specs=[lspec, row, row, vec],
                  out_shape=[jax.ShapeDtypeStruct((1, 128), F32), jax.ShapeDtypeStruct((t, d), F32),
                             jax.ShapeDtypeStruct((t, d), BF16), jax.ShapeDtypeStruct((1, d), F32)],
                  args=(x, g, target), sem=("arbitrary",))[0]


def _tri_ones(n, upper):
    r = lax.broadcasted_iota(jnp.int32, (n, n), 0)
    c = lax.broadcasted_iota(jnp.int32, (n, n), 1)
    return jnp.where((r <= c) if upper else (r >= c), 1.0, 0.0).astype(F32)


def _forget_fwd(flog, bpad, *, name):
    t = flog.shape[0]
    tb = _tile(t, 512)

    def body(f_ref, b_ref, c_ref, carry):
        z = f_ref[...] + b_ref[...]
        lf = jnp.minimum(z, 0.0) - jnp.log(1.0 + jnp.exp(-jnp.abs(z)))
        lft = lf.T
        tri = _tri_ones(tb, upper=True)

        @pl.when(pl.program_id(0) == 0)
        def _():
            carry[...] = jnp.zeros_like(carry)

        cs = jnp.dot(lft, tri, preferred_element_type=F32, precision=lax.Precision.HIGHEST) + carry[:, 0:1]
        c_ref[...] = cs
        carry[...] = jnp.broadcast_to(cs[:, tb - 1:tb], carry.shape)

    return _pcall(body, name=name, grid=(t // tb,),
                  in_specs=[pl.BlockSpec((tb, 128), lambda i: (i, 0)), pl.BlockSpec((1, 128), lambda i: (0, 0))],
                  out_specs=[pl.BlockSpec((128, tb), lambda i: (0, i))],
                  out_shape=[jax.ShapeDtypeStruct((128, t), F32)], args=(flog, bpad), sem=("arbitrary",),
                  scratch=[pltpu.VMEM((128, 128), F32)])[0][0]


def _forget_bwd(flog, bpad, dct, *, name):
    t = flog.shape[0]
    tb = _tile(t, 512)
    nb = t // tb

    def body(f_ref, b_ref, dc_ref, df_ref, db_ref, carry):
        i = pl.program_id(0)

        @pl.when(i == 0)
        def _():
            carry[...] = jnp.zeros_like(carry)

        tri = _tri_ones(tb, upper=False)
        dl = jnp.dot(dc_ref[...], tri, preferred_element_type=F32, precision=lax.Precision.HIGHEST) + carry[:, 0:1]
        carry[...] = jnp.broadcast_to(dl[:, 0:1], carry.shape)
        z = f_ref[...] + b_ref[...]
        df = dl.T * _sigmoid(-z)
        df_ref[...] = df.astype(BF16)
        part = jnp.sum(df, axis=0, keepdims=True)

        @pl.when(i == 0)
        def _():
            db_ref[...] = part

        @pl.when(i > 0)
        def _():
            db_ref[...] += part

    rev = lambda i: (nb - 1 - i, 0)
    return _pcall(body, name=name, grid=(nb,),
                  in_specs=[pl.BlockSpec((tb, 128), rev), pl.BlockSpec((1, 128), lambda i: (0, 0)),
                            pl.BlockSpec((128, tb), lambda i: (0, nb - 1 - i))],
                  out_specs=[pl.BlockSpec((tb, 128), rev), pl.BlockSpec((1, 128), lambda i: (0, 0))],
                  out_shape=[jax.ShapeDtypeStruct((t, 128), BF16), jax.ShapeDtypeStruct((1, 128), F32)],
                  args=(flog, bpad, dct), sem=("arbitrary",), scratch=[pltpu.VMEM((128, 128), F32)])[0]


def _causal_pairs(nq, k_major):
    if k_major:
        pairs = [(i, j) for j in range(nq) for i in range(j, nq)]
    else:
        pairs = [(i, j) for i in range(nq) for j in range(i + 1)]
    return (jnp.asarray(np.array([p[0] for p in pairs], np.int32)),
            jnp.asarray(np.array([p[1] for p in pairs], np.int32)), len(pairs))


def _logits(q, k, cq, ck, scale, masked):
    s = lax.dot_general(q, k, (((1,), (1,)), ((), ())), preferred_element_type=F32)
    s = s * scale + (cq[:, 0:1] - ck)
    if masked:
        row = lax.broadcasted_iota(jnp.int32, s.shape, 0)
        col = lax.broadcasted_iota(jnp.int32, s.shape, 1)
        s = jnp.where(col <= row, s, NEG)
    return s


def _head_group(heads):
    return 4 if heads % 4 == 0 else (2 if heads % 2 == 0 else 1)


def _fox_fwd(qkv, c3, heads, *, name, comm=None):
    t = qkv.shape[0]
    tq = _tile(t, 512)
    nq = t // tq
    hb = _head_group(heads)
    ng = heads // hb
    scale = HEAD_DIM ** -0.5
    i_tab, j_tab, npairs = _causal_pairs(nq, k_major=False)

    def body(it_ref, jt_ref, q_ref, k_ref, v_ref, cq_ref, ck_ref, o_ref, o32_ref, lse_ref, m_s, l_s, acc_s):
        p_id = pl.program_id(1)
        i, j = it_ref[p_id], jt_ref[p_id]

        @pl.when(j == 0)
        def _():
            m_s[...] = jnp.full_like(m_s, NEG)
            l_s[...] = jnp.zeros_like(l_s)
            acc_s[...] = jnp.zeros_like(acc_s)

        def update(masked):
            for hh in range(hb):
                ls = slice(hh * 128, (hh + 1) * 128)
                s = _logits(q_ref[:, ls], k_ref[:, ls], cq_ref[hh], ck_ref[hh], scale, masked)
                m_prev = m_s[hh, :, 0:1]
                m_new = jnp.maximum(m_prev, jnp.max(s, axis=1, keepdims=True))
                alpha = jnp.exp(m_prev - m_new)
                p = jnp.exp(s - m_new)
                l_s[hh, :, 0:1] = alpha * l_s[hh, :, 0:1] + jnp.sum(p, axis=1, keepdims=True)
                p_hi = p.astype(BF16)
                p_lo = (p - p_hi.astype(F32)).astype(BF16)
                vb = v_ref[:, ls]
                pv = jnp.dot(p_hi, vb, preferred_element_type=F32) + jnp.dot(p_lo, vb, preferred_element_type=F32)
                acc_s[hh] = alpha * acc_s[hh] + pv
                m_s[hh, :, 0:1] = m_new

        @pl.when(j < i)
        def _():
            update(False)

        @pl.when(j == i)
        def _():
            update(True)
            for hh in range(hb):
                ls = slice(hh * 128, (hh + 1) * 128)
                l = l_s[hh, :, 0:1]
                o = acc_s[hh] / l
                o_ref[:, ls] = o.astype(BF16)
                o32_ref[:, ls] = o
                lse_ref[hh] = jnp.broadcast_to(m_s[hh, :, 0:1] + jnp.log(l), (tq, 128))

    w = hb * 128
    qb = lambda g, p, it, jt: (it[p], g)
    outs, couts = _pcall(
        body, name=name, grid=(ng, npairs), prefetch=2,
        in_specs=[pl.BlockSpec((tq, w), qb),
                  pl.BlockSpec((tq, w), lambda g, p, it, jt: (jt[p], ng + g)),
                  pl.BlockSpec((tq, w), lambda g, p, it, jt: (jt[p], 2 * ng + g)),
                  pl.BlockSpec((hb, 1, tq), lambda g, p, it, jt: (g, 0, it[p])),
                  pl.BlockSpec((hb, 1, tq), lambda g, p, it, jt: (g, 0, jt[p]))],
        out_specs=[pl.BlockSpec((tq, w), qb), pl.BlockSpec((tq, w), qb),
                   pl.BlockSpec((hb, tq, 128), lambda g, p, it, jt: (g, it[p], 0))],
        out_shape=[jax.ShapeDtypeStruct((t, heads * 128), BF16), jax.ShapeDtypeStruct((t, heads * 128), F32),
                   jax.ShapeDtypeStruct((heads, t, 128), F32)],
        args=(i_tab, j_tab, qkv, qkv, qkv, c3, c3), sem=("parallel", "arbitrary"),
        scratch=[pltpu.VMEM((hb, tq, 128), F32), pltpu.VMEM((hb, tq, 128), F32), pltpu.VMEM((hb, tq, 128), F32)],
        comm=comm)
    return outs, couts


def _fox_bwd(qkv, c3, o, do, lse, heads, *, name, comm=None):
    t = qkv.shape[0]
    tq = _tile(t, 512)
    nq = t // tq
    hb = _head_group(heads)
    ng = heads // hb
    scale = HEAD_DIM ** -0.5
    i_tab, j_tab, npairs = _causal_pairs(nq, k_major=True)

    def body(it_ref, jt_ref, q_ref, k_ref, v_ref, o_ref, do_ref, lse_ref, cq_ref, ck_ref,
             dq_ref, dk_ref, dv_ref, dc_ref, dq_s, dk_s, dv_s, dc_s):
        p_id = pl.program_id(1)
        i, j = it_ref[p_id], jt_ref[p_id]

        @pl.when(p_id == 0)
        def _():
            dq_s[...] = jnp.zeros_like(dq_s)

        @pl.when(i == j)
        def _():
            dk_s[...] = jnp.zeros_like(dk_s)
            dv_s[...] = jnp.zeros_like(dv_s)
            dc_s[...] = jnp.zeros_like(dc_s)

        def update(masked):
            r0 = pl.multiple_of(i * tq, tq)
            for hh in range(hb):
                ls = slice(hh * 128, (hh + 1) * 128)
                q, k, v, dob = q_ref[:, ls], k_ref[:, ls], v_ref[:, ls], do_ref[:, ls]
                s = _logits(q, k, cq_ref[hh], ck_ref[hh], scale, masked)
                p = jnp.exp(s - lse_ref[hh, :, 0:1])
                delta = jnp.sum(dob.astype(F32) * o_ref[:, ls], axis=1, keepdims=True)
                dp = lax.dot_general(dob, v, (((1,), (1,)), ((), ())), preferred_element_type=F32)
                ds = p * (dp - delta)
                pb, dsb = p.astype(BF16), ds.astype(BF16)
                dv_s[hh] += lax.dot_general(pb, dob, (((0,), (0,)), ((), ())), preferred_element_type=F32)
                dk_s[hh] += lax.dot_general(dsb, q, (((0,), (0,)), ((), ())), preferred_element_type=F32)
                dq_s[hh, pl.ds(r0, tq), :] += jnp.dot(dsb, k, preferred_element_type=F32) * scale
                dc_s[hh] -= jnp.sum(ds, axis=0, keepdims=True)

        @pl.when(i > j)
        def _():
            update(False)

        @pl.when(i == j)
        def _():
            update(True)

        @pl.when(i == nq - 1)
        def _():
            for hh in range(hb):
                ls = slice(hh * 128, (hh + 1) * 128)
                dk_ref[:, ls] = (dk_s[hh] * scale).astype(BF16)
                dv_ref[:, ls] = dv_s[hh].astype(BF16)
            dc_ref[...] = dc_s[...]

        @pl.when(p_id == npairs - 1)
        def _():
            for hh in range(hb):
                dq_ref[:, hh * 128:(hh + 1) * 128] = dq_s[hh].astype(BF16)

    w = hb * 128
    qb = lambda g, p, it, jt: (it[p], g)
    kb = lambda g, p, it, jt: (jt[p], g)
    outs, couts = _pcall(
        body, name=name, grid=(ng, npairs), prefetch=2,
        in_specs=[pl.BlockSpec((tq, w), qb),
                  pl.BlockSpec((tq, w), lambda g, p, it, jt: (jt[p], ng + g)),
                  pl.BlockSpec((tq, w), lambda g, p, it, jt: (jt[p], 2 * ng + g)),
                  pl.BlockSpec((tq, w), qb), pl.BlockSpec((tq, w), qb),
                  pl.BlockSpec((hb, tq, 128), lambda g, p, it, jt: (g, it[p], 0)),
                  pl.BlockSpec((hb, 1, tq), lambda g, p, it, jt: (g, 0, it[p])),
                  pl.BlockSpec((hb, 1, tq), lambda g, p, it, jt: (g, 0, jt[p]))],
        out_specs=[pl.BlockSpec((t, w), lambda g, p, it, jt: (0, g)), pl.BlockSpec((tq, w), kb),
                   pl.BlockSpec((tq, w), kb), pl.BlockSpec((hb, 1, tq), lambda g, p, it, jt: (g, 0, jt[p]))],
        out_shape=[jax.ShapeDtypeStruct((t, heads * 128), BF16), jax.ShapeDtypeStruct((t, heads * 128), BF16),
                   jax.ShapeDtypeStruct((t, heads * 128), BF16), jax.ShapeDtypeStruct((heads, 1, t), F32)],
        args=(i_tab, j_tab, qkv, qkv, qkv, o, do, lse, c3, c3), sem=("arbitrary", "arbitrary"),
        scratch=[pltpu.VMEM((hb, t, 128), F32), pltpu.VMEM((hb, tq, 128), F32), pltpu.VMEM((hb, tq, 128), F32),
                 pltpu.VMEM((hb, 1, tq), F32)], comm=comm)
    return outs, couts


def _tril_mask():
    r = lax.broadcasted_iota(jnp.int32, (128, 128), 0)
    c = lax.broadcasted_iota(jnp.int32, (128, 128), 1)
    return r >= c


def _sgu_fwd(uv, g, w, bst, *, name):
    t = uv.shape[0]
    sw = uv.shape[1] // 2
    groups = sw // 128
    tr = _tile(t, 512)

    def body(u_ref, v_ref, g_ref, w_ref, b_ref, y_ref):
        gv = _gelu(v_ref[...].astype(F32))
        inv = lax.rsqrt(jnp.mean(gv * gv, axis=-1, keepdims=True) + RMS_EPS)
        vn = ((gv * inv) * g_ref[...]).astype(BF16)
        gu = _gelu(u_ref[...].astype(F32))
        mask = _tril_mask()
        for gi in range(groups):
            wg = jnp.where(mask, w_ref[gi], 0.0).astype(BF16)
            bcol = b_ref[:, gi:gi + 1]
            cs = slice(gi * 128, (gi + 1) * 128)
            for ci in range(tr // 128):
                rs = slice(ci * 128, (ci + 1) * 128)
                mixed = jnp.dot(wg, vn[rs, cs], preferred_element_type=F32) + bcol
                y_ref[rs, cs] = (gu[rs, cs] * mixed).astype(BF16)

    return _pcall(body, name=name, grid=(t // tr,),
                  in_specs=[pl.BlockSpec((tr, sw), lambda i: (i, 0)), pl.BlockSpec((tr, sw), lambda i: (i, 1)),
                            pl.BlockSpec((1, sw), lambda i: (0, 0)),
                            pl.BlockSpec((groups, 128, 128), lambda i: (0, 0, 0)),
                            pl.BlockSpec((128, 128), lambda i: (0, 0))],
                  out_specs=[pl.BlockSpec((tr, sw), lambda i: (i, 0))],
                  out_shape=[jax.ShapeDtypeStruct((t, sw), BF16)], args=(uv, uv, g, w, bst),
                  sem=("parallel",))[0][0]


def _sgu_bwd(uv, g, w, bst, dy, *, name):
    t = uv.shape[0]
    sw = uv.shape[1] // 2
    groups = sw // 128
    tr = _tile(t, 256)
    nsteps = t // tr

    def body(u_ref, v_ref, g_ref, w_ref, b_ref, dy_ref, duv_ref, dw_ref, db_ref, dg_ref, dvn_s, dgu_s):
        step = pl.program_id(0)

        @pl.when(step == 0)
        def _():
            dw_ref[...] = jnp.zeros_like(dw_ref)
            db_ref[...] = jnp.zeros_like(db_ref)
            dg_ref[...] = jnp.zeros_like(dg_ref)

        vf = v_ref[...].astype(F32)
        gv, gv_grad = _gelu_and_grad(vf)
        inv = lax.rsqrt(jnp.mean(gv * gv, axis=-1, keepdims=True) + RMS_EPS)
        xn = gv * inv
        gvec = g_ref[...]
        vn = (xn * gvec).astype(BF16)
        uf = u_ref[...].astype(F32)
        gu, gu_grad = _gelu_and_grad(uf)
        dyf = dy_ref[...].astype(F32)
        mask = _tril_mask()
        lane = lax.broadcasted_iota(jnp.int32, (128, 128), 1)
        dball = jnp.zeros((128, 128), F32)
        for gi in range(groups):
            wg = jnp.where(mask, w_ref[gi], 0.0).astype(BF16)
            wgt = wg.T
            bcol = b_ref[:, gi:gi + 1]
            cs = slice(gi * 128, (gi + 1) * 128)
            dwg = jnp.zeros((128, 128), F32)
            dbg = jnp.zeros((128, 1), F32)
            for ci in range(tr // 128):
                rs = slice(ci * 128, (ci + 1) * 128)
                vnb = vn[rs, cs]
                mixed = jnp.dot(wg, vnb, preferred_element_type=F32) + bcol
                dgu_s[rs, cs] = dyf[rs, cs] * mixed
                dmix = dyf[rs, cs] * gu[rs, cs]
                dmb = dmix.astype(BF16)
                dvn_s[rs, cs] = jnp.dot(wgt, dmb, preferred_element_type=F32)
                dwg = dwg + lax.dot_general(dmb, vnb, (((1,), (1,)), ((), ())), preferred_element_type=F32)
                dbg = dbg + jnp.sum(dmix, axis=1, keepdims=True)
            dw_ref[gi] += dwg
            dball = dball + jnp.where(lane == gi, dbg, 0.0)
        db_ref[...] += dball
        dvn = dvn_s[...]
        dg_ref[...] += jnp.sum(dvn * xn, axis=0, keepdims=True)
        dxn = dvn * gvec
        dgv = inv * (dxn - xn * jnp.mean(dxn * xn, axis=-1, keepdims=True))
        duv_ref[:, 0:sw] = (dgu_s[...] * gu_grad).astype(BF16)
        duv_ref[:, sw:2 * sw] = (dgv * gv_grad).astype(BF16)

        @pl.when(step == nsteps - 1)
        def _():
            for gi in range(groups):
                dw_ref[gi] = jnp.where(mask, dw_ref[gi], 0.0)

    return _pcall(body, name=name, grid=(nsteps,),
                  in_specs=[pl.BlockSpec((tr, sw), lambda i: (i, 0)), pl.BlockSpec((tr, sw), lambda i: (i, 1)),
                            pl.BlockSpec((1, sw), lambda i: (0, 0)),
                            pl.BlockSpec((groups, 128, 128), lambda i: (0, 0, 0)),
                            pl.BlockSpec((128, 128), lambda i: (0, 0)), pl.BlockSpec((tr, sw), lambda i: (i, 0))],
                  out_specs=[pl.BlockSpec((tr, 2 * sw), lambda i: (i, 0)),
                             pl.BlockSpec((groups, 128, 128), lambda i: (0, 0, 0)),
                             pl.BlockSpec((128, 128), lambda i: (0, 0)), pl.BlockSpec((1, sw), lambda i: (0, 0))],
                  out_shape=[jax.ShapeDtypeStruct((t, 2 * sw), BF16), jax.ShapeDtypeStruct((groups, 128, 128), F32),
                             jax.ShapeDtypeStruct((128, 128), F32), jax.ShapeDtypeStruct((1, sw), F32)],
                  args=(uv, uv, g, w, bst, dy), sem=("arbitrary",),
                  scratch=[pltpu.VMEM((tr, sw), F32), pltpu.VMEM((tr, sw), F32)])[0]


def _merge_fwd(ya, yb, wa, wb, gates, *, name, comm=None):
    t, kdim = ya.shape
    nsh, _, ns = wa.shape
    d = nsh * ns
    tm = _tile(t, 1024)

    def body(ya_ref, yb_ref, wa_ref, wb_ref, ga_ref, gb_ref, mg_ref, za_ref, zb_ref):
        za = jnp.dot(ya_ref[...], wa_ref[...], preferred_element_type=F32)
        zb = jnp.dot(yb_ref[...], wb_ref[...], preferred_element_type=F32)
        sa = _sigmoid(ga_ref[...].astype(F32))
        sb = _sigmoid(gb_ref[...].astype(F32))
        mg_ref[...] = (sa * za + sb * zb).astype(BF16)
        za_ref[...] = za.astype(BF16)
        zb_ref[...] = zb.astype(BF16)

    yspec = pl.BlockSpec((tm, kdim), lambda i, j: (i, 0))
    wspec = pl.BlockSpec((None, kdim, ns), lambda i, j: (j, 0, 0))
    ospec = pl.BlockSpec((tm, ns), lambda i, j: (i, j))
    outs, couts = _pcall(body, name=name, grid=(t // tm, nsh),
                         in_specs=[yspec, yspec, wspec, wspec, ospec, pl.BlockSpec((tm, ns), lambda i, j: (i, nsh + j))],
                         out_specs=[ospec, ospec, ospec], out_shape=[jax.ShapeDtypeStruct((t, d), BF16)] * 3,
                         args=(ya, yb, wa, wb, gates, gates), sem=("parallel", "parallel"), comm=comm)
    return outs, couts


def _merge_bwd(dmg, gates, za, zb, *, name):
    t, d = dmg.shape
    tr = _tile(t, 256)

    def body(dm_ref, ga_ref, gb_ref, za_ref, zb_ref, dza_ref, dzb_ref, dg_ref):
        dm = dm_ref[...].astype(F32)
        sa = _sigmoid(ga_ref[...].astype(F32))
        sb = _sigmoid(gb_ref[...].astype(F32))
        dza_ref[...] = (dm * sa).astype(BF16)
        dzb_ref[...] = (dm * sb).astype(BF16)
        dg_ref[:, 0:d] = (dm * za_ref[...].astype(F32) * (sa * (1.0 - sa))).astype(BF16)
        dg_ref[:, d:2 * d] = (dm * zb_ref[...].astype(F32) * (sb * (1.0 - sb))).astype(BF16)

    row = pl.BlockSpec((tr, d), lambda i: (i, 0))
    return _pcall(body, name=name, grid=(t // tr,),
                  in_specs=[row, row, pl.BlockSpec((tr, d), lambda i: (i, 1)), row, row],
                  out_specs=[row, row, pl.BlockSpec((tr, 2 * d), lambda i: (i, 0))],
                  out_shape=[jax.ShapeDtypeStruct((t, d), BF16), jax.ShapeDtypeStruct((t, d), BF16),
                             jax.ShapeDtypeStruct((t, 2 * d), BF16)],
                  args=(dmg, gates, gates, za, zb), sem=("parallel",))[0]


def _shift_down(ext, k, rows):
    return pltpu.roll(ext, k, 0)[8:8 + rows]


def _conv_fwd(ab, cw, cb, *, name):
    t = ab.shape[0]
    dff = ab.shape[1] // 2
    tr, tc = _tile(t, 512), _tile(dff, 512)
    nc = dff // tc
    r8 = tr // 8

    def body(a_ref, ap_ref, b_ref, cw_ref, cb_ref, g_ref, g1_ref, g2_ref):
        i = pl.program_id(0)
        prev = ap_ref[...].astype(F32) * jnp.where(i > 0, 1.0, 0.0)
        a = a_ref[...].astype(F32)
        ext = jnp.concatenate([prev, a], axis=0)
        acc = cb_ref[...] + cw_ref[0:1, :] * _shift_down(ext, 2, tr) + cw_ref[1:2, :] * _shift_down(ext, 1, tr) \
            + cw_ref[2:3, :] * a
        gel, gel_grad = _gelu_and_grad(acc)
        bf = b_ref[...].astype(F32)
        g_ref[...] = (gel * bf).astype(BF16)
        g1_ref[...] = gel.astype(BF16)
        g2_ref[...] = (bf * gel_grad).astype(BF16)

    ospec = pl.BlockSpec((tr, tc), lambda i, j: (i, j))
    return _pcall(body, name=name, grid=(t // tr, nc),
                  in_specs=[ospec, pl.BlockSpec((8, tc), lambda i, j: (jnp.maximum(i * r8 - 1, 0), j)),
                            pl.BlockSpec((tr, tc), lambda i, j: (i, nc + j)),
                            pl.BlockSpec((CONV_WIDTH, tc), lambda i, j: (0, j)),
                            pl.BlockSpec((1, tc), lambda i, j: (0, j))],
                  out_specs=[ospec, ospec, ospec],
                  out_shape=[jax.ShapeDtypeStruct((t, dff), BF16)] * 3, args=(ab, ab, ab, cw, cb),
                  sem=("parallel", "parallel"))[0]


def _conv_bwd(ab, g1, g2, cw, dgg, *, name, comm=None):
    t = ab.shape[0]
    dff = ab.shape[1] // 2
    tr, tc = _tile(t, 512), _tile(dff, 512)
    nc, nr = dff // tc, t // tr
    r8 = tr // 8
    ext_rows = tr + 8

    def body(a_ref, ap_ref, g1_ref, g2_ref, g2n_ref, dg_ref, dgn_ref, cw_ref, da_ref, db_ref, dcw_ref, dcb_ref):
        i = pl.program_id(1)
        has_prev = jnp.where(i > 0, 1.0, 0.0)
        has_next = jnp.where(i < nr - 1, 1.0, 0.0)
        dg = dg_ref[...].astype(F32)
        dacc = jnp.concatenate([dg * g2_ref[...].astype(F32),
                                dgn_ref[...].astype(F32) * g2n_ref[...].astype(F32) * has_next], axis=0)
        w0, w1, w2 = cw_ref[0:1, :], cw_ref[1:2, :], cw_ref[2:3, :]
        d_a = w2 * dacc + w1 * pltpu.roll(dacc, ext_rows - 1, 0) + w0 * pltpu.roll(dacc, ext_rows - 2, 0)
        da_ref[...] = d_a[0:tr].astype(BF16)
        db_ref[...] = (dg * g1_ref[...].astype(F32)).astype(BF16)
        dm = dacc[0:tr]
        a = a_ref[...].astype(F32)
        a_ext = jnp.concatenate([ap_ref[...].astype(F32) * has_prev, a], axis=0)
        dcw = jnp.concatenate([jnp.sum(dm * _shift_down(a_ext, 2, tr), axis=0, keepdims=True),
                               jnp.sum(dm * _shift_down(a_ext, 1, tr), axis=0, keepdims=True),
                               jnp.sum(dm * a, axis=0, keepdims=True)], axis=0)
        dcb = jnp.sum(dm, axis=0, keepdims=True)

        @pl.when(i == 0)
        def _():
            dcw_ref[...] = dcw
            dcb_ref[...] = dcb

        @pl.when(i > 0)
        def _():
            dcw_ref[...] += dcw
            dcb_ref[...] += dcb

    cur = lambda off: pl.BlockSpec((tr, tc), lambda j, i: (i, off + j))
    prv = lambda off: pl.BlockSpec((8, tc), lambda j, i: (jnp.maximum(i * r8 - 1, 0), off + j))
    nxt = lambda off: pl.BlockSpec((8, tc), lambda j, i: (jnp.minimum((i + 1) * r8, nr * r8 - 1), off + j))
    return _pcall(body, name=name, grid=(nc, nr),
                  in_specs=[cur(0), prv(0), cur(0), cur(0), nxt(0), cur(0), nxt(0),
                            pl.BlockSpec((CONV_WIDTH, tc), lambda j, i: (0, j))],
                  out_specs=[cur(0), cur(0), pl.BlockSpec((CONV_WIDTH, tc), lambda j, i: (0, j)),
                             pl.BlockSpec((1, tc), lambda j, i: (0, j))],
                  out_shape=[jax.ShapeDtypeStruct((t, dff), BF16), jax.ShapeDtypeStruct((t, dff), BF16),
                             jax.ShapeDtypeStruct((CONV_WIDTH, dff), F32), jax.ShapeDtypeStruct((1, dff), F32)],
                  args=(ab, ab, g1, g2, g2, dgg, dgg, cw), sem=("parallel", "arbitrary"), comm=comm)


class _InLayout:
    def __init__(self, nsi, heads, fw, sw, d):
        self.nsi = nsi
        self.pw = -(-(nsi + 127) // 128) * 128
        self.o_u = 3 * fw + heads
        self.ins = 128 - heads
        self.widths = [3 * fw, 128, 2 * sw, 2 * d]
        self.total = sum(self.widths)

    def regions(self, s):
        g0 = self.nsi * s
        out = []
        lo, hi = g0, min(g0 + self.nsi, self.o_u)
        if lo < hi:
            out.append((lo, hi, g0 % 128, g0 // 128, 0, hi - g0))
        lo, hi = max(g0, self.o_u), g0 + self.nsi
        if lo < hi:
            gi = g0 + self.ins
            out.append((lo + self.ins, hi + self.ins, gi % 128, gi // 128, lo - g0, self.nsi))
        return out


def _lane_mix(lane, pieces):
    val = None
    for piece, lo, hi in pieces:
        if lo <= 0 and hi >= 128:
            val = piece
        else:
            val = jnp.where((lane >= lo) & (lane < hi), piece, jnp.zeros_like(piece) if val is None else val)
    return val


def _w_in_unpack(g, lay, *, name):
    _, d, pw = g.shape
    tr = _tile(d, 128)
    ntiles = lay.total // 128
    plan = [[] for _ in range(ntiles)]
    for s in range(N_DEV):
        for r, (lo, hi, _, tile0, _, _) in enumerate(lay.regions(s)):
            for tt in range(lo // 128, (hi - 1) // 128 + 1):
                plan[tt].append((s, r, tt - tile0, lo - 128 * tt, hi - 128 * tt))
    bounds = np.cumsum([0] + [w // 128 for w in lay.widths])

    def body(g_ref, *o_refs):
        lane = lax.broadcasted_iota(jnp.int32, (tr, 128), 1)
        rolled = {}

        def src(s, r):
            if (s, r) not in rolled:
                shift = lay.regions(s)[r][2]
                xs = g_ref[s].astype(F32)
                rolled[(s, r)] = pltpu.roll(xs, shift, 1) if shift else xs
            return rolled[(s, r)]

        for tt in range(ntiles):
            val = _lane_mix(lane, [(src(s, r)[:, 128 * st:128 * (st + 1)], lo, hi) for s, r, st, lo, hi in plan[tt]])
            o = int(np.searchsorted(bounds, tt, side="right")) - 1
            lt = tt - int(bounds[o])
            o_refs[o][:, 128 * lt:128 * (lt + 1)] = val.astype(BF16)

    return _pcall(body, name=name, grid=(d // tr,), in_specs=[pl.BlockSpec((N_DEV, tr, pw), lambda i: (0, i, 0))],
                  out_specs=[pl.BlockSpec((tr, w), lambda i: (i, 0)) for w in lay.widths],
                  out_shape=[jax.ShapeDtypeStruct((d, w), BF16) for w in lay.widths], args=(g,),
                  sem=("parallel",))[0]


def _w_in_grad_pack(pieces, lay, *, name):
    d = pieces[0].shape[0]
    tr = _tile(d, 128)
    assert sum(p.shape[1] for p in pieces) == lay.total
    starts = np.cumsum([0] + [p.shape[1] // 128 for p in pieces])
    ntot = lay.total // 128
    nfull, rem = lay.nsi // 128, lay.nsi % 128
    npc = len(pieces)

    def body(*refs):
        p_refs, o_ref = refs[:npc], refs[npc]
        lane = lax.broadcasted_iota(jnp.int32, (tr, 128), 1)

        def padded_tile(tau):
            pi = int(np.searchsorted(starts, tau, side="right")) - 1
            lt = tau - int(starts[pi])
            return p_refs[pi][:, 128 * lt:128 * (lt + 1)].astype(F32)

        for s in range(N_DEV):
            unrolled = []
            for _, _, shift, tile0, j_lo, j_hi in lay.regions(s):
                win = jnp.concatenate([padded_tile(min(tile0 + q, ntot - 1)) for q in range(lay.pw // 128)], axis=1)
                unrolled.append((pltpu.roll(win, lay.pw - shift, 1) if shift else win, j_lo, j_hi))
            for kt in range(nfull + (1 if rem else 0)):
                val = _lane_mix(lane, [(u[:, 128 * kt:128 * (kt + 1)], j_lo - 128 * kt, j_hi - 128 * kt)
                                       for u, j_lo, j_hi in unrolled if j_lo < 128 * (kt + 1) and j_hi > 128 * kt])
                if kt < nfull:
                    o_ref[s, :, 128 * kt:128 * (kt + 1)] = val.astype(BF16)
                else:
                    o_ref[s, :, 128 * nfull:lay.nsi] = val[:, :rem].astype(BF16)

    return _pcall(body, name=name, grid=(d // tr,),
                  in_specs=[pl.BlockSpec((tr, p.shape[1]), lambda i: (i, 0)) for p in pieces],
                  out_specs=[pl.BlockSpec((N_DEV, tr, lay.nsi), lambda i: (0, i, 0))],
                  out_shape=[jax.ShapeDtypeStruct((N_DEV, d, lay.nsi), BF16)], args=tuple(pieces),
                  sem=("parallel",))[0][0]


def _row_tile(rows, cols):
    for cand in (512, 256, 128, 64, 32, 16, 8):
        if rows % cand == 0 and cand * cols * 4 <= 2 * 1024 * 1024:
            return cand
    return rows


def _pair_add(grad8, recv4, cidx, *, name):
    _, rows, cols = grad8.shape
    tr = _row_tile(rows, cols)

    def body(c_ref, g_ref, r_ref, o_ref):
        o_ref[...] = (g_ref[...].astype(F32) + r_ref[...].astype(F32)).astype(BF16)

    blk = (None, tr, cols)
    return _pcall(body, name=name, grid=(4, rows // tr), prefetch=1,
                  in_specs=[pl.BlockSpec(blk, lambda k, i, c_ref: (2 * k + c_ref[0], i, 0)),
                            pl.BlockSpec(blk, lambda k, i, c_ref: (k, i, 0))],
                  out_specs=[pl.BlockSpec(blk, lambda k, i, c_ref: (k, i, 0))],
                  out_shape=[jax.ShapeDtypeStruct((4, rows, cols), BF16)], args=(cidx, grad8, recv4),
                  sem=("parallel", "parallel"))[0][0]


def _adamw_math(w, g, m, v):
    m = ADAM_B1 * m + (1.0 - ADAM_B1) * g
    v = ADAM_B2 * v + (1.0 - ADAM_B2) * (g * g)
    m_hat = m / (1.0 - ADAM_B1 ** ADAM_STEP)
    v_hat = v / (1.0 - ADAM_B2 ** ADAM_STEP)
    delta = -ADAM_LR * (m_hat / (jnp.sqrt(v_hat) + ADAM_EPS) + ADAM_WD * w)
    return delta, m, v


def _shard_adamw(part4, recv3, w, m, v, kidx, layer, bufs, *, name):
    depth, rows, cols = w.shape
    tr = _row_tile(rows, cols)

    def body(k_ref, p_ref, r0_ref, r1_ref, r2_ref, w_ref, m_ref, v_ref, *rest):
        g_out, d_out, m_out, v_out = rest[-4:]
        g = ((p_ref[...].astype(F32) + r0_ref[...].astype(F32)) + r1_ref[...].astype(F32)) + r2_ref[...].astype(F32)
        delta, mn, vn = _adamw_math(w_ref[...], g, m_ref[...], v_ref[...])
        g_out[...] = g
        d_out[...] = delta
        m_out[...] = mn
        v_out[...] = vn

    blk = (None, tr, cols)
    rspec = lambda j: pl.BlockSpec(blk, lambda i, k_ref: (j, i, 0))
    espec = pl.BlockSpec(blk, lambda i, k_ref: (layer, i, 0))
    in_specs = [pl.BlockSpec(blk, lambda i, k_ref: (k_ref[0], i, 0)), rspec(0), rspec(1), rspec(2), espec, espec, espec]
    args = [kidx, part4, recv3, recv3, recv3, w, m, v]
    aliases = {}
    if bufs is not None:
        in_specs += [ANY] * 4
        aliases = {8 + q: q for q in range(4)}
        args += list(bufs)
    return _pcall(body, name=name, grid=(rows // tr,), prefetch=1, in_specs=in_specs, out_specs=[espec] * 4,
                  out_shape=[jax.ShapeDtypeStruct((depth, rows, cols), F32)] * 4, args=args, sem=("parallel",),
                  aliases=aliases)[0]


def _small_adamw(gath, w, m, v, *, name):
    rows = w.shape[0]

    def body(g_ref, w_ref, m_ref, v_ref, g_out, d_out, m_out, v_out):
        g = g_ref[0]
        for dev in range(1, N_DEV):
            g = g + g_ref[dev]
        delta, mn, vn = _adamw_math(w_ref[...], g, m_ref[...], v_ref[...])
        g_out[...] = g
        d_out[...] = delta
        m_out[...] = mn
        v_out[...] = vn

    tr = _row_tile(rows, 128 * N_DEV)
    espec = pl.BlockSpec((tr, 128), lambda i: (i, 0))
    return _pcall(body, name=name, grid=(rows // tr,),
                  in_specs=[pl.BlockSpec((N_DEV, tr, 128), lambda i: (0, i, 0)), espec, espec, espec],
                  out_specs=[espec] * 4, out_shape=[jax.ShapeDtypeStruct((rows, 128), F32)] * 4,
                  args=(gath, w, m, v), sem=("parallel",))[0]


def _pack(arrs):
    flat = jnp.concatenate([a.reshape(-1) for a in arrs])
    total = flat.shape[0]
    rows = -(-total // (128 * 64)) * 64
    return jnp.pad(flat, (0, rows * 128 - total)).reshape(rows, 128)


def _unpack(packed, like):
    flat = packed.reshape(-1)
    out, off = [], 0
    for a in like:
        out.append(flat[off:off + a.size].reshape(a.shape))
        off += a.size
    return out


def kernel(x, g_mix, w_in, b_forget, g_sgu, w_spatial, b_spatial, w_branch_a, w_branch_b, w_out, g_ffn, w_up, conv_w, conv_b, w_down, g_final, loss_target, m_g_mix, m_w_in, m_b_forget, m_g_sgu, m_w_spatial, m_b_spatial, m_w_branch_a, m_w_branch_b, m_w_out, m_g_ffn, m_w_up, m_conv_w, m_conv_b, m_w_down, m_g_final, v_g_mix, v_w_in, v_b_forget, v_g_sgu, v_w_spatial, v_b_spatial, v_w_branch_a, v_w_branch_b, v_w_out, v_g_ffn, v_w_up, v_conv_w, v_conv_b, v_w_down, v_g_final):
    depth, d = g_mix.shape
    heads = b_forget.shape[1]
    fw = heads * HEAD_DIM
    sw = g_sgu.shape[1]
    dff = conv_b.shape[1]
    t = x.shape[1]
    nsi = w_in.shape[2]
    nsu = w_up.shape[2]
    o_f, o_u, o_g = 3 * fw, 3 * fw + heads, 3 * fw + heads + 2 * sw

    bpad = jnp.pad(b_forget, ((0, 0), (0, 128 - heads)))
    bst = jnp.pad(jnp.swapaxes(b_spatial, 1, 2), ((0, 0), (0, 0), (0, 128 - b_spatial.shape[1])))

    lay = _InLayout(nsi, heads, fw, sw, d)

    def in_shards(l):
        return [jnp.pad(w_in[l].astype(BF16), ((0, 0), (0, lay.pw - nsi)))]

    def small_shards(l):
        return [w_branch_a[l].astype(BF16), w_branch_b[l].astype(BF16), w_out[l].astype(BF16), conv_w[l]]

    def ffn_shards(l):
        return [w_up[l].astype(BF16), w_down[l].astype(BF16)]

    def unpack_in(bufs, l):
        wqkv, wf, wuv, wg = _w_in_unpack(bufs[0], lay, name=f"unpack_w_in_{l}")
        return dict(wqkv=wqkv, wf=wf, wuv=wuv, wg=wg)

    def unpack_small(bufs):
        g_wa, g_wb, g_wo, g_cw = bufs
        plain = lambda g: jnp.moveaxis(g, 0, 1).reshape(g.shape[1], d)
        return dict(wa=g_wa, wb=g_wb, wa_t=plain(g_wa), wb_t=plain(g_wb), wo=g_wo.reshape(d, d),
                    cw=jnp.moveaxis(g_cw, 0, 1).reshape(CONV_WIDTH, dff))

    mixer_w = [None] * depth
    ffn_w = [None] * depth
    first = _comm_only(_gather_first(in_shards(0)), name="gather_in_first_0")
    mixer_w[0] = unpack_in(_comm_only(_gather_second(first), name="gather_in_second_0"), 0)
    small_first = None

    xs = x[0]
    saved = []
    for l in range(depth):
        n = lambda s: f"{s}_{l}"
        mw = mixer_w[l]
        h = _rms_fwd(xs, g_mix[l][None], name=n("rms_mix"))
        if small_first is None:
            qkv, small_first = _mm(h, mw["wqkv"], name=n("proj_qkv"), comm=_gather_first(small_shards(l)))
        else:
            qkv = _mm(h, mw["wqkv"], name=n("proj_qkv"))
        uv = _mm(h, mw["wuv"], name=n("proj_uv"))
        gates, small_bufs = _mm(h, mw["wg"], name=n("proj_gates"), comm=_gather_second(small_first))
        mw.update(unpack_small(small_bufs))
        flog = _mm(h, mw["wf"], out_dtype=F32, name=n("proj_forget"))
        c3 = _forget_fwd(flog, bpad[l][None], name=n("forget_fwd")).reshape(128, 1, t)
        (ya, ya32, lse), ffn_first = _fox_fwd(qkv, c3, heads, name=n("fox_fwd"), comm=_gather_first(ffn_shards(l)))
        yb = _sgu_fwd(uv, g_sgu[l][None], w_spatial[l], bst[l], name=n("sgu_fwd"))
        (merged, za, zb), _ = _merge_fwd(ya, yb, mw["wa"], mw["wb"], gates, name=n("merge_fwd"))
        x1, ffn_bufs = _mm(merged, mw["wo"], out_dtype=F32, res=xs, name=n("out_proj"), comm=_gather_second(ffn_first))
        g_wu, g_wd = ffn_bufs
        ffn_w[l] = dict(wu=g_wu, wd=g_wd.reshape(dff, d))
        h2 = _rms_fwd(x1, g_ffn[l][None], name=n("rms_ffn"))
        if l + 1 < depth:
            ab, in_first = _mm(h2, g_wu, b_cols=True, name=n("ffn_up"), comm=_gather_first(in_shards(l + 1)))
        else:
            ab = _mm(h2, g_wu, b_cols=True, name=n("ffn_up"))
        gg, g1, g2 = _conv_fwd(ab, mw["cw"], conv_b[l][None], name=n("conv_fwd"))
        if l + 1 < depth:
            both = _Join([_gather_second(in_first), _gather_first(small_shards(l + 1))])
            x2, nxt = _mm(gg, ffn_w[l]["wd"], out_dtype=F32, res=x1, name=n("ffn_down"), comm=both)
            in_bufs, small_first = both.split(nxt)
            mixer_w[l + 1] = unpack_in(in_bufs, l + 1)
        else:
            x2 = _mm(gg, ffn_w[l]["wd"], out_dtype=F32, res=x1, name=n("ffn_down"))
        saved.append((xs, h, qkv, uv, gates, flog, c3, ya, ya32, lse, yb, merged, za, zb, x1, h2, ab, gg, g1, g2))
        xs = x2

    loss_row, dx, dxb, d_g_final = _final_loss(xs, g_final[None], loss_target[0], name="final_loss")

    cidx = lax.axis_index("c").astype(jnp.int32).reshape(1)
    kidx = (2 * lax.axis_index("x") + lax.axis_index("y")).astype(jnp.int32).reshape(1)
    small_g = {k: [None] * depth for k in ("g_mix", "b_forget", "g_sgu", "w_spatial", "b_spatial", "g_ffn", "conv_b")}
    parts = {}
    recvs = {}
    mixer_names = ["w_in", "w_branch_a", "w_branch_b", "w_out"]
    ffn_names = ["w_up", "w_down", "conv_w"]
    for nm in mixer_names + ffn_names:
        parts[nm], recvs[nm] = [None] * depth, [None] * depth
    pending_mixer = None
    for l in reversed(range(depth)):
        n = lambda s: f"{s}_{l}"
        mw, fw_ = mixer_w[l], ffn_w[l]
        xs, h, qkv, uv, gates, flog, c3, ya, ya32, lse, yb, merged, za, zb, x1, h2, ab, gg, g1, g2 = saved[l]
        g_wd = _mm(gg, dxb, ta=True, name=n("d_w_down"))
        if pending_mixer is not None:
            dgg, got = _mm(dxb, fw_["wd"], tb=True, name=n("d_gg"), comm=_scatter_second(pending_mixer[1:]))
            for nm, r in zip(mixer_names[1:], got):
                recvs[nm][l + 1] = r
            (d_a, d_b, d_cw, d_cb), got = _conv_bwd(ab, g1, g2, mw["cw"], dgg, name=n("conv_bwd"),
                                                    comm=_scatter_second(pending_mixer[:1]))
            recvs[mixer_names[0]][l + 1] = got[0]
        else:
            dgg = _mm(dxb, fw_["wd"], tb=True, name=n("d_gg"))
            (d_a, d_b, d_cw, d_cb), _ = _conv_bwd(ab, g1, g2, mw["cw"], dgg, name=n("conv_bwd"))
        small_g["conv_b"][l] = d_cb[0]
        g_wu = _mm(h2, d_a, ta=True, out_cols=(N_DEV, 0, nsu), name=n("d_w_up_a"))
        g_wu = _mm(h2, d_b, ta=True, out_cols=(N_DEV, N_DEV // 2, nsu), out_alias=g_wu, name=n("d_w_up_b"))
        ffn_grads = [g_wu, g_wd.reshape(N_DEV, dff // N_DEV, d),
                     jnp.moveaxis(d_cw.reshape(CONV_WIDTH, N_DEV, dff // N_DEV), 1, 0)]
        dh2 = _mm(d_a, fw_["wu"], tb=True, b_cols=True, out_dtype=F32, name=n("d_h2_a"))
        dh2, from_sib = _mm(d_b, fw_["wu"], tb=True, b_cols=True, b_off=N_DEV // 2, out_dtype=F32, res=dh2,
                            name=n("d_h2_b"), comm=_scatter_first(ffn_grads))
        ffn_parts = [_pair_add(g8, r4, cidx, name=n(f"pair_add_{nm}")) for nm, g8, r4 in zip(ffn_names, ffn_grads, from_sib)]
        for nm, p in zip(ffn_names, ffn_parts):
            parts[nm][l] = p
        dx, dxb, dg = _rms_bwd(x1, g_ffn[l][None], dh2, dx, name=n("rms_ffn_bwd"))
        small_g["g_ffn"][l] = dg[0]
        g_wo = _mm(merged, dxb, ta=True, name=n("d_w_out"))
        dmg = _mm(dxb, mw["wo"], tb=True, name=n("d_merged"))
        dza, dzb, dgates = _merge_bwd(dmg, gates, za, zb, name=n("merge_bwd"))
        g_wa = _mm(ya, dza, ta=True, out_cols=(N_DEV, 0, d // N_DEV), name=n("d_w_a"))
        g_wb = _mm(yb, dzb, ta=True, out_cols=(N_DEV, 0, d // N_DEV), name=n("d_w_b"))
        dya = _mm(dza, mw["wa_t"], tb=True, name=n("d_ya"))
        dyb = _mm(dzb, mw["wb_t"], tb=True, name=n("d_yb"))
        duv, d_ws, d_bst, d_gs = _sgu_bwd(uv, g_sgu[l][None], w_spatial[l], bst[l], dyb, name=n("sgu_bwd"))
        small_g["w_spatial"][l], small_g["g_sgu"][l] = d_ws, d_gs[0]
        small_g["b_spatial"][l] = d_bst[:, :b_spatial.shape[1]].T
        (dq, dk, dv, dc3), got = _fox_bwd(qkv, c3, ya32, dya, lse, heads, name=n("fox_bwd"),
                                          comm=_scatter_second(ffn_parts))
        for nm, r in zip(ffn_names, got):
            recvs[nm][l] = r
        dct = jnp.pad(dc3.reshape(heads, t), ((0, 128 - heads), (0, 0)))
        dflog, d_bf = _forget_bwd(flog, bpad[l][None], dct, name=n("forget_bwd"))
        small_g["b_forget"][l] = d_bf[0, :heads]
        gw = [_mm(h, dq, ta=True, name=n("d_w_q")), _mm(h, dk, ta=True, name=n("d_w_k")),
              _mm(h, dv, ta=True, name=n("d_w_v")), _mm(h, dflog, ta=True, name=n("d_w_forget")),
              _mm(h, duv, ta=True, name=n("d_w_uv")), _mm(h, dgates, ta=True, name=n("d_w_gates"))]
        g_in = _w_in_grad_pack(gw, lay, name=n("pack_d_w_in"))
        mixer_grads = [g_in, g_wa, g_wb, g_wo.reshape(N_DEV, d // N_DEV, d)]
        dh = _mm(dflog, mw["wf"], tb=True, out_dtype=F32, name=n("d_h_forget"))
        dh = _mm(dq, mw["wqkv"], tb=True, out_dtype=F32, res=dh, name=n("d_h_q"))
        dh = _mm(dk, mw["wqkv"], tb=True, b_off=fw, out_dtype=F32, res=dh, name=n("d_h_k"))
        dh = _mm(dv, mw["wqkv"], tb=True, b_off=2 * fw, out_dtype=F32, res=dh, name=n("d_h_v"))
        dh = _mm(duv, mw["wuv"], tb=True, out_dtype=F32, res=dh, name=n("d_h_uv"))
        dh, from_sib = _mm(dgates, mw["wg"], tb=True, out_dtype=F32, res=dh, name=n("d_h_gates"),
                           comm=_scatter_first(mixer_grads))
        pending_mixer = [_pair_add(g8, r4, cidx, name=n(f"pair_add_{nm}"))
                         for nm, g8, r4 in zip(mixer_names, mixer_grads, from_sib)]
        for nm, p in zip(mixer_names, pending_mixer):
            parts[nm][l] = p
        dx, dxb, dg = _rms_bwd(xs, g_mix[l][None], dh, dx, name=n("rms_mix_bwd"))
        small_g["g_mix"][l] = dg[0]
    grad_x = dx[None]
    for nm, r in zip(mixer_names, _comm_only(_scatter_second(pending_mixer), name="scatter_mixer_second_0")):
        recvs[nm][0] = r

    weights = {"w_in": (w_in, m_w_in, v_w_in), "w_branch_a": (w_branch_a, m_w_branch_a, v_w_branch_a),
               "w_branch_b": (w_branch_b, m_w_branch_b, v_w_branch_b), "w_out": (w_out, m_w_out, v_w_out),
               "w_up": (w_up, m_w_up, v_w_up), "conv_w": (conv_w, m_conv_w, v_conv_w),
               "w_down": (w_down, m_w_down, v_w_down)}
    res = {}
    for nm, (w, m, v) in weights.items():
        bufs = None
        for l in range(depth):
            bufs = _shard_adamw(parts[nm][l], recvs[nm][l], w, m, v, kidx, l, bufs, name=f"adamw_{nm}_{l}")
        res[nm] = bufs

    small = ["g_mix", "b_forget", "g_sgu", "w_spatial", "b_spatial", "g_ffn", "conv_b", "g_final", "loss"]
    zero = jnp.zeros((1,), F32)
    small_w = [g_mix, b_forget, g_sgu, w_spatial, b_spatial, g_ffn, conv_b, g_final, zero]
    small_m = [m_g_mix, m_b_forget, m_g_sgu, m_w_spatial, m_b_spatial, m_g_ffn, m_conv_b, m_g_final, zero]
    small_v = [v_g_mix, v_b_forget, v_g_sgu, v_w_spatial, v_b_spatial, v_g_ffn, v_conv_b, v_g_final, zero]
    small_grads = [jnp.stack(small_g[nm]) for nm in small[:-2]] + [d_g_final[0], loss_row[0, 0:1]]
    first = _comm_only(_gather_first([_pack(small_grads)]), name="gather_small_first")
    (gath,) = _comm_only(_gather_second(first), name="gather_small_second")
    outs = _small_adamw(gath, _pack(small_w), _pack(small_m), _pack(small_v), name="adamw_replicated")
    for nm, vals in zip(small, zip(*[_unpack(o, small_w) for o in outs])):
        res[nm] = list(vals)
    loss = res["loss"][0][0]

    order = ["g_mix", "w_in", "b_forget", "g_sgu", "w_spatial", "b_spatial", "w_branch_a", "w_branch_b", "w_out",
             "g_ffn", "w_up", "conv_w", "conv_b", "w_down", "g_final"]
    return (loss, grad_x, *[res[nm][0] for nm in order], *[res[nm][1] for nm in order],
            *[res[nm][2] for nm in order], *[res[nm][3] for nm in order])
```

```python
import numpy as np

import jax
import jax.numpy as jnp
from jax import lax
from jax.experimental import pallas as pl
from jax.experimental.pallas import tpu as pltpu

F32 = jnp.float32
BF16 = jnp.bfloat16

RMS_EPS = 1e-6
HEAD_DIM = 128
CONV_WIDTH = 3
ADAM_LR = 0.001
ADAM_B1 = 0.9
ADAM_B2 = 0.999
ADAM_EPS = 1e-08
ADAM_WD = 0.01
ADAM_STEP = 10
N_DEV = 8
V7X_VMEM_LIMIT = 58 * 1024 * 1024
MM_VMEM_BUDGET = 46 * 1024 * 1024
NEG = -1e30
ANY = pl.BlockSpec(memory_space=pl.ANY)
MESH = pl.DeviceIdType.MESH


def _tile(dim, pref):
    for t in (2048, 1024, 512, 256, 128):
        if t <= pref and dim % t == 0:
            return t
    return dim


def _gelu(x):
    t = jnp.tanh(0.7978845608028654 * (x + 0.044715 * (x * x * x)))
    return x * (0.5 * (1.0 + t))


def _gelu_and_grad(x):
    x2 = x * x
    t = jnp.tanh(0.7978845608028654 * (x + 0.044715 * (x2 * x)))
    cdf = 0.5 * (1.0 + t)
    dt = (1.0 - t * t) * (0.7978845608028654 * (1.0 + 3.0 * 0.044715 * x2))
    return x * cdf, cdf + 0.5 * x * dt


def _sigmoid(x):
    return 1.0 / (1.0 + jnp.exp(-x))


class _Comm:
    def __init__(self, srcs, new, alias, n_copies, emit):
        self.srcs = list(srcs)
        self.new = list(new)
        self.alias = list(alias)
        self.n_copies = n_copies
        self.emit = emit

    def split(self, couts):
        return [couts]


class _Join(_Comm):
    def __init__(self, comms):
        self.comms = comms
        srcs = [s for cm in comms for s in cm.srcs]
        new = [s for cm in comms for s in cm.new]
        alias = [s for cm in comms for s in cm.alias]

        def emit(src_refs, new_refs, alias_refs, sems):
            copies, s0, n0, a0, k0 = [], 0, 0, 0, sems[2]
            for cm in comms:
                copies += cm.emit(src_refs[s0:s0 + len(cm.srcs)], new_refs[n0:n0 + len(cm.new)],
                                  alias_refs[a0:a0 + len(cm.alias)], (sems[0], sems[1], k0))
                s0, n0, a0, k0 = s0 + len(cm.srcs), n0 + len(cm.new), a0 + len(cm.alias), k0 + cm.n_copies
            return copies

        super().__init__(srcs, new, alias, sum(cm.n_copies for cm in comms), emit)

    def split(self, couts):
        n_new = len(self.new)
        out, n0, a0 = [], 0, 0
        for cm in self.comms:
            out.append(couts[n0:n0 + len(cm.new)] + couts[n_new + a0:n_new + a0 + len(cm.alias)])
            n0, a0 = n0 + len(cm.new), a0 + len(cm.alias)
        return out


def _place():
    x, y, c = lax.axis_index("x"), lax.axis_index("y"), lax.axis_index("c")
    chips = [(1 - x, y), (x, 1 - y), (1 - x, 1 - y)]
    return x, y, c, chips


def _remote(src, dst, sems, k, to):
    return pltpu.make_async_remote_copy(src_ref=src, dst_ref=dst, send_sem=sems[0].at[sems[2] + k],
                                        recv_sem=sems[1].at[sems[2] + k], device_id=to, device_id_type=MESH)


def _gather_first(shards):
    n = len(shards)

    def emit(srcs, new, alias, sems):
        x, y, c, chips = _place()
        me = 4 * x + 2 * y + c
        copies = []
        for a in range(n):
            copies.append(pltpu.make_async_copy(srcs[a], new[a].at[me], sems[0].at[sems[2] + 5 * a + 4]))
            copies.append(_remote(srcs[a], new[a].at[me], sems, 5 * a, (x, y, 1 - c)))
            for j, chip in enumerate(chips):
                copies.append(_remote(srcs[a], new[a].at[me], sems, 5 * a + 1 + j, (*chip, c)))
        return copies

    new = [jax.ShapeDtypeStruct((N_DEV,) + s.shape, s.dtype) for s in shards]
    return _Comm(shards, new, [], 5 * n, emit)


def _gather_second(bufs):
    n = len(bufs)

    def emit(srcs, new, alias, sems):
        x, y, c, chips = _place()
        copies = []
        for a in range(n):
            for j, chip in enumerate(chips):
                blk = alias[a].at[4 * chip[0] + 2 * chip[1] + c]
                copies.append(_remote(blk, blk, sems, 3 * a + j, (x, y, 1 - c)))
        return copies

    return _Comm([], [], bufs, 3 * n, emit)


def _scatter_first(grads):
    n = len(grads)

    def emit(srcs, new, alias, sems):
        x, y, c, _ = _place()
        return [_remote(srcs[a].at[2 * k + 1 - c], new[a].at[k], sems, 4 * a + k, (x, y, 1 - c))
                for a in range(n) for k in range(4)]

    new = [jax.ShapeDtypeStruct((4,) + g.shape[1:], g.dtype) for g in grads]
    return _Comm(grads, new, [], 4 * n, emit)


def _scatter_second(parts, rows=None, bufs=None):
    n = len(parts)

    def emit(srcs, new, alias, sems):
        x, y, c, chips = _place()
        dsts = new if bufs is None else alias
        win = (lambda r, idx: r.at[idx]) if rows is None else (lambda r, idx: r.at[idx, pl.ds(rows[0], rows[1])])
        return [_remote(win(srcs[a], 2 * chip[0] + chip[1]), win(dsts[a], j), sems, 3 * a + j, (*chip, c))
                for a in range(n) for j, chip in enumerate(chips)]

    new = [jax.ShapeDtypeStruct((3,) + p.shape[1:], p.dtype) for p in parts] if bufs is None else []
    return _Comm(parts, new, [] if bufs is None else bufs, 3 * n, emit)


def _pcall(body, *, name, grid, in_specs, out_specs, out_shape, args, sem, scratch=(), aliases=None, comm=None,
           prefetch=0):
    in_specs, out_specs, out_shape, scratch = list(in_specs), list(out_specs), list(out_shape), list(scratch)
    aliases = dict(aliases or {})
    n_in, n_out, n_scr = len(in_specs), len(out_shape), len(scratch)

    def make(body_fn, ins, outs, shapes, scr, sem_):
        params = pltpu.CompilerParams(dimension_semantics=sem_, vmem_limit_bytes=V7X_VMEM_LIMIT)
        if prefetch:
            spec = pltpu.PrefetchScalarGridSpec(num_scalar_prefetch=prefetch, grid=grid, in_specs=ins,
                                                out_specs=outs, scratch_shapes=scr)
            return pl.pallas_call(body_fn, name=name, grid_spec=spec, out_shape=shapes,
                                  input_output_aliases=aliases, compiler_params=params)
        return pl.pallas_call(body_fn, name=name, grid=grid, in_specs=ins, out_specs=outs, out_shape=shapes,
                              scratch_shapes=scr, input_output_aliases=aliases, compiler_params=params)

    if comm is None:
        return list(make(body, in_specs, out_specs, out_shape, scratch, sem)(*args)), []

    n_src, n_new, n_al = len(comm.srcs), len(comm.new), len(comm.alias)
    for a in range(n_al):
        aliases[prefetch + n_in + n_src + a] = n_out + n_new + a

    def wrapped(*refs):
        pre, refs = refs[:prefetch], refs[prefetch:]
        ins = refs[:n_in]
        src_refs = refs[n_in:n_in + n_src]
        o0 = n_in + n_src + n_al
        outs = refs[o0:o0 + n_out]
        new_refs = refs[o0 + n_out:o0 + n_out + n_new]
        alias_refs = refs[o0 + n_out + n_new:o0 + n_out + n_new + n_al]
        s0 = o0 + n_out + n_new + n_al
        scr = refs[s0:s0 + n_scr]
        send_sems, recv_sems = refs[s0 + n_scr], refs[s0 + n_scr + 1]
        first = pl.program_id(0) == 0
        last = pl.program_id(0) == grid[0] - 1
        for dim in range(1, len(grid)):
            first = first & (pl.program_id(dim) == 0)
            last = last & (pl.program_id(dim) == grid[dim] - 1)

        @pl.when(first)
        def _():
            for cp in comm.emit(src_refs, new_refs, alias_refs, (send_sems, recv_sems, 0)):
                cp.start()

        body(*pre, *ins, *outs, *scr)

        @pl.when(last)
        def _():
            for cp in comm.emit(src_refs, new_refs, alias_refs, (send_sems, recv_sems, 0)):
                cp.wait()

    call = make(wrapped, in_specs + [ANY] * (n_src + n_al), out_specs + [ANY] * (n_new + n_al),
                out_shape + comm.new + [jax.ShapeDtypeStruct(b.shape, b.dtype) for b in comm.alias],
                scratch + [pltpu.SemaphoreType.DMA((comm.n_copies,)), pltpu.SemaphoreType.DMA((comm.n_copies,))],
                ("arbitrary",) * len(grid))
    res = list(call(*args, *comm.srcs, *comm.alias))
    return res[:n_out], res[n_out:]


def _comm_only(comm, *, name):
    def body(o_ref):
        o_ref[...] = jnp.zeros_like(o_ref)

    _, couts = _pcall(body, name=name, grid=(1,), in_specs=[], out_specs=[pl.BlockSpec((8, 128), lambda i: (0, 0))],
                      out_shape=[jax.ShapeDtypeStruct((8, 128), F32)], args=(), sem=("arbitrary",), comm=comm)
    return couts


def _divisor_tiles(dim, cap):
    tiles = [t for t in range(128, min(dim, cap) + 1, 128) if dim % t == 0]
    return sorted(tiles, reverse=True) or [dim]


def _mm_tiles(m, n, k, obytes, has_res, tn_fixed=None, tk_fixed=None):
    tms = _divisor_tiles(m, 1408)
    tns = [tn_fixed] if tn_fixed else _divisor_tiles(n, 1408)
    tks = [tk_fixed] if tk_fixed else [k] + [tt for tt in _divisor_tiles(k, 2048) if tt != k]
    best, best_score = None, None
    for tk in tks:
        for tm in tms:
            for tn in tns:
                nk = k // tk
                use = 4 * tm * tk + 4 * tk * tn + 2 * tm * tn * obytes
                use += (8 * tm * tn if has_res else 0) + (4 * tm * tn if nk > 1 else 0)
                score = (nk == 1, min(tm, 1024), tn, tm, tk)
                if use <= MM_VMEM_BUDGET and (best is None or score > best_score):
                    best, best_score = (tm, tn, tk), score
    assert best is not None, (m, n, k)
    return best


def _mm(a, b, *, ta=False, tb=False, out_dtype=BF16, res=None, name, b_cols=False, b_off=0,
        out_cols=None, out_alias=None, comm=None):
    m, k = (a.shape[1], a.shape[0]) if ta else a.shape
    obytes = jnp.dtype(out_dtype).itemsize
    if b_cols and not tb:
        ns = b.shape[2]
        assert b.shape[1] == k
        n = b.shape[0] * ns
        tm, tn, tk = _mm_tiles(m, n, k, obytes, res is not None, tn_fixed=ns)
        b_spec = pl.BlockSpec((None, tk, ns), lambda i, j, kk: (j, kk, 0))
    elif b_cols:
        ns = b.shape[2]
        assert k % ns == 0
        n = b.shape[1]
        tm, tn, tk = _mm_tiles(m, n, k, obytes, res is not None, tk_fixed=ns)
        b_spec = pl.BlockSpec((None, tn, ns), lambda i, j, kk: (b_off + kk, j, 0))
    else:
        n = b.shape[0] if tb else b.shape[1]
        assert (b.shape[1] >= b_off + k) if tb else (b.shape[0] == k and b_off == 0)
        tm, tn, tk = _mm_tiles(m, n, k, obytes, res is not None, tn_fixed=out_cols[2] if out_cols is not None else None)
        assert b_off % tk == 0
        k0 = b_off // tk
        b_spec = pl.BlockSpec((tn, tk), lambda i, j, kk: (j, k0 + kk)) if tb else pl.BlockSpec((tk, tn), lambda i, j, kk: (kk, j))
    nk = k // tk
    dn = (((0,) if ta else (1,), (1,) if tb else (0,)), ((), ()))
    n_extra = (res is not None) + (out_alias is not None)

    def body(*refs):
        a_ref, b_ref = refs[0], refs[1]
        r_ref = refs[2] if res is not None else None
        o_ref = refs[2 + n_extra]
        part = lax.dot_general(a_ref[...], b_ref[...], dn, preferred_element_type=F32)

        def finish(r):
            if r_ref is not None:
                r = r + r_ref[...]
            o_ref[...] = r.astype(out_dtype)

        if nk == 1:
            finish(part)
        else:
            acc_ref = refs[3 + n_extra]
            kk = pl.program_id(2)

            @pl.when(kk == 0)
            def _():
                acc_ref[...] = part

            @pl.when(kk > 0)
            def _():
                acc_ref[...] += part

            @pl.when(kk == nk - 1)
            def _():
                finish(acc_ref[...])

    a_spec = pl.BlockSpec((tk, tm), lambda i, j, kk: (kk, i)) if ta else pl.BlockSpec((tm, tk), lambda i, j, kk: (i, kk))
    in_specs, args, aliases = [a_spec, b_spec], [a, b], {}
    if res is not None:
        in_specs.append(pl.BlockSpec((tm, tn), lambda i, j, kk: (i, j)))
        args.append(res)
    if out_cols is not None:
        s_total, o_off, ns_o = out_cols
        assert tn == ns_o and n % ns_o == 0
        o_spec = pl.BlockSpec((None, tm, tn), lambda i, j, kk: (o_off + j, i, 0))
        o_shape = jax.ShapeDtypeStruct((s_total, m, tn), out_dtype)
        if out_alias is not None:
            in_specs.append(ANY)
            args.append(out_alias)
            aliases[len(args) - 1] = 0
    else:
        o_spec = pl.BlockSpec((tm, tn), lambda i, j, kk: (i, j))
        o_shape = jax.ShapeDtypeStruct((m, n), out_dtype)
    outs, couts = _pcall(body, name=name, grid=(m // tm, n // tn, nk), in_specs=in_specs, out_specs=[o_spec],
                         out_shape=[o_shape], args=args, sem=("parallel", "parallel", "arbitrary"),
                         scratch=[pltpu.VMEM((tm, tn), F32)] if nk > 1 else [], aliases=aliases, comm=comm)
    return (outs[0], couts) if comm is not None else outs[0]


def _rms_fwd(x, g, *, name):
    t, d = x.shape
    tr = _tile(t, 256)

    def body(x_ref, g_ref, h_ref):
        xf = x_ref[...]
        inv = lax.rsqrt(jnp.mean(xf * xf, axis=-1, keepdims=True) + RMS_EPS)
        h_ref[...] = ((xf * inv) * g_ref[...]).astype(BF16)

    row = pl.BlockSpec((tr, d), lambda i: (i, 0))
    return _pcall(body, name=name, grid=(t // tr,), in_specs=[row, pl.BlockSpec((1, d), lambda i: (0, 0))],
                  out_specs=[row], out_shape=[jax.ShapeDtypeStruct((t, d), BF16)], args=(x, g),
                  sem=("parallel",))[0][0]


def _rms_bwd(x, g, dh, dres, *, name):
    t, d = x.shape
    tr = _tile(t, 256)

    def body(x_ref, g_ref, dh_ref, dres_ref, dx_ref, dxb_ref, dg_ref):
        xf = x_ref[...]
        inv = lax.rsqrt(jnp.mean(xf * xf, axis=-1, keepdims=True) + RMS_EPS)
        xn = xf * inv
        dh_f = dh_ref[...].astype(F32)
        dxn = dh_f * g_ref[...]
        dx = dres_ref[...] + inv * (dxn - xn * jnp.mean(dxn * xn, axis=-1, keepdims=True))
        dx_ref[...] = dx
        dxb_ref[...] = dx.astype(BF16)
        part = jnp.sum(dh_f * xn, axis=0, keepdims=True)

        @pl.when(pl.program_id(0) == 0)
        def _():
            dg_ref[...] = part

        @pl.when(pl.program_id(0) > 0)
        def _():
            dg_ref[...] += part

    row = pl.BlockSpec((tr, d), lambda i: (i, 0))
    vec = pl.BlockSpec((1, d), lambda i: (0, 0))
    return _pcall(body, name=name, grid=(t // tr,), in_specs=[row, vec, row, row], out_specs=[row, row, vec],
                  out_shape=[jax.ShapeDtypeStruct((t, d), F32), jax.ShapeDtypeStruct((t, d), BF16),
                             jax.ShapeDtypeStruct((1, d), F32)], args=(x, g, dh, dres), sem=("arbitrary",))[0]


def _final_loss(x, g, target, *, name):
    t, d = x.shape
    tr = _tile(t, 256)

    def body(x_ref, g_ref, tg_ref, loss_ref, dx_ref, dxb_ref, dg_ref):
        xf = x_ref[...]
        gv = g_ref[...]
        inv = lax.rsqrt(jnp.mean(xf * xf, axis=-1, keepdims=True) + RMS_EPS)
        xn = xf * inv
        err = xn * gv - tg_ref[...]
        lpart = 0.5 * jnp.sum(jnp.mean(err * err, axis=-1, keepdims=True), axis=0, keepdims=True)
        dy = err * (1.0 / d)
        dxn = dy * gv
        dx = inv * (dxn - xn * jnp.mean(dxn * xn, axis=-1, keepdims=True))
        dx_ref[...] = dx
        dxb_ref[...] = dx.astype(BF16)
        gpart = jnp.sum(dy * xn, axis=0, keepdims=True)
        lrow = jnp.broadcast_to(lpart, (1, 128))

        @pl.when(pl.program_id(0) == 0)
        def _():
            dg_ref[...] = gpart
            loss_ref[...] = lrow

        @pl.when(pl.program_id(0) > 0)
        def _():
            dg_ref[...] += gpart
            loss_ref[...] += lrow

    row = pl.BlockSpec((tr, d), lambda i: (i, 0))
    vec = pl.BlockSpec((1, d), lambda i: (0, 0))
    lspec = pl.BlockSpec((1, 128), lambda i: (0, 0))
    return _pcall(body, name=name, grid=(t // tr,), in_specs=[row, vec, row], out_specs=[lspec, row, row, vec],
                  out_shape=[jax.ShapeDtypeStruct((1, 128), F32), jax.ShapeDtypeStruct((t, d), F32),
                             jax.ShapeDtypeStruct((t, d), BF16), jax.ShapeDtypeStruct((1, d), F32)],
                  args=(x, g, target), sem=("arbitrary",))[0]


def _tri_ones(n, upper):
    r = lax.broadcasted_iota(jnp.int32, (n, n), 0)
    c = lax.broadcasted_iota(jnp.int32, (n, n), 1)
    return jnp.where((r <= c) if upper else (r >= c), 1.0, 0.0).astype(F32)


def _forget_fwd(flog, bpad, *, name):
    t = flog.shape[0]
    tb = _tile(t, 512)

    def body(f_ref, b_ref, c_ref, carry):
        z = f_ref[...] + b_ref[...]
        lf = jnp.minimum(z, 0.0) - jnp.log(1.0 + jnp.exp(-jnp.abs(z)))
        lft = lf.T
        tri = _tri_ones(tb, upper=True)

        @pl.when(pl.program_id(0) == 0)
        def _():
            carry[...] = jnp.zeros_like(carry)

        cs = jnp.dot(lft, tri, preferred_element_type=F32, precision=lax.Precision.HIGHEST) + carry[:, 0:1]
        c_ref[...] = cs
        carry[...] = jnp.broadcast_to(cs[:, tb - 1:tb], carry.shape)

    return _pcall(body, name=name, grid=(t // tb,),
                  in_specs=[pl.BlockSpec((tb, 128), lambda i: (i, 0)), pl.BlockSpec((1, 128), lambda i: (0, 0))],
                  out_specs=[pl.BlockSpec((128, tb), lambda i: (0, i))],
                  out_shape=[jax.ShapeDtypeStruct((128, t), F32)], args=(flog, bpad), sem=("arbitrary",),
                  scratch=[pltpu.VMEM((128, 128), F32)])[0][0]


def _forget_bwd(flog, bpad, dct, *, name):
    t = flog.shape[0]
    tb = _tile(t, 512)
    nb = t // tb

    def body(f_ref, b_ref, dc_ref, df_ref, db_ref, carry):
        i = pl.program_id(0)

        @pl.when(i == 0)
        def _():
            carry[...] = jnp.zeros_like(carry)

        tri = _tri_ones(tb, upper=False)
        dl = jnp.dot(dc_ref[...], tri, preferred_element_type=F32, precision=lax.Precision.HIGHEST) + carry[:, 0:1]
        carry[...] = jnp.broadcast_to(dl[:, 0:1], carry.shape)
        z = f_ref[...] + b_ref[...]
        df = dl.T * _sigmoid(-z)
        df_ref[...] = df.astype(BF16)
        part = jnp.sum(df, axis=0, keepdims=True)

        @pl.when(i == 0)
        def _():
            db_ref[...] = part

        @pl.when(i > 0)
        def _():
            db_ref[...] += part

    rev = lambda i: (nb - 1 - i, 0)
    return _pcall(body, name=name, grid=(nb,),
                  in_specs=[pl.BlockSpec((tb, 128), rev), pl.BlockSpec((1, 128), lambda i: (0, 0)),
                            pl.BlockSpec((128, tb), lambda i: (0, nb - 1 - i))],
                  out_specs=[pl.BlockSpec((tb, 128), rev), pl.BlockSpec((1, 128), lambda i: (0, 0))],
                  out_shape=[jax.ShapeDtypeStruct((t, 128), BF16), jax.ShapeDtypeStruct((1, 128), F32)],
                  args=(flog, bpad, dct), sem=("arbitrary",), scratch=[pltpu.VMEM((128, 128), F32)])[0]


def _causal_pairs(nq, k_major):
    if k_major:
        pairs = [(i, j) for j in range(nq) for i in range(j, nq)]
    else:
        pairs = [(i, j) for i in range(nq) for j in range(i + 1)]
    return (jnp.asarray(np.array([p[0] for p in pairs], np.int32)),
            jnp.asarray(np.array([p[1] for p in pairs], np.int32)), len(pairs))


def _logits(q, k, cq, ck, scale, masked):
    s = lax.dot_general(q, k, (((1,), (1,)), ((), ())), preferred_element_type=F32)
    s = s * scale + (cq[:, 0:1] - ck)
    if masked:
        row = lax.broadcasted_iota(jnp.int32, s.shape, 0)
        col = lax.broadcasted_iota(jnp.int32, s.shape, 1)
        s = jnp.where(col <= row, s, NEG)
    return s


def _head_group(heads):
    return 4 if heads % 4 == 0 else (2 if heads % 2 == 0 else 1)


def _fox_fwd(qkv, c3, heads, *, name, comm=None):
    t = qkv.shape[0]
    tq = _tile(t, 512)
    nq = t // tq
    hb = _head_group(heads)
    ng = heads // hb
    scale = HEAD_DIM ** -0.5
    i_tab, j_tab, npairs = _causal_pairs(nq, k_major=False)

    def body(it_ref, jt_ref, q_ref, k_ref, v_ref, cq_ref, ck_ref, o_ref, o32_ref, lse_ref, m_s, l_s, acc_s):
        p_id = pl.program_id(1)
        i, j = it_ref[p_id], jt_ref[p_id]

        @pl.when(j == 0)
        def _():
            m_s[...] = jnp.full_like(m_s, NEG)
            l_s[...] = jnp.zeros_like(l_s)
            acc_s[...] = jnp.zeros_like(acc_s)

        def update(masked):
            for hh in range(hb):
                ls = slice(hh * 128, (hh + 1) * 128)
                s = _logits(q_ref[:, ls], k_ref[:, ls], cq_ref[hh], ck_ref[hh], scale, masked)
                m_prev = m_s[hh, :, 0:1]
                m_new = jnp.maximum(m_prev, jnp.max(s, axis=1, keepdims=True))
                alpha = jnp.exp(m_prev - m_new)
                p = jnp.exp(s - m_new)
                l_s[hh, :, 0:1] = alpha * l_s[hh, :, 0:1] + jnp.sum(p, axis=1, keepdims=True)
                p_hi = p.astype(BF16)
                p_lo = (p - p_hi.astype(F32)).astype(BF16)
                vb = v_ref[:, ls]
                pv = jnp.dot(p_hi, vb, preferred_element_type=F32) + jnp.dot(p_lo, vb, preferred_element_type=F32)
                acc_s[hh] = alpha * acc_s[hh] + pv
                m_s[hh, :, 0:1] = m_new

        @pl.when(j < i)
        def _():
            update(False)

        @pl.when(j == i)
        def _():
            update(True)
            for hh in range(hb):
                ls = slice(hh * 128, (hh + 1) * 128)
                l = l_s[hh, :, 0:1]
                o = acc_s[hh] / l
                o_ref[:, ls] = o.astype(BF16)
                o32_ref[:, ls] = o
                lse_ref[hh] = jnp.broadcast_to(m_s[hh, :, 0:1] + jnp.log(l), (tq, 128))

    w = hb * 128
    qb = lambda g, p, it, jt: (it[p], g)
    outs, couts = _pcall(
        body, name=name, grid=(ng, npairs), prefetch=2,
        in_specs=[pl.BlockSpec((tq, w), qb),
                  pl.BlockSpec((tq, w), lambda g, p, it, jt: (jt[p], ng + g)),
                  pl.BlockSpec((tq, w), lambda g, p, it, jt: (jt[p], 2 * ng + g)),
                  pl.BlockSpec((hb, 1, tq), lambda g, p, it, jt: (g, 0, it[p])),
                  pl.BlockSpec((hb, 1, tq), lambda g, p, it, jt: (g, 0, jt[p]))],
        out_specs=[pl.BlockSpec((tq, w), qb), pl.BlockSpec((tq, w), qb),
                   pl.BlockSpec((hb, tq, 128), lambda g, p, it, jt: (g, it[p], 0))],
        out_shape=[jax.ShapeDtypeStruct((t, heads * 128), BF16), jax.ShapeDtypeStruct((t, heads * 128), F32),
                   jax.ShapeDtypeStruct((heads, t, 128), F32)],
        args=(i_tab, j_tab, qkv, qkv, qkv, c3, c3), sem=("parallel", "arbitrary"),
        scratch=[pltpu.VMEM((hb, tq, 128), F32), pltpu.VMEM((hb, tq, 128), F32), pltpu.VMEM((hb, tq, 128), F32)],
        comm=comm)
    return outs, couts


def _fox_bwd(qkv, c3, o, do, lse, heads, *, name, comm=None):
    t = qkv.shape[0]
    tq = _tile(t, 512)
    nq = t // tq
    hb = _head_group(heads)
    ng = heads // hb
    scale = HEAD_DIM ** -0.5
    i_tab, j_tab, npairs = _causal_pairs(nq, k_major=True)

    def body(it_ref, jt_ref, q_ref, k_ref, v_ref, o_ref, do_ref, lse_ref, cq_ref, ck_ref,
             dq_ref, dk_ref, dv_ref, dc_ref, dq_s, dk_s, dv_s, dc_s):
        p_id = pl.program_id(1)
        i, j = it_ref[p_id], jt_ref[p_id]

        @pl.when(p_id == 0)
        def _():
            dq_s[...] = jnp.zeros_like(dq_s)

        @pl.when(i == j)
        def _():
            dk_s[...] = jnp.zeros_like(dk_s)
            dv_s[...] = jnp.zeros_like(dv_s)
            dc_s[...] = jnp.zeros_like(dc_s)

        def update(masked):
            r0 = pl.multiple_of(i * tq, tq)
            for hh in range(hb):
                ls = slice(hh * 128, (hh + 1) * 128)
                q, k, v, dob = q_ref[:, ls], k_ref[:, ls], v_ref[:, ls], do_ref[:, ls]
                s = _logits(q, k, cq_ref[hh], ck_ref[hh], scale, masked)
                p = jnp.exp(s - lse_ref[hh, :, 0:1])
                delta = jnp.sum(dob.astype(F32) * o_ref[:, ls], axis=1, keepdims=True)
                dp = lax.dot_general(dob, v, (((1,), (1,)), ((), ())), preferred_element_type=F32)
                ds = p * (dp - delta)
                pb, dsb = p.astype(BF16), ds.astype(BF16)
                dv_s[hh] += lax.dot_general(pb, dob, (((0,), (0,)), ((), ())), preferred_element_type=F32)
                dk_s[hh] += lax.dot_general(dsb, q, (((0,), (0,)), ((), ())), preferred_element_type=F32)
                dq_s[hh, pl.ds(r0, tq), :] += jnp.dot(dsb, k, preferred_element_type=F32) * scale
                dc_s[hh] -= jnp.sum(ds, axis=0, keepdims=True)

        @pl.when(i > j)
        def _():
            update(False)

        @pl.when(i == j)
        def _():
            update(True)

        @pl.when(i == nq - 1)
        def _():
            for hh in range(hb):
                ls = slice(hh * 128, (hh + 1) * 128)
                dk_ref[:, ls] = (dk_s[hh] * scale).astype(BF16)
                dv_ref[:, ls] = dv_s[hh].astype(BF16)
            dc_ref[...] = dc_s[...]

        @pl.when(p_id == npairs - 1)
        def _():
            for hh in range(hb):
                dq_ref[:, hh * 128:(hh + 1) * 128] = dq_s[hh].astype(BF16)

    w = hb * 128
    qb = lambda g, p, it, jt: (it[p], g)
    kb = lambda g, p, it, jt: (jt[p], g)
    outs, couts = _pcall(
        body, name=name, grid=(ng, npairs), prefetch=2,
        in_specs=[pl.BlockSpec((tq, w), qb),
                  pl.BlockSpec((tq, w), lambda g, p, it, jt: (jt[p], ng + g)),
                  pl.BlockSpec((tq, w), lambda g, p, it, jt: (jt[p], 2 * ng + g)),
                  pl.BlockSpec((tq, w), qb), pl.BlockSpec((tq, w), qb),
                  pl.BlockSpec((hb, tq, 128), lambda g, p, it, jt: (g, it[p], 0)),
                  pl.BlockSpec((hb, 1, tq), lambda g, p, it, jt: (g, 0, it[p])),
                  pl.BlockSpec((hb, 1, tq), lambda g, p, it, jt: (g, 0, jt[p]))],
        out_specs=[pl.BlockSpec((t, w), lambda g, p, it, jt: (0, g)), pl.BlockSpec((tq, w), kb),
                   pl.BlockSpec((tq, w), kb), pl.BlockSpec((hb, 1, tq), lambda g, p, it, jt: (g, 0, jt[p]))],
        out_shape=[jax.ShapeDtypeStruct((t, heads * 128), BF16), jax.ShapeDtypeStruct((t, heads * 128), BF16),
                   jax.ShapeDtypeStruct((t, heads * 128), BF16), jax.ShapeDtypeStruct((heads, 1, t), F32)],
        args=(i_tab, j_tab, qkv, qkv, qkv, o, do, lse, c3, c3), sem=("arbitrary", "arbitrary"),
        scratch=[pltpu.VMEM((hb, t, 128), F32), pltpu.VMEM((hb, tq, 128), F32), pltpu.VMEM((hb, tq, 128), F32),
                 pltpu.VMEM((hb, 1, tq), F32)], comm=comm)
    return outs, couts


def _tril_mask():
    r = lax.broadcasted_iota(jnp.int32, (128, 128), 0)
    c = lax.broadcasted_iota(jnp.int32, (128, 128), 1)
    return r >= c


def _sgu_fwd(uv, g, w, bst, *, name):
    t = uv.shape[0]
    sw = uv.shape[1] // 2
    groups = sw // 128
    tr = _tile(t, 512)

    def body(u_ref, v_ref, g_ref, w_ref, b_ref, y_ref):
        gv = _gelu(v_ref[...].astype(F32))
        inv = lax.rsqrt(jnp.mean(gv * gv, axis=-1, keepdims=True) + RMS_EPS)
        vn = ((gv * inv) * g_ref[...]).astype(BF16)
        gu = _gelu(u_ref[...].astype(F32))
        mask = _tril_mask()
        for gi in range(groups):
            wg = jnp.where(mask, w_ref[gi], 0.0).astype(BF16)
            bcol = b_ref[:, gi:gi + 1]
            cs = slice(gi * 128, (gi + 1) * 128)
            for ci in range(tr // 128):
                rs = slice(ci * 128, (ci + 1) * 128)
                mixed = jnp.dot(wg, vn[rs, cs], preferred_element_type=F32) + bcol
                y_ref[rs, cs] = (gu[rs, cs] * mixed).astype(BF16)

    return _pcall(body, name=name, grid=(t // tr,),
                  in_specs=[pl.BlockSpec((tr, sw), lambda i: (i, 0)), pl.BlockSpec((tr, sw), lambda i: (i, 1)),
                            pl.BlockSpec((1, sw), lambda i: (0, 0)),
                            pl.BlockSpec((groups, 128, 128), lambda i: (0, 0, 0)),
                            pl.BlockSpec((128, 128), lambda i: (0, 0))],
                  out_specs=[pl.BlockSpec((tr, sw), lambda i: (i, 0))],
                  out_shape=[jax.ShapeDtypeStruct((t, sw), BF16)], args=(uv, uv, g, w, bst),
                  sem=("parallel",))[0][0]


def _sgu_bwd(uv, g, w, bst, dy, *, name):
    t = uv.shape[0]
    sw = uv.shape[1] // 2
    groups = sw // 128
    tr = _tile(t, 256)
    nsteps = t // tr

    def body(u_ref, v_ref, g_ref, w_ref, b_ref, dy_ref, duv_ref, dw_ref, db_ref, dg_ref, dvn_s, dgu_s):
        step = pl.program_id(0)

        @pl.when(step == 0)
        def _():
            dw_ref[...] = jnp.zeros_like(dw_ref)
            db_ref[...] = jnp.zeros_like(db_ref)
            dg_ref[...] = jnp.zeros_like(dg_ref)

        vf = v_ref[...].astype(F32)
        gv, gv_grad = _gelu_and_grad(vf)
        inv = lax.rsqrt(jnp.mean(gv * gv, axis=-1, keepdims=True) + RMS_EPS)
        xn = gv * inv
        gvec = g_ref[...]
        vn = (xn * gvec).astype(BF16)
        uf = u_ref[...].astype(F32)
        gu, gu_grad = _gelu_and_grad(uf)
        dyf = dy_ref[...].astype(F32)
        mask = _tril_mask()
        lane = lax.broadcasted_iota(jnp.int32, (128, 128), 1)
        dball = jnp.zeros((128, 128), F32)
        for gi in range(groups):
            wg = jnp.where(mask, w_ref[gi], 0.0).astype(BF16)
            wgt = wg.T
            bcol = b_ref[:, gi:gi + 1]
            cs = slice(gi * 128, (gi + 1) * 128)
            dwg = jnp.zeros((128, 128), F32)
            dbg = jnp.zeros((128, 1), F32)
            for ci in range(tr // 128):
                rs = slice(ci * 128, (ci + 1) * 128)
                vnb = vn[rs, cs]
                mixed = jnp.dot(wg, vnb, preferred_element_type=F32) + bcol
                dgu_s[rs, cs] = dyf[rs, cs] * mixed
                dmix = dyf[rs, cs] * gu[rs, cs]
                dmb = dmix.astype(BF16)
                dvn_s[rs, cs] = jnp.dot(wgt, dmb, preferred_element_type=F32)
                dwg = dwg + lax.dot_general(dmb, vnb, (((1,), (1,)), ((), ())), preferred_element_type=F32)
                dbg = dbg + jnp.sum(dmix, axis=1, keepdims=True)
            dw_ref[gi] += dwg
            dball = dball + jnp.where(lane == gi, dbg, 0.0)
        db_ref[...] += dball
        dvn = dvn_s[...]
        dg_ref[...] += jnp.sum(dvn * xn, axis=0, keepdims=True)
        dxn = dvn * gvec
        dgv = inv * (dxn - xn * jnp.mean(dxn * xn, axis=-1, keepdims=True))
        duv_ref[:, 0:sw] = (dgu_s[...] * gu_grad).astype(BF16)
        duv_ref[:, sw:2 * sw] = (dgv * gv_grad).astype(BF16)

        @pl.when(step == nsteps - 1)
        def _():
            for gi in range(groups):
                dw_ref[gi] = jnp.where(mask, dw_ref[gi], 0.0)

    return _pcall(body, name=name, grid=(nsteps,),
                  in_specs=[pl.BlockSpec((tr, sw), lambda i: (i, 0)), pl.BlockSpec((tr, sw), lambda i: (i, 1)),
                            pl.BlockSpec((1, sw), lambda i: (0, 0)),
                            pl.BlockSpec((groups, 128, 128), lambda i: (0, 0, 0)),
                            pl.BlockSpec((128, 128), lambda i: (0, 0)), pl.BlockSpec((tr, sw), lambda i: (i, 0))],
                  out_specs=[pl.BlockSpec((tr, 2 * sw), lambda i: (i, 0)),
                             pl.BlockSpec((groups, 128, 128), lambda i: (0, 0, 0)),
                             pl.BlockSpec((128, 128), lambda i: (0, 0)), pl.BlockSpec((1, sw), lambda i: (0, 0))],
                  out_shape=[jax.ShapeDtypeStruct((t, 2 * sw), BF16), jax.ShapeDtypeStruct((groups, 128, 128), F32),
                             jax.ShapeDtypeStruct((128, 128), F32), jax.ShapeDtypeStruct((1, sw), F32)],
                  args=(uv, uv, g, w, bst, dy), sem=("arbitrary",),
                  scratch=[pltpu.VMEM((tr, sw), F32), pltpu.VMEM((tr, sw), F32)])[0]


def _merge_fwd(ya, yb, wa, wb, gates, *, name, comm=None):
    t, kdim = ya.shape
    nsh, _, ns = wa.shape
    d = nsh * ns
    tm = _tile(t, 1024)

    def body(ya_ref, yb_ref, wa_ref, wb_ref, ga_ref, gb_ref, mg_ref, za_ref, zb_ref):
        za = jnp.dot(ya_ref[...], wa_ref[...], preferred_element_type=F32)
        zb = jnp.dot(yb_ref[...], wb_ref[...], preferred_element_type=F32)
        sa = _sigmoid(ga_ref[...].astype(F32))
        sb = _sigmoid(gb_ref[...].astype(F32))
        mg_ref[...] = (sa * za + sb * zb).astype(BF16)
        za_ref[...] = za.astype(BF16)
        zb_ref[...] = zb.astype(BF16)

    yspec = pl.BlockSpec((tm, kdim), lambda i, j: (i, 0))
    wspec = pl.BlockSpec((None, kdim, ns), lambda i, j: (j, 0, 0))
    ospec = pl.BlockSpec((tm, ns), lambda i, j: (i, j))
    outs, couts = _pcall(body, name=name, grid=(t // tm, nsh),
                         in_specs=[yspec, yspec, wspec, wspec, ospec, pl.BlockSpec((tm, ns), lambda i, j: (i, nsh + j))],
                         out_specs=[ospec, ospec, ospec], out_shape=[jax.ShapeDtypeStruct((t, d), BF16)] * 3,
                         args=(ya, yb, wa, wb, gates, gates), sem=("parallel", "parallel"), comm=comm)
    return outs, couts


def _merge_bwd(dmg, gates, za, zb, *, name):
    t, d = dmg.shape
    tr = _tile(t, 256)

    def body(dm_ref, ga_ref, gb_ref, za_ref, zb_ref, dza_ref, dzb_ref, dg_ref):
        dm = dm_ref[...].astype(F32)
        sa = _sigmoid(ga_ref[...].astype(F32))
        sb = _sigmoid(gb_ref[...].astype(F32))
        dza_ref[...] = (dm * sa).astype(BF16)
        dzb_ref[...] = (dm * sb).astype(BF16)
        dg_ref[:, 0:d] = (dm * za_ref[...].astype(F32) * (sa * (1.0 - sa))).astype(BF16)
        dg_ref[:, d:2 * d] = (dm * zb_ref[...].astype(F32) * (sb * (1.0 - sb))).astype(BF16)

    row = pl.BlockSpec((tr, d), lambda i: (i, 0))
    return _pcall(body, name=name, grid=(t // tr,),
                  in_specs=[row, row, pl.BlockSpec((tr, d), lambda i: (i, 1)), row, row],
                  out_specs=[row, row, pl.BlockSpec((tr, 2 * d), lambda i: (i, 0))],
                  out_shape=[jax.ShapeDtypeStruct((t, d), BF16), jax.ShapeDtypeStruct((t, d), BF16),
                             jax.ShapeDtypeStruct((t, 2 * d), BF16)],
                  args=(dmg, gates, gates, za, zb), sem=("parallel",))[0]


def _shift_down(ext, k, rows):
    return pltpu.roll(ext, k, 0)[8:8 + rows]


def _conv_fwd(ab, cw, cb, *, name):
    t = ab.shape[0]
    dff = ab.shape[1] // 2
    tr, tc = _tile(t, 512), _tile(dff, 512)
    nc = dff // tc
    r8 = tr // 8

    def body(a_ref, ap_ref, b_ref, cw_ref, cb_ref, g_ref, g1_ref, g2_ref):
        i = pl.program_id(0)
        prev = ap_ref[...].astype(F32) * jnp.where(i > 0, 1.0, 0.0)
        a = a_ref[...].astype(F32)
        ext = jnp.concatenate([prev, a], axis=0)
        acc = cb_ref[...] + cw_ref[0:1, :] * _shift_down(ext, 2, tr) + cw_ref[1:2, :] * _shift_down(ext, 1, tr) \
            + cw_ref[2:3, :] * a
        gel, gel_grad = _gelu_and_grad(acc)
        bf = b_ref[...].astype(F32)
        g_ref[...] = (gel * bf).astype(BF16)
        g1_ref[...] = gel.astype(BF16)
        g2_ref[...] = (bf * gel_grad).astype(BF16)

    ospec = pl.BlockSpec((tr, tc), lambda i, j: (i, j))
    return _pcall(body, name=name, grid=(t // tr, nc),
                  in_specs=[ospec, pl.BlockSpec((8, tc), lambda i, j: (jnp.maximum(i * r8 - 1, 0), j)),
                            pl.BlockSpec((tr, tc), lambda i, j: (i, nc + j)),
                            pl.BlockSpec((CONV_WIDTH, tc), lambda i, j: (0, j)),
                            pl.BlockSpec((1, tc), lambda i, j: (0, j))],
                  out_specs=[ospec, ospec, ospec],
                  out_shape=[jax.ShapeDtypeStruct((t, dff), BF16)] * 3, args=(ab, ab, ab, cw, cb),
                  sem=("parallel", "parallel"))[0]


def _conv_bwd(ab, g1, g2, cw, dgg, *, name, comm=None):
    t = ab.shape[0]
    dff = ab.shape[1] // 2
    tr, tc = _tile(t, 512), _tile(dff, 512)
    nc, nr = dff // tc, t // tr
    r8 = tr // 8
    ext_rows = tr + 8

    def body(a_ref, ap_ref, g1_ref, g2_ref, g2n_ref, dg_ref, dgn_ref, cw_ref, da_ref, db_ref, dcw_ref, dcb_ref):
        i = pl.program_id(1)
        has_prev = jnp.where(i > 0, 1.0, 0.0)
        has_next = jnp.where(i < nr - 1, 1.0, 0.0)
        dg = dg_ref[...].astype(F32)
        dacc = jnp.concatenate([dg * g2_ref[...].astype(F32),
                                dgn_ref[...].astype(F32) * g2n_ref[...].astype(F32) * has_next], axis=0)
        w0, w1, w2 = cw_ref[0:1, :], cw_ref[1:2, :], cw_ref[2:3, :]
        d_a = w2 * dacc + w1 * pltpu.roll(dacc, ext_rows - 1, 0) + w0 * pltpu.roll(dacc, ext_rows - 2, 0)
        da_ref[...] = d_a[0:tr].astype(BF16)
        db_ref[...] = (dg * g1_ref[...].astype(F32)).astype(BF16)
        dm = dacc[0:tr]
        a = a_ref[...].astype(F32)
        a_ext = jnp.concatenate([ap_ref[...].astype(F32) * has_prev, a], axis=0)
        dcw = jnp.concatenate([jnp.sum(dm * _shift_down(a_ext, 2, tr), axis=0, keepdims=True),
                               jnp.sum(dm * _shift_down(a_ext, 1, tr), axis=0, keepdims=True),
                               jnp.sum(dm * a, axis=0, keepdims=True)], axis=0)
        dcb = jnp.sum(dm, axis=0, keepdims=True)

        @pl.when(i == 0)
        def _():
            dcw_ref[...] = dcw
            dcb_ref[...] = dcb

        @pl.when(i > 0)
        def _():
            dcw_ref[...] += dcw
            dcb_ref[...] += dcb

    cur = lambda off: pl.BlockSpec((tr, tc), lambda j, i: (i, off + j))
    prv = lambda off: pl.BlockSpec((8, tc), lambda j, i: (jnp.maximum(i * r8 - 1, 0), off + j))
    nxt = lambda off: pl.BlockSpec((8, tc), lambda j, i: (jnp.minimum((i + 1) * r8, nr * r8 - 1), off + j))
    return _pcall(body, name=name, grid=(nc, nr),
                  in_specs=[cur(0), prv(0), cur(0), cur(0), nxt(0), cur(0), nxt(0),
                            pl.BlockSpec((CONV_WIDTH, tc), lambda j, i: (0, j))],
                  out_specs=[cur(0), cur(0), pl.BlockSpec((CONV_WIDTH, tc), lambda j, i: (0, j)),
                             pl.BlockSpec((1, tc), lambda j, i: (0, j))],
                  out_shape=[jax.ShapeDtypeStruct((t, dff), BF16), jax.ShapeDtypeStruct((t, dff), BF16),
                             jax.ShapeDtypeStruct((CONV_WIDTH, dff), F32), jax.ShapeDtypeStruct((1, dff), F32)],
                  args=(ab, ab, g1, g2, g2, dgg, dgg, cw), sem=("parallel", "arbitrary"), comm=comm)


class _InLayout:
    def __init__(self, nsi, heads, fw, sw, d):
        self.nsi = nsi
        self.pw = -(-(nsi + 127) // 128) * 128
        self.o_u = 3 * fw + heads
        self.ins = 128 - heads
        self.widths = [3 * fw, 128, 2 * sw, 2 * d]
        self.total = sum(self.widths)

    def regions(self, s):
        g0 = self.nsi * s
        out = []
        lo, hi = g0, min(g0 + self.nsi, self.o_u)
        if lo < hi:
            out.append((lo, hi, g0 % 128, g0 // 128, 0, hi - g0))
        lo, hi = max(g0, self.o_u), g0 + self.nsi
        if lo < hi:
            gi = g0 + self.ins
            out.append((lo + self.ins, hi + self.ins, gi % 128, gi // 128, lo - g0, self.nsi))
        return out


def _lane_mix(lane, pieces):
    val = None
    for piece, lo, hi in pieces:
        if lo <= 0 and hi >= 128:
            val = piece
        else:
            val = jnp.where((lane >= lo) & (lane < hi), piece, jnp.zeros_like(piece) if val is None else val)
    return val


def _w_in_unpack(g, lay, *, name):
    _, d, pw = g.shape
    tr = _tile(d, 128)
    ntiles = lay.total // 128
    plan = [[] for _ in range(ntiles)]
    for s in range(N_DEV):
        for r, (lo, hi, _, tile0, _, _) in enumerate(lay.regions(s)):
            for tt in range(lo // 128, (hi - 1) // 128 + 1):
                plan[tt].append((s, r, tt - tile0, lo - 128 * tt, hi - 128 * tt))
    bounds = np.cumsum([0] + [w // 128 for w in lay.widths])

    def body(g_ref, *o_refs):
        lane = lax.broadcasted_iota(jnp.int32, (tr, 128), 1)
        rolled = {}

        def src(s, r):
            if (s, r) not in rolled:
                shift = lay.regions(s)[r][2]
                xs = g_ref[s].astype(F32)
                rolled[(s, r)] = pltpu.roll(xs, shift, 1) if shift else xs
            return rolled[(s, r)]

        for tt in range(ntiles):
            val = _lane_mix(lane, [(src(s, r)[:, 128 * st:128 * (st + 1)], lo, hi) for s, r, st, lo, hi in plan[tt]])
            o = int(np.searchsorted(bounds, tt, side="right")) - 1
            lt = tt - int(bounds[o])
            o_refs[o][:, 128 * lt:128 * (lt + 1)] = val.astype(BF16)

    return _pcall(body, name=name, grid=(d // tr,), in_specs=[pl.BlockSpec((N_DEV, tr, pw), lambda i: (0, i, 0))],
                  out_specs=[pl.BlockSpec((tr, w), lambda i: (i, 0)) for w in lay.widths],
                  out_shape=[jax.ShapeDtypeStruct((d, w), BF16) for w in lay.widths], args=(g,),
                  sem=("parallel",))[0]


def _w_in_grad_pack(pieces, lay, *, name):
    d = pieces[0].shape[0]
    tr = _tile(d, 128)
    assert sum(p.shape[1] for p in pieces) == lay.total
    starts = np.cumsum([0] + [p.shape[1] // 128 for p in pieces])
    ntot = lay.total // 128
    nfull, rem = lay.nsi // 128, lay.nsi % 128
    npc = len(pieces)

    def body(*refs):
        p_refs, o_ref = refs[:npc], refs[npc]
        lane = lax.broadcasted_iota(jnp.int32, (tr, 128), 1)

        def padded_tile(tau):
            pi = int(np.searchsorted(starts, tau, side="right")) - 1
            lt = tau - int(starts[pi])
            return p_refs[pi][:, 128 * lt:128 * (lt + 1)].astype(F32)

        for s in range(N_DEV):
            unrolled = []
            for _, _, shift, tile0, j_lo, j_hi in lay.regions(s):
                win = jnp.concatenate([padded_tile(min(tile0 + q, ntot - 1)) for q in range(lay.pw // 128)], axis=1)
                unrolled.append((pltpu.roll(win, lay.pw - shift, 1) if shift else win, j_lo, j_hi))
            for kt in range(nfull + (1 if rem else 0)):
                val = _lane_mix(lane, [(u[:, 128 * kt:128 * (kt + 1)], j_lo - 128 * kt, j_hi - 128 * kt)
                                       for u, j_lo, j_hi in unrolled if j_lo < 128 * (kt + 1) and j_hi > 128 * kt])
                if kt < nfull:
                    o_ref[s, :, 128 * kt:128 * (kt + 1)] = val.astype(BF16)
                else:
                    o_ref[s, :, 128 * nfull:lay.nsi] = val[:, :rem].astype(BF16)

    return _pcall(body, name=name, grid=(d // tr,),
                  in_specs=[pl.BlockSpec((tr, p.shape[1]), lambda i: (i, 0)) for p in pieces],
                  out_specs=[pl.BlockSpec((N_DEV, tr, lay.nsi), lambda i: (0, i, 0))],
                  out_shape=[jax.ShapeDtypeStruct((N_DEV, d, lay.nsi), BF16)], args=tuple(pieces),
                  sem=("parallel",))[0][0]


def _row_tile(rows, cols):
    for cand in (512, 256, 128, 64, 32, 16, 8):
        if rows % cand == 0 and cand * cols * 4 <= 2 * 1024 * 1024:
            return cand
    return rows


def _pair_add(grad8, recv4, cidx, *, name):
    _, rows, cols = grad8.shape
    tr = _row_tile(rows, cols)

    def body(c_ref, g_ref, r_ref, o_ref):
        o_ref[...] = (g_ref[...].astype(F32) + r_ref[...].astype(F32)).astype(BF16)

    blk = (None, tr, cols)
    return _pcall(body, name=name, grid=(4, rows // tr), prefetch=1,
                  in_specs=[pl.BlockSpec(blk, lambda k, i, c_ref: (2 * k + c_ref[0], i, 0)),
                            pl.BlockSpec(blk, lambda k, i, c_ref: (k, i, 0))],
                  out_specs=[pl.BlockSpec(blk, lambda k, i, c_ref: (k, i, 0))],
                  out_shape=[jax.ShapeDtypeStruct((4, rows, cols), BF16)], args=(cidx, grad8, recv4),
                  sem=("parallel", "parallel"))[0][0]


def _adamw_math(w, g, m, v):
    m = ADAM_B1 * m + (1.0 - ADAM_B1) * g
    v = ADAM_B2 * v + (1.0 - ADAM_B2) * (g * g)
    m_hat = m / (1.0 - ADAM_B1 ** ADAM_STEP)
    v_hat = v / (1.0 - ADAM_B2 ** ADAM_STEP)
    delta = -ADAM_LR * (m_hat / (jnp.sqrt(v_hat) + ADAM_EPS) + ADAM_WD * w)
    return delta, m, v


def _shard_adamw(part4, recv3, w, m, v, kidx, layer, bufs, *, name):
    depth, rows, cols = w.shape
    tr = _row_tile(rows, cols)

    def body(k_ref, p_ref, r0_ref, r1_ref, r2_ref, w_ref, m_ref, v_ref, *rest):
        g_out, d_out, m_out, v_out = rest[-4:]
        g = ((p_ref[...].astype(F32) + r0_ref[...].astype(F32)) + r1_ref[...].astype(F32)) + r2_ref[...].astype(F32)
        delta, mn, vn = _adamw_math(w_ref[...], g, m_ref[...], v_ref[...])
        g_out[...] = g
        d_out[...] = delta
        m_out[...] = mn
        v_out[...] = vn

    blk = (None, tr, cols)
    rspec = lambda j: pl.BlockSpec(blk, lambda i, k_ref: (j, i, 0))
    espec = pl.BlockSpec(blk, lambda i, k_ref: (layer, i, 0))
    in_specs = [pl.BlockSpec(blk, lambda i, k_ref: (k_ref[0], i, 0)), rspec(0), rspec(1), rspec(2), espec, espec, espec]
    args = [kidx, part4, recv3, recv3, recv3, w, m, v]
    aliases = {}
    if bufs is not None:
        in_specs += [ANY] * 4
        aliases = {8 + q: q for q in range(4)}
        args += list(bufs)
    return _pcall(body, name=name, grid=(rows // tr,), prefetch=1, in_specs=in_specs, out_specs=[espec] * 4,
                  out_shape=[jax.ShapeDtypeStruct((depth, rows, cols), F32)] * 4, args=args, sem=("parallel",),
                  aliases=aliases)[0]


def _small_adamw(gath, w, m, v, *, name):
    rows = w.shape[0]

    def body(g_ref, w_ref, m_ref, v_ref, g_out, d_out, m_out, v_out):
        g = g_ref[0]
        for dev in range(1, N_DEV):
            g = g + g_ref[dev]
        delta, mn, vn = _adamw_math(w_ref[...], g, m_ref[...], v_ref[...])
        g_out[...] = g
        d_out[...] = delta
        m_out[...] = mn
        v_out[...] = vn

    tr = _row_tile(rows, 128 * N_DEV)
    espec = pl.BlockSpec((tr, 128), lambda i: (i, 0))
    return _pcall(body, name=name, grid=(rows // tr,),
                  in_specs=[pl.BlockSpec((N_DEV, tr, 128), lambda i: (0, i, 0)), espec, espec, espec],
                  out_specs=[espec] * 4, out_shape=[jax.ShapeDtypeStruct((rows, 128), F32)] * 4,
                  args=(gath, w, m, v), sem=("parallel",))[0]


def _pack(arrs):
    flat = jnp.concatenate([a.reshape(-1) for a in arrs])
    total = flat.shape[0]
    rows = -(-total // (128 * 64)) * 64
    return jnp.pad(flat, (0, rows * 128 - total)).reshape(rows, 128)


def _unpack(packed, like):
    flat = packed.reshape(-1)
    out, off = [], 0
    for a in like:
        out.append(flat[off:off + a.size].reshape(a.shape))
        off += a.size
    return out


def kernel(x, g_mix, w_in, b_forget, g_sgu, w_spatial, b_spatial, w_branch_a, w_branch_b, w_out, g_ffn, w_up, conv_w, conv_b, w_down, g_final, loss_target, m_g_mix, m_w_in, m_b_forget, m_g_sgu, m_w_spatial, m_b_spatial, m_w_branch_a, m_w_branch_b, m_w_out, m_g_ffn, m_w_up, m_conv_w, m_conv_b, m_w_down, m_g_final, v_g_mix, v_w_in, v_b_forget, v_g_sgu, v_w_spatial, v_b_spatial, v_w_branch_a, v_w_branch_b, v_w_out, v_g_ffn, v_w_up, v_conv_w, v_conv_b, v_w_down, v_g_final):
    depth, d = g_mix.shape
    heads = b_forget.shape[1]
    fw = heads * HEAD_DIM
    sw = g_sgu.shape[1]
    dff = conv_b.shape[1]
    t = x.shape[1]
    nsi = w_in.shape[2]
    nsu = w_up.shape[2]
    o_f, o_u, o_g = 3 * fw, 3 * fw + heads, 3 * fw + heads + 2 * sw

    bpad = jnp.pad(b_forget, ((0, 0), (0, 128 - heads)))
    bst = jnp.pad(jnp.swapaxes(b_spatial, 1, 2), ((0, 0), (0, 0), (0, 128 - b_spatial.shape[1])))

    lay = _InLayout(nsi, heads, fw, sw, d)

    def in_shards(l):
        return [jnp.pad(w_in[l].astype(BF16), ((0, 0), (0, lay.pw - nsi)))]

    def small_shards(l):
        return [w_branch_a[l].astype(BF16), w_branch_b[l].astype(BF16), w_out[l].astype(BF16), conv_w[l]]

    def ffn_shards(l):
        return [w_up[l].astype(BF16), w_down[l].astype(BF16)]

    def unpack_in(bufs, l):
        wqkv, wf, wuv, wg = _w_in_unpack(bufs[0], lay, name=f"unpack_w_in_{l}")
        return dict(wqkv=wqkv, wf=wf, wuv=wuv, wg=wg)

    def unpack_small(bufs):
        g_wa, g_wb, g_wo, g_cw = bufs
        plain = lambda g: jnp.moveaxis(g, 0, 1).reshape(g.shape[1], d)
        return dict(wa=g_wa, wb=g_wb, wa_t=plain(g_wa), wb_t=plain(g_wb), wo=g_wo.reshape(d, d),
                    cw=jnp.moveaxis(g_cw, 0, 1).reshape(CONV_WIDTH, dff))

    mixer_w = [None] * depth
    ffn_w = [None] * depth
    first = _comm_only(_gather_first(in_shards(0)), name="gather_in_first_0")
    mixer_w[0] = unpack_in(_comm_only(_gather_second(first), name="gather_in_second_0"), 0)
    small_first = None

    xs = x[0]
    saved = []
    for l in range(depth):
        n = lambda s: f"{s}_{l}"
        mw = mixer_w[l]
        h = _rms_fwd(xs, g_mix[l][None], name=n("rms_mix"))
        if small_first is None:
            qkv, small_first = _mm(h, mw["wqkv"], name=n("proj_qkv"), comm=_gather_first(small_shards(l)))
        else:
            qkv = _mm(h, mw["wqkv"], name=n("proj_qkv"))
        uv = _mm(h, mw["wuv"], name=n("proj_uv"))
        gates, small_bufs = _mm(h, mw["wg"], name=n("proj_gates"), comm=_gather_second(small_first))
        mw.update(unpack_small(small_bufs))
        flog = _mm(h, mw["wf"], out_dtype=F32, name=n("proj_forget"))
        c3 = _forget_fwd(flog, bpad[l][None], name=n("forget_fwd")).reshape(128, 1, t)
        (ya, ya32, lse), ffn_first = _fox_fwd(qkv, c3, heads, name=n("fox_fwd"), comm=_gather_first(ffn_shards(l)))
        yb = _sgu_fwd(uv, g_sgu[l][None], w_spatial[l], bst[l], name=n("sgu_fwd"))
        (merged, za, zb), _ = _merge_fwd(ya, yb, mw["wa"], mw["wb"], gates, name=n("merge_fwd"))
        x1, ffn_bufs = _mm(merged, mw["wo"], out_dtype=F32, res=xs, name=n("out_proj"), comm=_gather_second(ffn_first))
        g_wu, g_wd = ffn_bufs
        ffn_w[l] = dict(wu=g_wu, wd=g_wd.reshape(dff, d))
        h2 = _rms_fwd(x1, g_ffn[l][None], name=n("rms_ffn"))
        if l + 1 < depth:
            ab, in_first = _mm(h2, g_wu, b_cols=True, name=n("ffn_up"), comm=_gather_first(in_shards(l + 1)))
        else:
            ab = _mm(h2, g_wu, b_cols=True, name=n("ffn_up"))
        gg, g1, g2 = _conv_fwd(ab, mw["cw"], conv_b[l][None], name=n("conv_fwd"))
        if l + 1 < depth:
            both = _Join([_gather_second(in_first), _gather_first(small_shards(l + 1))])
            x2, nxt = _mm(gg, ffn_w[l]["wd"], out_dtype=F32, res=x1, name=n("ffn_down"), comm=both)
            in_bufs, small_first = both.split(nxt)
            mixer_w[l + 1] = unpack_in(in_bufs, l + 1)
        else:
            x2 = _mm(gg, ffn_w[l]["wd"], out_dtype=F32, res=x1, name=n("ffn_down"))
        saved.append((xs, h, qkv, uv, gates, flog, c3, ya, ya32, lse, yb, merged, za, zb, x1, h2, ab, gg, g1, g2))
        xs = x2

    loss_row, dx, dxb, d_g_final = _final_loss(xs, g_final[None], loss_target[0], name="final_loss")

    cidx = lax.axis_index("c").astype(jnp.int32).reshape(1)
    kidx = (2 * lax.axis_index("x") + lax.axis_index("y")).astype(jnp.int32).reshape(1)
    small_g = {k: [None] * depth for k in ("g_mix", "b_forget", "g_sgu", "w_spatial", "b_spatial", "g_ffn", "conv_b")}
    parts = {}
    recvs = {}
    mixer_names = ["w_in", "w_branch_a", "w_branch_b", "w_out"]
    ffn_names = ["w_up", "w_down", "conv_w"]
    for nm in mixer_names + ffn_names:
        parts[nm], recvs[nm] = [None] * depth, [None] * depth
    pending_mixer = None
    for l in reversed(range(depth)):
        n = lambda s: f"{s}_{l}"
        mw, fw_ = mixer_w[l], ffn_w[l]
        xs, h, qkv, uv, gates, flog, c3, ya, ya32, lse, yb, merged, za, zb, x1, h2, ab, gg, g1, g2 = saved[l]
        g_wd = _mm(gg, dxb, ta=True, name=n("d_w_down"))
        if pending_mixer is not None:
            dgg, got = _mm(dxb, fw_["wd"], tb=True, name=n("d_gg"), comm=_scatter_second(pending_mixer[1:]))
            for nm, r in zip(mixer_names[1:], got):
                recvs[nm][l + 1] = r
            half = d // 2
            (d_a, d_b, d_cw, d_cb), got = _conv_bwd(ab, g1, g2, mw["cw"], dgg, name=n("conv_bwd"),
                                                    comm=_scatter_second(pending_mixer[:1], rows=(0, half)))
            g_wu, got = _mm(h2, d_a, ta=True, out_cols=(N_DEV, 0, nsu), name=n("d_w_up_a"),
                            comm=_scatter_second(pending_mixer[:1], rows=(half, d - half), bufs=got))
            recvs[mixer_names[0]][l + 1] = got[0]
        else:
            dgg = _mm(dxb, fw_["wd"], tb=True, name=n("d_gg"))
            (d_a, d_b, d_cw, d_cb), _ = _conv_bwd(ab, g1, g2, mw["cw"], dgg, name=n("conv_bwd"))
            g_wu = _mm(h2, d_a, ta=True, out_cols=(N_DEV, 0, nsu), name=n("d_w_up_a"))
        small_g["conv_b"][l] = d_cb[0]
        g_wu = _mm(h2, d_b, ta=True, out_cols=(N_DEV, N_DEV // 2, nsu), out_alias=g_wu, name=n("d_w_up_b"))
        ffn_grads = [g_wu, g_wd.reshape(N_DEV, dff // N_DEV, d),
                     jnp.moveaxis(d_cw.reshape(CONV_WIDTH, N_DEV, dff // N_DEV), 1, 0)]
        dh2 = _mm(d_a, fw_["wu"], tb=True, b_cols=True, out_dtype=F32, name=n("d_h2_a"))
        dh2, from_sib = _mm(d_b, fw_["wu"], tb=True, b_cols=True, b_off=N_DEV // 2, out_dtype=F32, res=dh2,
                            name=n("d_h2_b"), comm=_scatter_first(ffn_grads))
        ffn_parts = [_pair_add(g8, r4, cidx, name=n(f"pair_add_{nm}")) for nm, g8, r4 in zip(ffn_names, ffn_grads, from_sib)]
        for nm, p in zip(ffn_names, ffn_parts):
            parts[nm][l] = p
        dx, dxb, dg = _rms_bwd(x1, g_ffn[l][None], dh2, dx, name=n("rms_ffn_bwd"))
        small_g["g_ffn"][l] = dg[0]
        g_wo = _mm(merged, dxb, ta=True, name=n("d_w_out"))
        dmg = _mm(dxb, mw["wo"], tb=True, name=n("d_merged"))
        dza, dzb, dgates = _merge_bwd(dmg, gates, za, zb, name=n("merge_bwd"))
        g_wa = _mm(ya, dza, ta=True, out_cols=(N_DEV, 0, d // N_DEV), name=n("d_w_a"))
        g_wb = _mm(yb, dzb, ta=True, out_cols=(N_DEV, 0, d // N_DEV), name=n("d_w_b"))
        dya = _mm(dza, mw["wa_t"], tb=True, name=n("d_ya"))
        dyb = _mm(dzb, mw["wb_t"], tb=True, name=n("d_yb"))
        duv, d_ws, d_bst, d_gs = _sgu_bwd(uv, g_sgu[l][None], w_spatial[l], bst[l], dyb, name=n("sgu_bwd"))
        small_g["w_spatial"][l], small_g["g_sgu"][l] = d_ws, d_gs[0]
        small_g["b_spatial"][l] = d_bst[:, :b_spatial.shape[1]].T
        (dq, dk, dv, dc3), got = _fox_bwd(qkv, c3, ya32, dya, lse, heads, name=n("fox_bwd"),
                                          comm=_scatter_second(ffn_parts))
        for nm, r in zip(ffn_names, got):
            recvs[nm][l] = r
        dct = jnp.pad(dc3.reshape(heads, t), ((0, 128 - heads), (0, 0)))
        dflog, d_bf = _forget_bwd(flog, bpad[l][None], dct, name=n("forget_bwd"))
        small_g["b_forget"][l] = d_bf[0, :heads]
        gw = [_mm(h, dq, ta=True, name=n("d_w_q")), _mm(h, dk, ta=True, name=n("d_w_k")),
              _mm(h, dv, ta=True, name=n("d_w_v")), _mm(h, dflog, ta=True, name=n("d_w_forget")),
              _mm(h, duv, ta=True, name=n("d_w_uv")), _mm(h, dgates, ta=True, name=n("d_w_gates"))]
        g_in = _w_in_grad_pack(gw, lay, name=n("pack_d_w_in"))
        mixer_grads = [g_in, g_wa, g_wb, g_wo.reshape(N_DEV, d // N_DEV, d)]
        dh = _mm(dflog, mw["wf"], tb=True, out_dtype=F32, name=n("d_h_forget"))
        dh = _mm(dq, mw["wqkv"], tb=True, out_dtype=F32, res=dh, name=n("d_h_q"))
        dh = _mm(dk, mw["wqkv"], tb=True, b_off=fw, out_dtype=F32, res=dh, name=n("d_h_k"))
        dh = _mm(dv, mw["wqkv"], tb=True, b_off=2 * fw, out_dtype=F32, res=dh, name=n("d_h_v"))
        dh = _mm(duv, mw["wuv"], tb=True, out_dtype=F32, res=dh, name=n("d_h_uv"))
        dh, from_sib = _mm(dgates, mw["wg"], tb=True, out_dtype=F32, res=dh, name=n("d_h_gates"),
                           comm=_scatter_first(mixer_grads))
        pending_mixer = [_pair_add(g8, r4, cidx, name=n(f"pair_add_{nm}"))
                         for nm, g8, r4 in zip(mixer_names, mixer_grads, from_sib)]
        for nm, p in zip(mixer_names, pending_mixer):
            parts[nm][l] = p
        dx, dxb, dg = _rms_bwd(xs, g_mix[l][None], dh, dx, name=n("rms_mix_bwd"))
        small_g["g_mix"][l] = dg[0]
    grad_x = dx[None]
    for nm, r in zip(mixer_names, _comm_only(_scatter_second(pending_mixer), name="scatter_mixer_second_0")):
        recvs[nm][0] = r

    weights = {"w_in": (w_in, m_w_in, v_w_in), "w_branch_a": (w_branch_a, m_w_branch_a, v_w_branch_a),
               "w_branch_b": (w_branch_b, m_w_branch_b, v_w_branch_b), "w_out": (w_out, m_w_out, v_w_out),
               "w_up": (w_up, m_w_up, v_w_up), "conv_w": (conv_w, m_conv_w, v_conv_w),
               "w_down": (w_down, m_w_down, v_w_down)}
    res = {}
    for nm, (w, m, v) in weights.items():
        bufs = None
        for l in range(depth):
            bufs = _shard_adamw(parts[nm][l], recvs[nm][l], w, m, v, kidx, l, bufs, name=f"adamw_{nm}_{l}")
        res[nm] = bufs

    small = ["g_mix", "b_forget", "g_sgu", "w_spatial", "b_spatial", "g_ffn", "conv_b", "g_final", "loss"]
    zero = jnp.zeros((1,), F32)
    small_w = [g_mix, b_forget, g_sgu, w_spatial, b_spatial, g_ffn, conv_b, g_final, zero]
    small_m = [m_g_mix, m_b_forget, m_g_sgu, m_w_spatial, m_b_spatial, m_g_ffn, m_conv_b, m_g_final, zero]
    small_v = [v_g_mix, v_b_forget, v_g_sgu, v_w_spatial, v_b_spatial, v_g_ffn, v_conv_b, v_g_final, zero]
    small_grads = [jnp.stack(small_g[nm]) for nm in small[:-2]] + [d_g_final[0], loss_row[0, 0:1]]
    first = _comm_only(_gather_first([_pack(small_grads)]), name="gather_small_first")
    (gath,) = _comm_only(_gather_second(first), name="gather_small_second")
    outs = _small_adamw(gath, _pack(small_w), _pack(small_m), _pack(small_v), name="adamw_replicated")
    for nm, vals in zip(small, zip(*[_unpack(o, small_w) for o in outs])):
        res[nm] = list(vals)
    loss = res["loss"][0][0]

    order = ["g_mix", "w_in", "b_forget", "g_sgu", "w_spatial", "b_spatial", "w_branch_a", "w_branch_b", "w_out",
             "g_ffn", "w_up", "conv_w", "conv_b", "w_down", "g_final"]
    return (loss, grad_x, *[res[nm][0] for nm in order], *[res[nm][1] for nm in order],
            *[res[nm][2] for nm in order], *[res[nm][3] for nm in order])
```

```python
import numpy as np

import jax
import jax.numpy as jnp
from jax import lax
from jax.experimental import pallas as pl
from jax.experimental.pallas import tpu as pltpu

F32 = jnp.float32
BF16 = jnp.bfloat16

RMS_EPS = 1e-6
HEAD_DIM = 128
CONV_WIDTH = 3
ADAM_LR = 0.001
ADAM_B1 = 0.9
ADAM_B2 = 0.999
ADAM_EPS = 1e-08
ADAM_WD = 0.01
ADAM_STEP = 10
N_DEV = 8
V7X_VMEM_LIMIT = 58 * 1024 * 1024
MM_VMEM_BUDGET = 46 * 1024 * 1024
NEG = -1e30
ANY = pl.BlockSpec(memory_space=pl.ANY)
MESH = pl.DeviceIdType.MESH


def _tile(dim, pref):
    for t in (2048, 1024, 512, 256, 128):
        if t <= pref and dim % t == 0:
            return t
    return dim


def _gelu(x):
    t = jnp.tanh(0.7978845608028654 * (x + 0.044715 * (x * x * x)))
    return x * (0.5 * (1.0 + t))


def _gelu_and_grad(x):
    x2 = x * x
    t = jnp.tanh(0.7978845608028654 * (x + 0.044715 * (x2 * x)))
    cdf = 0.5 * (1.0 + t)
    dt = (1.0 - t * t) * (0.7978845608028654 * (1.0 + 3.0 * 0.044715 * x2))
    return x * cdf, cdf + 0.5 * x * dt


def _sigmoid(x):
    return 1.0 / (1.0 + jnp.exp(-x))


class _Comm:
    def __init__(self, srcs, new, alias, n_copies, emit):
        self.srcs = list(srcs)
        self.new = list(new)
        self.alias = list(alias)
        self.n_copies = n_copies
        self.emit = emit

    def split(self, couts):
        return [couts]


class _Join(_Comm):
    def __init__(self, comms):
        self.comms = comms
        srcs = [s for cm in comms for s in cm.srcs]
        new = [s for cm in comms for s in cm.new]
        alias = [s for cm in comms for s in cm.alias]

        def emit(src_refs, new_refs, alias_refs, sems):
            copies, s0, n0, a0, k0 = [], 0, 0, 0, sems[2]
            for cm in comms:
                copies += cm.emit(src_refs[s0:s0 + len(cm.srcs)], new_refs[n0:n0 + len(cm.new)],
                                  alias_refs[a0:a0 + len(cm.alias)], (sems[0], sems[1], k0))
                s0, n0, a0, k0 = s0 + len(cm.srcs), n0 + len(cm.new), a0 + len(cm.alias), k0 + cm.n_copies
            return copies

        super().__init__(srcs, new, alias, sum(cm.n_copies for cm in comms), emit)

    def split(self, couts):
        n_new = len(self.new)
        out, n0, a0 = [], 0, 0
        for cm in self.comms:
            out.append(couts[n0:n0 + len(cm.new)] + couts[n_new + a0:n_new + a0 + len(cm.alias)])
            n0, a0 = n0 + len(cm.new), a0 + len(cm.alias)
        return out


def _place():
    x, y, c = lax.axis_index("x"), lax.axis_index("y"), lax.axis_index("c")
    chips = [(1 - x, y), (x, 1 - y), (1 - x, 1 - y)]
    return x, y, c, chips


def _remote(src, dst, sems, k, to):
    return pltpu.make_async_remote_copy(src_ref=src, dst_ref=dst, send_sem=sems[0].at[sems[2] + k],
                                        recv_sem=sems[1].at[sems[2] + k], device_id=to, device_id_type=MESH)


def _gather_first(shards):
    n = len(shards)

    def emit(srcs, new, alias, sems):
        x, y, c, chips = _place()
        me = 4 * x + 2 * y + c
        copies = []
        for a in range(n):
            copies.append(pltpu.make_async_copy(srcs[a], new[a].at[me], sems[0].at[sems[2] + 5 * a + 4]))
            copies.append(_remote(srcs[a], new[a].at[me], sems, 5 * a, (x, y, 1 - c)))
            for j, chip in enumerate(chips):
                copies.append(_remote(srcs[a], new[a].at[me], sems, 5 * a + 1 + j, (*chip, c)))
        return copies

    new = [jax.ShapeDtypeStruct((N_DEV,) + s.shape, s.dtype) for s in shards]
    return _Comm(shards, new, [], 5 * n, emit)


def _gather_second(bufs):
    n = len(bufs)

    def emit(srcs, new, alias, sems):
        x, y, c, chips = _place()
        copies = []
        for a in range(n):
            for j, chip in enumerate(chips):
                blk = alias[a].at[4 * chip[0] + 2 * chip[1] + c]
                copies.append(_remote(blk, blk, sems, 3 * a + j, (x, y, 1 - c)))
        return copies

    return _Comm([], [], bufs, 3 * n, emit)


def _scatter_first(grads):
    n = len(grads)

    def emit(srcs, new, alias, sems):
        x, y, c, _ = _place()
        return [_remote(srcs[a].at[2 * k + 1 - c], new[a].at[k], sems, 4 * a + k, (x, y, 1 - c))
                for a in range(n) for k in range(4)]

    new = [jax.ShapeDtypeStruct((4,) + g.shape[1:], g.dtype) for g in grads]
    return _Comm(grads, new, [], 4 * n, emit)


def _scatter_second(parts, rows=None, bufs=None):
    n = len(parts)

    def emit(srcs, new, alias, sems):
        x, y, c, chips = _place()
        dsts = new if bufs is None else alias
        win = (lambda r, idx: r.at[idx]) if rows is None else (lambda r, idx: r.at[idx, pl.ds(rows[0], rows[1])])
        return [_remote(win(srcs[a], 2 * chip[0] + chip[1]), win(dsts[a], j), sems, 3 * a + j, (*chip, c))
                for a in range(n) for j, chip in enumerate(chips)]

    new = [jax.ShapeDtypeStruct((3,) + p.shape[1:], p.dtype) for p in parts] if bufs is None else []
    return _Comm(parts, new, [] if bufs is None else bufs, 3 * n, emit)


def _pcall(body, *, name, grid, in_specs, out_specs, out_shape, args, sem, scratch=(), aliases=None, comm=None,
           prefetch=0):
    in_specs, out_specs, out_shape, scratch = list(in_specs), list(out_specs), list(out_shape), list(scratch)
    aliases = dict(aliases or {})
    n_in, n_out, n_scr = len(in_specs), len(out_shape), len(scratch)

    def make(body_fn, ins, outs, shapes, scr, sem_):
        params = pltpu.CompilerParams(dimension_semantics=sem_, vmem_limit_bytes=V7X_VMEM_LIMIT)
        if prefetch:
            spec = pltpu.PrefetchScalarGridSpec(num_scalar_prefetch=prefetch, grid=grid, in_specs=ins,
                                                out_specs=outs, scratch_shapes=scr)
            return pl.pallas_call(body_fn, name=name, grid_spec=spec, out_shape=shapes,
                                  input_output_aliases=aliases, compiler_params=params)
        return pl.pallas_call(body_fn, name=name, grid=grid, in_specs=ins, out_specs=outs, out_shape=shapes,
                              scratch_shapes=scr, input_output_aliases=aliases, compiler_params=params)

    if comm is None:
        return list(make(body, in_specs, out_specs, out_shape, scratch, sem)(*args)), []

    n_src, n_new, n_al = len(comm.srcs), len(comm.new), len(comm.alias)
    for a in range(n_al):
        aliases[prefetch + n_in + n_src + a] = n_out + n_new + a

    def wrapped(*refs):
        pre, refs = refs[:prefetch], refs[prefetch:]
        ins = refs[:n_in]
        src_refs = refs[n_in:n_in + n_src]
        o0 = n_in + n_src + n_al
        outs = refs[o0:o0 + n_out]
        new_refs = refs[o0 + n_out:o0 + n_out + n_new]
        alias_refs = refs[o0 + n_out + n_new:o0 + n_out + n_new + n_al]
        s0 = o0 + n_out + n_new + n_al
        scr = refs[s0:s0 + n_scr]
        send_sems, recv_sems = refs[s0 + n_scr], refs[s0 + n_scr + 1]
        first = pl.program_id(0) == 0
        last = pl.program_id(0) == grid[0] - 1
        for dim in range(1, len(grid)):
            first = first & (pl.program_id(dim) == 0)
            last = last & (pl.program_id(dim) == grid[dim] - 1)

        @pl.when(first)
        def _():
            for cp in comm.emit(src_refs, new_refs, alias_refs, (send_sems, recv_sems, 0)):
                cp.start()

        body(*pre, *ins, *outs, *scr)

        @pl.when(last)
        def _():
            for cp in comm.emit(src_refs, new_refs, alias_refs, (send_sems, recv_sems, 0)):
                cp.wait()

    call = make(wrapped, in_specs + [ANY] * (n_src + n_al), out_specs + [ANY] * (n_new + n_al),
                out_shape + comm.new + [jax.ShapeDtypeStruct(b.shape, b.dtype) for b in comm.alias],
                scratch + [pltpu.SemaphoreType.DMA((comm.n_copies,)), pltpu.SemaphoreType.DMA((comm.n_copies,))],
                ("arbitrary",) * len(grid))
    res = list(call(*args, *comm.srcs, *comm.alias))
    return res[:n_out], res[n_out:]


def _comm_only(comm, *, name):
    def body(o_ref):
        o_ref[...] = jnp.zeros_like(o_ref)

    _, couts = _pcall(body, name=name, grid=(1,), in_specs=[], out_specs=[pl.BlockSpec((8, 128), lambda i: (0, 0))],
                      out_shape=[jax.ShapeDtypeStruct((8, 128), F32)], args=(), sem=("arbitrary",), comm=comm)
    return couts


def _divisor_tiles(dim, cap):
    tiles = [t for t in range(128, min(dim, cap) + 1, 128) if dim % t == 0]
    return sorted(tiles, reverse=True) or [dim]


def _mm_tiles(m, n, k, obytes, has_res, tn_fixed=None, tk_fixed=None):
    tms = _divisor_tiles(m, 1408)
    tns = [tn_fixed] if tn_fixed else _divisor_tiles(n, 1408)
    tks = [tk_fixed] if tk_fixed else [k] + [tt for tt in _divisor_tiles(k, 2048) if tt != k]
    best, best_score = None, None
    for tk in tks:
        for tm in tms:
            for tn in tns:
                nk = k // tk
                use = 4 * tm * tk + 4 * tk * tn + 2 * tm * tn * obytes
                use += (8 * tm * tn if has_res else 0) + (4 * tm * tn if nk > 1 else 0)
                score = (nk == 1, min(tm, 1024), tn, tm, tk)
                if use <= MM_VMEM_BUDGET and (best is None or score > best_score):
                    best, best_score = (tm, tn, tk), score
    assert best is not None, (m, n, k)
    return best


def _mm(a, b, *, ta=False, tb=False, out_dtype=BF16, res=None, name, b_cols=False, b_off=0,
        out_cols=None, out_alias=None, comm=None):
    m, k = (a.shape[1], a.shape[0]) if ta else a.shape
    obytes = jnp.dtype(out_dtype).itemsize
    if b_cols and not tb:
        ns = b.shape[2]
        assert b.shape[1] == k
        n = b.shape[0] * ns
        tm, tn, tk = _mm_tiles(m, n, k, obytes, res is not None, tn_fixed=ns)
        b_spec = pl.BlockSpec((None, tk, ns), lambda i, j, kk: (j, kk, 0))
    elif b_cols:
        ns = b.shape[2]
        assert k % ns == 0
        n = b.shape[1]
        tm, tn, tk = _mm_tiles(m, n, k, obytes, res is not None, tk_fixed=ns)
        b_spec = pl.BlockSpec((None, tn, ns), lambda i, j, kk: (b_off + kk, j, 0))
    else:
        n = b.shape[0] if tb else b.shape[1]
        assert (b.shape[1] >= b_off + k) if tb else (b.shape[0] == k and b_off == 0)
        tm, tn, tk = _mm_tiles(m, n, k, obytes, res is not None, tn_fixed=out_cols[2] if out_cols is not None else None)
        assert b_off % tk == 0
        k0 = b_off // tk
        b_spec = pl.BlockSpec((tn, tk), lambda i, j, kk: (j, k0 + kk)) if tb else pl.BlockSpec((tk, tn), lambda i, j, kk: (kk, j))
    nk = k // tk
    dn = (((0,) if ta else (1,), (1,) if tb else (0,)), ((), ()))
    n_extra = (res is not None) + (out_alias is not None)

    def body(*refs):
        a_ref, b_ref = refs[0], refs[1]
        r_ref = refs[2] if res is not None else None
        o_ref = refs[2 + n_extra]
        part = lax.dot_general(a_ref[...], b_ref[...], dn, preferred_element_type=F32)

        def finish(r):
            if r_ref is not None:
                r = r + r_ref[...]
            o_ref[...] = r.astype(out_dtype)

        if nk == 1:
            finish(part)
        else:
            acc_ref = refs[3 + n_extra]
            kk = pl.program_id(2)

            @pl.when(kk == 0)
            def _():
                acc_ref[...] = part

            @pl.when(kk > 0)
            def _():
                acc_ref[...] += part

            @pl.when(kk == nk - 1)
            def _():
                finish(acc_ref[...])

    a_spec = pl.BlockSpec((tk, tm), lambda i, j, kk: (kk, i)) if ta else pl.BlockSpec((tm, tk), lambda i, j, kk: (i, kk))
    in_specs, args, aliases = [a_spec, b_spec], [a, b], {}
    if res is not None:
        in_specs.append(pl.BlockSpec((tm, tn), lambda i, j, kk: (i, j)))
        args.append(res)
    if out_cols is not None:
        s_total, o_off, ns_o = out_cols
        assert tn == ns_o and n % ns_o == 0
        o_spec = pl.BlockSpec((None, tm, tn), lambda i, j, kk: (o_off + j, i, 0))
        o_shape = jax.ShapeDtypeStruct((s_total, m, tn), out_dtype)
        if out_alias is not None:
            in_specs.append(ANY)
            args.append(out_alias)
            aliases[len(args) - 1] = 0
    else:
        o_spec = pl.BlockSpec((tm, tn), lambda i, j, kk: (i, j))
        o_shape = jax.ShapeDtypeStruct((m, n), out_dtype)
    outs, couts = _pcall(body, name=name, grid=(m // tm, n // tn, nk), in_specs=in_specs, out_specs=[o_spec],
                         out_shape=[o_shape], args=args, sem=("parallel", "parallel", "arbitrary"),
                         scratch=[pltpu.VMEM((tm, tn), F32)] if nk > 1 else [], aliases=aliases, comm=comm)
    return (outs[0], couts) if comm is not None else outs[0]


def _rms_fwd(x, g, *, name):
    t, d = x.shape
    tr = _tile(t, 256)

    def body(x_ref, g_ref, h_ref):
        xf = x_ref[...]
        inv = lax.rsqrt(jnp.mean(xf * xf, axis=-1, keepdims=True) + RMS_EPS)
        h_ref[...] = ((xf * inv) * g_ref[...]).astype(BF16)

    row = pl.BlockSpec((tr, d), lambda i: (i, 0))
    return _pcall(body, name=name, grid=(t // tr,), in_specs=[row, pl.BlockSpec((1, d), lambda i: (0, 0))],
                  out_specs=[row], out_shape=[jax.ShapeDtypeStruct((t, d), BF16)], args=(x, g),
                  sem=("parallel",))[0][0]


def _rms_bwd(x, g, dh, dres, *, name):
    t, d = x.shape
    tr = _tile(t, 256)

    def body(x_ref, g_ref, dh_ref, dres_ref, dx_ref, dxb_ref, dg_ref):
        xf = x_ref[...]
        inv = lax.rsqrt(jnp.mean(xf * xf, axis=-1, keepdims=True) + RMS_EPS)
        xn = xf * inv
        dh_f = dh_ref[...].astype(F32)
        dxn = dh_f * g_ref[...]
        dx = dres_ref[...] + inv * (dxn - xn * jnp.mean(dxn * xn, axis=-1, keepdims=True))
        dx_ref[...] = dx
        dxb_ref[...] = dx.astype(BF16)
        part = jnp.sum(dh_f * xn, axis=0, keepdims=True)

        @pl.when(pl.program_id(0) == 0)
        def _():
            dg_ref[...] = part

        @pl.when(pl.program_id(0) > 0)
        def _():
            dg_ref[...] += part

    row = pl.BlockSpec((tr, d), lambda i: (i, 0))
    vec = pl.BlockSpec((1, d), lambda i: (0, 0))
    return _pcall(body, name=name, grid=(t // tr,), in_specs=[row, vec, row, row], out_specs=[row, row, vec],
                  out_shape=[jax.ShapeDtypeStruct((t, d), F32), jax.ShapeDtypeStruct((t, d), BF16),
                             jax.ShapeDtypeStruct((1, d), F32)], args=(x, g, dh, dres), sem=("arbitrary",))[0]


def _final_loss(x, g, target, *, name):
    t, d = x.shape
    tr = _tile(t, 256)

    def body(x_ref, g_ref, tg_ref, loss_ref, dx_ref, dxb_ref, dg_ref):
        xf = x_ref[...]
        gv = g_ref[...]
        inv = lax.rsqrt(jnp.mean(xf * xf, axis=-1, keepdims=True) + RMS_EPS)
        xn = xf * inv
        err = xn * gv - tg_ref[...]
        lpart = 0.5 * jnp.sum(jnp.mean(err * err, axis=-1, keepdims=True), axis=0, keepdims=True)
        dy = err * (1.0 / d)
        dxn = dy * gv
        dx = inv * (dxn - xn * jnp.mean(dxn * xn, axis=-1, keepdims=True))
        dx_ref[...] = dx
        dxb_ref[...] = dx.astype(BF16)
        gpart = jnp.sum(dy * xn, axis=0, keepdims=True)
        lrow = jnp.broadcast_to(lpart, (1, 128))

        @pl.when(pl.program_id(0) == 0)
        def _():
            dg_ref[...] = gpart
            loss_ref[...] = lrow

        @pl.when(pl.program_id(0) > 0)
        def _():
            dg_ref[...] += gpart
            loss_ref[...] += lrow

    row = pl.BlockSpec((tr, d), lambda i: (i, 0))
    vec = pl.BlockSpec((1, d), lambda i: (0, 0))
    lspec = pl.BlockSpec((1, 128), lambda i: (0, 0))
    return _pcall(body, name=name, grid=(t // tr,), in_specs=[row, vec, row], out_specs=[lspec, row, row, vec],
                  out_shape=[jax.ShapeDtypeStruct((1, 128), F32), jax.ShapeDtypeStruct((t, d), F32),
                             jax.ShapeDtypeStruct((t, d), BF16), jax.ShapeDtypeStruct((1, d), F32)],
                  args=(x, g, target), sem=("arbitrary",))[0]


def _tri_ones(n, upper):
    r = lax.broadcasted_iota(jnp.int32, (n, n), 0)
    c = lax.broadcasted_iota(jnp.int32, (n, n), 1)
    return jnp.where((r <= c) if upper else (r >= c), 1.0, 0.0).astype(F32)


def _forget_fwd(flog, bpad, *, name):
    t = flog.shape[0]
    tb = _tile(t, 512)

    def body(f_ref, b_ref, c_ref, carry):
        z = f_ref[...] + b_ref[...]
        lf = jnp.minimum(z, 0.0) - jnp.log(1.0 + jnp.exp(-jnp.abs(z)))
        lft = lf.T
        tri = _tri_ones(tb, upper=True)

        @pl.when(pl.program_id(0) == 0)
        def _():
            carry[...] = jnp.zeros_like(carry)

        cs = jnp.dot(lft, tri, preferred_element_type=F32, precision=lax.Precision.HIGHEST) + carry[:, 0:1]
        c_ref[...] = cs
        carry[...] = jnp.broadcast_to(cs[:, tb - 1:tb], carry.shape)

    return _pcall(body, name=name, grid=(t // tb,),
                  in_specs=[pl.BlockSpec((tb, 128), lambda i: (i, 0)), pl.BlockSpec((1, 128), lambda i: (0, 0))],
                  out_specs=[pl.BlockSpec((128, tb), lambda i: (0, i))],
                  out_shape=[jax.ShapeDtypeStruct((128, t), F32)], args=(flog, bpad), sem=("arbitrary",),
                  scratch=[pltpu.VMEM((128, 128), F32)])[0][0]


def _forget_bwd(flog, bpad, dct, *, name):
    t = flog.shape[0]
    tb = _tile(t, 512)
    nb = t // tb

    def body(f_ref, b_ref, dc_ref, df_ref, db_ref, carry):
        i = pl.program_id(0)

        @pl.when(i == 0)
        def _():
            carry[...] = jnp.zeros_like(carry)

        tri = _tri_ones(tb, upper=False)
        dl = jnp.dot(dc_ref[...], tri, preferred_element_type=F32, precision=lax.Precision.HIGHEST) + carry[:, 0:1]
        carry[...] = jnp.broadcast_to(dl[:, 0:1], carry.shape)
        z = f_ref[...] + b_ref[...]
        df = dl.T * _sigmoid(-z)
        df_ref[...] = df.astype(BF16)
        part = jnp.sum(df, axis=0, keepdims=True)

        @pl.when(i == 0)
        def _():
            db_ref[...] = part

        @pl.when(i > 0)
        def _():
            db_ref[...] += part

    rev = lambda i: (nb - 1 - i, 0)
    return _pcall(body, name=name, grid=(nb,),
                  in_specs=[pl.BlockSpec((tb, 128), rev), pl.BlockSpec((1, 128), lambda i: (0, 0)),
                            pl.BlockSpec((128, tb), lambda i: (0, nb - 1 - i))],
                  out_specs=[pl.BlockSpec((tb, 128), rev), pl.BlockSpec((1, 128), lambda i: (0, 0))],
                  out_shape=[jax.ShapeDtypeStruct((t, 128), BF16), jax.ShapeDtypeStruct((1, 128), F32)],
                  args=(flog, bpad, dct), sem=("arbitrary",), scratch=[pltpu.VMEM((128, 128), F32)])[0]


def _causal_pairs(nq, k_major):
    if k_major:
        pairs = [(i, j) for j in range(nq) for i in range(j, nq)]
    else:
        pairs = [(i, j) for i in range(nq) for j in range(i + 1)]
    return (jnp.asarray(np.array([p[0] for p in pairs], np.int32)),
            jnp.asarray(np.array([p[1] for p in pairs], np.int32)), len(pairs))


def _logits(q, k, cq, ck, scale, masked):
    s = lax.dot_general(q, k, (((1,), (1,)), ((), ())), preferred_element_type=F32)
    s = s * scale + (cq[:, 0:1] - ck)
    if masked:
        row = lax.broadcasted_iota(jnp.int32, s.shape, 0)
        col = lax.broadcasted_iota(jnp.int32, s.shape, 1)
        s = jnp.where(col <= row, s, NEG)
    return s


def _head_group(heads):
    return 4 if heads % 4 == 0 else (2 if heads % 2 == 0 else 1)


def _fox_fwd(qkv, c3, heads, *, name, comm=None):
    t = qkv.shape[0]
    tq = _tile(t, 512)
    nq = t // tq
    hb = _head_group(heads)
    ng = heads // hb
    scale = HEAD_DIM ** -0.5
    i_tab, j_tab, npairs = _causal_pairs(nq, k_major=False)

    def body(it_ref, jt_ref, q_ref, k_ref, v_ref, cq_ref, ck_ref, o_ref, o32_ref, lse_ref, m_s, l_s, acc_s):
        p_id = pl.program_id(1)
        i, j = it_ref[p_id], jt_ref[p_id]

        @pl.when(j == 0)
        def _():
            m_s[...] = jnp.full_like(m_s, NEG)
            l_s[...] = jnp.zeros_like(l_s)
            acc_s[...] = jnp.zeros_like(acc_s)

        def update(masked):
            for hh in range(hb):
                ls = slice(hh * 128, (hh + 1) * 128)
                s = _logits(q_ref[:, ls], k_ref[:, ls], cq_ref[hh], ck_ref[hh], scale, masked)
                m_prev = m_s[hh, :, 0:1]
                m_new = jnp.maximum(m_prev, jnp.max(s, axis=1, keepdims=True))
                alpha = jnp.exp(m_prev - m_new)
                p = jnp.exp(s - m_new)
                l_s[hh, :, 0:1] = alpha * l_s[hh, :, 0:1] + jnp.sum(p, axis=1, keepdims=True)
                p_hi = p.astype(BF16)
                p_lo = (p - p_hi.astype(F32)).astype(BF16)
                vb = v_ref[:, ls]
                pv = jnp.dot(p_hi, vb, preferred_element_type=F32) + jnp.dot(p_lo, vb, preferred_element_type=F32)
                acc_s[hh] = alpha * acc_s[hh] + pv
                m_s[hh, :, 0:1] = m_new

        @pl.when(j < i)
        def _():
            update(False)

        @pl.when(j == i)
        def _():
            update(True)
            for hh in range(hb):
                ls = slice(hh * 128, (hh + 1) * 128)
                l = l_s[hh, :, 0:1]
                o = acc_s[hh] / l
                o_ref[:, ls] = o.astype(BF16)
                o32_ref[:, ls] = o
                lse_ref[hh] = jnp.broadcast_to(m_s[hh, :, 0:1] + jnp.log(l), (tq, 128))

    w = hb * 128
    qb = lambda g, p, it, jt: (it[p], g)
    outs, couts = _pcall(
        body, name=name, grid=(ng, npairs), prefetch=2,
        in_specs=[pl.BlockSpec((tq, w), qb),
                  pl.BlockSpec((tq, w), lambda g, p, it, jt: (jt[p], ng + g)),
                  pl.BlockSpec((tq, w), lambda g, p, it, jt: (jt[p], 2 * ng + g)),
                  pl.BlockSpec((hb, 1, tq), lambda g, p, it, jt: (g, 0, it[p])),
                  pl.BlockSpec((hb, 1, tq), lambda g, p, it, jt: (g, 0, jt[p]))],
        out_specs=[pl.BlockSpec((tq, w), qb), pl.BlockSpec((tq, w), qb),
                   pl.BlockSpec((hb, tq, 128), lambda g, p, it, jt: (g, it[p], 0))],
        out_shape=[jax.ShapeDtypeStruct((t, heads * 128), BF16), jax.ShapeDtypeStruct((t, heads * 128), F32),
                   jax.ShapeDtypeStruct((heads, t, 128), F32)],
        args=(i_tab, j_tab, qkv, qkv, qkv, c3, c3), sem=("parallel", "arbitrary"),
        scratch=[pltpu.VMEM((hb, tq, 128), F32), pltpu.VMEM((hb, tq, 128), F32), pltpu.VMEM((hb, tq, 128), F32)],
        comm=comm)
    return outs, couts


def _fox_bwd(qkv, c3, o, do, lse, heads, *, name, comm=None):
    t = qkv.shape[0]
    tq = _tile(t, 512)
    nq = t // tq
    hb = _head_group(heads)
    ng = heads // hb
    scale = HEAD_DIM ** -0.5
    i_tab, j_tab, npairs = _causal_pairs(nq, k_major=True)

    def body(it_ref, jt_ref, q_ref, k_ref, v_ref, o_ref, do_ref, lse_ref, cq_ref, ck_ref,
             dq_ref, dk_ref, dv_ref, dc_ref, dq_s, dk_s, dv_s, dc_s):
        p_id = pl.program_id(1)
        i, j = it_ref[p_id], jt_ref[p_id]

        @pl.when(p_id == 0)
        def _():
            dq_s[...] = jnp.zeros_like(dq_s)

        @pl.when(i == j)
        def _():
            dk_s[...] = jnp.zeros_like(dk_s)
            dv_s[...] = jnp.zeros_like(dv_s)
            dc_s[...] = jnp.zeros_like(dc_s)

        def update(masked):
            r0 = pl.multiple_of(i * tq, tq)
            for hh in range(hb):
                ls = slice(hh * 128, (hh + 1) * 128)
                q, k, v, dob = q_ref[:, ls], k_ref[:, ls], v_ref[:, ls], do_ref[:, ls]
                s = _logits(q, k, cq_ref[hh], ck_ref[hh], scale, masked)
                p = jnp.exp(s - lse_ref[hh, :, 0:1])
                delta = jnp.sum(dob.astype(F32) * o_ref[:, ls], axis=1, keepdims=True)
                dp = lax.dot_general(dob, v, (((1,), (1,)), ((), ())), preferred_element_type=F32)
                ds = p * (dp - delta)
                pb, dsb = p.astype(BF16), ds.astype(BF16)
                dv_s[hh] += lax.dot_general(pb, dob, (((0,), (0,)), ((), ())), preferred_element_type=F32)
                dk_s[hh] += lax.dot_general(dsb, q, (((0,), (0,)), ((), ())), preferred_element_type=F32)
                dq_s[hh, pl.ds(r0, tq), :] += jnp.dot(dsb, k, preferred_element_type=F32) * scale
                dc_s[hh] -= jnp.sum(ds, axis=0, keepdims=True)

        @pl.when(i > j)
        def _():
            update(False)

        @pl.when(i == j)
        def _():
            update(True)

        @pl.when(i == nq - 1)
        def _():
            for hh in range(hb):
                ls = slice(hh * 128, (hh + 1) * 128)
                dk_ref[:, ls] = (dk_s[hh] * scale).astype(BF16)
                dv_ref[:, ls] = dv_s[hh].astype(BF16)
            dc_ref[...] = dc_s[...]

        @pl.when(p_id == npairs - 1)
        def _():
            for hh in range(hb):
                dq_ref[:, hh * 128:(hh + 1) * 128] = dq_s[hh].astype(BF16)

    w = hb * 128
    qb = lambda g, p, it, jt: (it[p], g)
    kb = lambda g, p, it, jt: (jt[p], g)
    outs, couts = _pcall(
        body, name=name, grid=(ng, npairs), prefetch=2,
        in_specs=[pl.BlockSpec((tq, w), qb),
                  pl.BlockSpec((tq, w), lambda g, p, it, jt: (jt[p], ng + g)),
                  pl.BlockSpec((tq, w), lambda g, p, it, jt: (jt[p], 2 * ng + g)),
                  pl.BlockSpec((tq, w), qb), pl.BlockSpec((tq, w), qb),
                  pl.BlockSpec((hb, tq, 128), lambda g, p, it, jt: (g, it[p], 0)),
                  pl.BlockSpec((hb, 1, tq), lambda g, p, it, jt: (g, 0, it[p])),
                  pl.BlockSpec((hb, 1, tq), lambda g, p, it, jt: (g, 0, jt[p]))],
        out_specs=[pl.BlockSpec((t, w), lambda g, p, it, jt: (0, g)), pl.BlockSpec((tq, w), kb),
                   pl.BlockSpec((tq, w), kb), pl.BlockSpec((hb, 1, tq), lambda g, p, it, jt: (g, 0, jt[p]))],
        out_shape=[jax.ShapeDtypeStruct((t, heads * 128), BF16), jax.ShapeDtypeStruct((t, heads * 128), BF16),
                   jax.ShapeDtypeStruct((t, heads * 128), BF16), jax.ShapeDtypeStruct((heads, 1, t), F32)],
        args=(i_tab, j_tab, qkv, qkv, qkv, o, do, lse, c3, c3), sem=("arbitrary", "arbitrary"),
        scratch=[pltpu.VMEM((hb, t, 128), F32), pltpu.VMEM((hb, tq, 128), F32), pltpu.VMEM((hb, tq, 128), F32),
                 pltpu.VMEM((hb, 1, tq), F32)], comm=comm)
    return outs, couts


def _tril_mask():
    r = lax.broadcasted_iota(jnp.int32, (128, 128), 0)
    c = lax.broadcasted_iota(jnp.int32, (128, 128), 1)
    return r >= c


def _sgu_fwd(uv, g, w, bst, *, name):
    t = uv.shape[0]
    sw = uv.shape[1] // 2
    groups = sw // 128
    tr = _tile(t, 512)

    def body(u_ref, v_ref, g_ref, w_ref, b_ref, y_ref):
        gv = _gelu(v_ref[...].astype(F32))
        inv = lax.rsqrt(jnp.mean(gv * gv, axis=-1, keepdims=True) + RMS_EPS)
        vn = ((gv * inv) * g_ref[...]).astype(BF16)
        gu = _gelu(u_ref[...].astype(F32))
        mask = _tril_mask()
        for gi in range(groups):
            wg = jnp.where(mask, w_ref[gi], 0.0).astype(BF16)
            bcol = b_ref[:, gi:gi + 1]
            cs = slice(gi * 128, (gi + 1) * 128)
            for ci in range(tr // 128):
                rs = slice(ci * 128, (ci + 1) * 128)
                mixed = jnp.dot(wg, vn[rs, cs], preferred_element_type=F32) + bcol
                y_ref[rs, cs] = (gu[rs, cs] * mixed).astype(BF16)

    return _pcall(body, name=name, grid=(t // tr,),
                  in_specs=[pl.BlockSpec((tr, sw), lambda i: (i, 0)), pl.BlockSpec((tr, sw), lambda i: (i, 1)),
                            pl.BlockSpec((1, sw), lambda i: (0, 0)),
                            pl.BlockSpec((groups, 128, 128), lambda i: (0, 0, 0)),
                            pl.BlockSpec((128, 128), lambda i: (0, 0))],
                  out_specs=[pl.BlockSpec((tr, sw), lambda i: (i, 0))],
                  out_shape=[jax.ShapeDtypeStruct((t, sw), BF16)], args=(uv, uv, g, w, bst),
                  sem=("parallel",))[0][0]


def _sgu_bwd(uv, g, w, bst, dy, *, name):
    t = uv.shape[0]
    sw = uv.shape[1] // 2
    groups = sw // 128
    tr = _tile(t, 256)
    nsteps = t // tr

    def body(u_ref, v_ref, g_ref, w_ref, b_ref, dy_ref, duv_ref, dw_ref, db_ref, dg_ref, dvn_s, dgu_s):
        step = pl.program_id(0)

        @pl.when(step == 0)
        def _():
            dw_ref[...] = jnp.zeros_like(dw_ref)
            db_ref[...] = jnp.zeros_like(db_ref)
            dg_ref[...] = jnp.zeros_like(dg_ref)

        vf = v_ref[...].astype(F32)
        gv, gv_grad = _gelu_and_grad(vf)
        inv = lax.rsqrt(jnp.mean(gv * gv, axis=-1, keepdims=True) + RMS_EPS)
        xn = gv * inv
        gvec = g_ref[...]
        vn = (xn * gvec).astype(BF16)
        uf = u_ref[...].astype(F32)
        gu, gu_grad = _gelu_and_grad(uf)
        dyf = dy_ref[...].astype(F32)
        mask = _tril_mask()
        lane = lax.broadcasted_iota(jnp.int32, (128, 128), 1)
        dball = jnp.zeros((128, 128), F32)
        for gi in range(groups):
            wg = jnp.where(mask, w_ref[gi], 0.0).astype(BF16)
            wgt = wg.T
            bcol = b_ref[:, gi:gi + 1]
            cs = slice(gi * 128, (gi + 1) * 128)
            dwg = jnp.zeros((128, 128), F32)
            dbg = jnp.zeros((128, 1), F32)
            for ci in range(tr // 128):
                rs = slice(ci * 128, (ci + 1) * 128)
                vnb = vn[rs, cs]
                mixed = jnp.dot(wg, vnb, preferred_element_type=F32) + bcol
                dgu_s[rs, cs] = dyf[rs, cs] * mixed
                dmix = dyf[rs, cs] * gu[rs, cs]
                dmb = dmix.astype(BF16)
                dvn_s[rs, cs] = jnp.dot(wgt, dmb, preferred_element_type=F32)
                dwg = dwg + lax.dot_general(dmb, vnb, (((1,), (1,)), ((), ())), preferred_element_type=F32)
                dbg = dbg + jnp.sum(dmix, axis=1, keepdims=True)
            dw_ref[gi] += dwg
            dball = dball + jnp.where(lane == gi, dbg, 0.0)
        db_ref[...] += dball
        dvn = dvn_s[...]
        dg_ref[...] += jnp.sum(dvn * xn, axis=0, keepdims=True)
        dxn = dvn * gvec
        dgv = inv * (dxn - xn * jnp.mean(dxn * xn, axis=-1, keepdims=True))
        duv_ref[:, 0:sw] = (dgu_s[...] * gu_grad).astype(BF16)
        duv_ref[:, sw:2 * sw] = (dgv * gv_grad).astype(BF16)

        @pl.when(step == nsteps - 1)
        def _():
            for gi in range(groups):
                dw_ref[gi] = jnp.where(mask, dw_ref[gi], 0.0)

    return _pcall(body, name=name, grid=(nsteps,),
                  in_specs=[pl.BlockSpec((tr, sw), lambda i: (i, 0)), pl.BlockSpec((tr, sw), lambda i: (i, 1)),
                            pl.BlockSpec((1, sw), lambda i: (0, 0)),
                            pl.BlockSpec((groups, 128, 128), lambda i: (0, 0, 0)),
                            pl.BlockSpec((128, 128), lambda i: (0, 0)), pl.BlockSpec((tr, sw), lambda i: (i, 0))],
                  out_specs=[pl.BlockSpec((tr, 2 * sw), lambda i: (i, 0)),
                             pl.BlockSpec((groups, 128, 128), lambda i: (0, 0, 0)),
                             pl.BlockSpec((128, 128), lambda i: (0, 0)), pl.BlockSpec((1, sw), lambda i: (0, 0))],
                  out_shape=[jax.ShapeDtypeStruct((t, 2 * sw), BF16), jax.ShapeDtypeStruct((groups, 128, 128), F32),
                             jax.ShapeDtypeStruct((128, 128), F32), jax.ShapeDtypeStruct((1, sw), F32)],
                  args=(uv, uv, g, w, bst, dy), sem=("arbitrary",),
                  scratch=[pltpu.VMEM((tr, sw), F32), pltpu.VMEM((tr, sw), F32)])[0]


def _merge_fwd(ya, yb, wa, wb, gates, *, name, comm=None):
    t, kdim = ya.shape
    d = wa.shape[1]
    tm, ns = _tile(t, 1024), _tile(d, 1024)
    nsh = d // ns

    def body(ya_ref, yb_ref, wa_ref, wb_ref, ga_ref, gb_ref, mg_ref, za_ref, zb_ref):
        za = jnp.dot(ya_ref[...], wa_ref[...], preferred_element_type=F32)
        zb = jnp.dot(yb_ref[...], wb_ref[...], preferred_element_type=F32)
        sa = _sigmoid(ga_ref[...].astype(F32))
        sb = _sigmoid(gb_ref[...].astype(F32))
        mg_ref[...] = (sa * za + sb * zb).astype(BF16)
        za_ref[...] = za.astype(BF16)
        zb_ref[...] = zb.astype(BF16)

    yspec = pl.BlockSpec((tm, kdim), lambda i, j: (i, 0))
    wspec = pl.BlockSpec((kdim, ns), lambda i, j: (0, j))
    ospec = pl.BlockSpec((tm, ns), lambda i, j: (i, j))
    outs, couts = _pcall(body, name=name, grid=(t // tm, nsh),
                         in_specs=[yspec, yspec, wspec, wspec, ospec, pl.BlockSpec((tm, ns), lambda i, j: (i, nsh + j))],
                         out_specs=[ospec, ospec, ospec], out_shape=[jax.ShapeDtypeStruct((t, d), BF16)] * 3,
                         args=(ya, yb, wa, wb, gates, gates), sem=("parallel", "parallel"), comm=comm)
    return outs, couts


def _merge_bwd(dmg, gates, za, zb, *, name):
    t, d = dmg.shape
    tr = _tile(t, 256)

    def body(dm_ref, ga_ref, gb_ref, za_ref, zb_ref, dza_ref, dzb_ref, dg_ref):
        dm = dm_ref[...].astype(F32)
        sa = _sigmoid(ga_ref[...].astype(F32))
        sb = _sigmoid(gb_ref[...].astype(F32))
        dza_ref[...] = (dm * sa).astype(BF16)
        dzb_ref[...] = (dm * sb).astype(BF16)
        dg_ref[:, 0:d] = (dm * za_ref[...].astype(F32) * (sa * (1.0 - sa))).astype(BF16)
        dg_ref[:, d:2 * d] = (dm * zb_ref[...].astype(F32) * (sb * (1.0 - sb))).astype(BF16)

    row = pl.BlockSpec((tr, d), lambda i: (i, 0))
    return _pcall(body, name=name, grid=(t // tr,),
                  in_specs=[row, row, pl.BlockSpec((tr, d), lambda i: (i, 1)), row, row],
                  out_specs=[row, row, pl.BlockSpec((tr, 2 * d), lambda i: (i, 0))],
                  out_shape=[jax.ShapeDtypeStruct((t, d), BF16), jax.ShapeDtypeStruct((t, d), BF16),
                             jax.ShapeDtypeStruct((t, 2 * d), BF16)],
                  args=(dmg, gates, gates, za, zb), sem=("parallel",))[0]


def _shift_down(ext, k, rows):
    return pltpu.roll(ext, k, 0)[8:8 + rows]


def _conv_fwd(ab, cw, cb, *, name):
    t = ab.shape[0]
    dff = ab.shape[1] // 2
    tr, tc = _tile(t, 512), _tile(dff, 512)
    nc = dff // tc
    r8 = tr // 8

    def body(a_ref, ap_ref, b_ref, cw_ref, cb_ref, g_ref, g1_ref, g2_ref):
        i = pl.program_id(0)
        prev = ap_ref[...].astype(F32) * jnp.where(i > 0, 1.0, 0.0)
        a = a_ref[...].astype(F32)
        ext = jnp.concatenate([prev, a], axis=0)
        acc = cb_ref[...] + cw_ref[0:1, :] * _shift_down(ext, 2, tr) + cw_ref[1:2, :] * _shift_down(ext, 1, tr) \
            + cw_ref[2:3, :] * a
        gel, gel_grad = _gelu_and_grad(acc)
        bf = b_ref[...].astype(F32)
        g_ref[...] = (gel * bf).astype(BF16)
        g1_ref[...] = gel.astype(BF16)
        g2_ref[...] = (bf * gel_grad).astype(BF16)

    ospec = pl.BlockSpec((tr, tc), lambda i, j: (i, j))
    return _pcall(body, name=name, grid=(t // tr, nc),
                  in_specs=[ospec, pl.BlockSpec((8, tc), lambda i, j: (jnp.maximum(i * r8 - 1, 0), j)),
                            pl.BlockSpec((tr, tc), lambda i, j: (i, nc + j)),
                            pl.BlockSpec((CONV_WIDTH, tc), lambda i, j: (0, j)),
                            pl.BlockSpec((1, tc), lambda i, j: (0, j))],
                  out_specs=[ospec, ospec, ospec],
                  out_shape=[jax.ShapeDtypeStruct((t, dff), BF16)] * 3, args=(ab, ab, ab, cw, cb),
                  sem=("parallel", "parallel"))[0]


def _conv_bwd(ab, g1, g2, cw, dgg, *, name, comm=None):
    t = ab.shape[0]
    dff = ab.shape[1] // 2
    tr, tc = _tile(t, 512), _tile(dff, 512)
    nc, nr = dff // tc, t // tr
    r8 = tr // 8
    ext_rows = tr + 8

    def body(a_ref, ap_ref, g1_ref, g2_ref, g2n_ref, dg_ref, dgn_ref, cw_ref, da_ref, db_ref, dcw_ref, dcb_ref):
        i = pl.program_id(1)
        has_prev = jnp.where(i > 0, 1.0, 0.0)
        has_next = jnp.where(i < nr - 1, 1.0, 0.0)
        dg = dg_ref[...].astype(F32)
        dacc = jnp.concatenate([dg * g2_ref[...].astype(F32),
                                dgn_ref[...].astype(F32) * g2n_ref[...].astype(F32) * has_next], axis=0)
        w0, w1, w2 = cw_ref[0:1, :], cw_ref[1:2, :], cw_ref[2:3, :]
        d_a = w2 * dacc + w1 * pltpu.roll(dacc, ext_rows - 1, 0) + w0 * pltpu.roll(dacc, ext_rows - 2, 0)
        da_ref[...] = d_a[0:tr].astype(BF16)
        db_ref[...] = (dg * g1_ref[...].astype(F32)).astype(BF16)
        dm = dacc[0:tr]
        a = a_ref[...].astype(F32)
        a_ext = jnp.concatenate([ap_ref[...].astype(F32) * has_prev, a], axis=0)
        dcw = jnp.concatenate([jnp.sum(dm * _shift_down(a_ext, 2, tr), axis=0, keepdims=True),
                               jnp.sum(dm * _shift_down(a_ext, 1, tr), axis=0, keepdims=True),
                               jnp.sum(dm * a, axis=0, keepdims=True)], axis=0)
        dcb = jnp.sum(dm, axis=0, keepdims=True)

        @pl.when(i == 0)
        def _():
            dcw_ref[...] = dcw
            dcb_ref[...] = dcb

        @pl.when(i > 0)
        def _():
            dcw_ref[...] += dcw
            dcb_ref[...] += dcb

    cur = lambda off: pl.BlockSpec((tr, tc), lambda j, i: (i, off + j))
    prv = lambda off: pl.BlockSpec((8, tc), lambda j, i: (jnp.maximum(i * r8 - 1, 0), off + j))
    nxt = lambda off: pl.BlockSpec((8, tc), lambda j, i: (jnp.minimum((i + 1) * r8, nr * r8 - 1), off + j))
    return _pcall(body, name=name, grid=(nc, nr),
                  in_specs=[cur(0), prv(0), cur(0), cur(0), nxt(0), cur(0), nxt(0),
                            pl.BlockSpec((CONV_WIDTH, tc), lambda j, i: (0, j))],
                  out_specs=[cur(0), cur(0), pl.BlockSpec((CONV_WIDTH, tc), lambda j, i: (0, j)),
                             pl.BlockSpec((1, tc), lambda j, i: (0, j))],
                  out_shape=[jax.ShapeDtypeStruct((t, dff), BF16), jax.ShapeDtypeStruct((t, dff), BF16),
                             jax.ShapeDtypeStruct((CONV_WIDTH, dff), F32), jax.ShapeDtypeStruct((1, dff), F32)],
                  args=(ab, ab, g1, g2, g2, dgg, dgg, cw), sem=("parallel", "arbitrary"), comm=comm)


class _InLayout:
    def __init__(self, nsi, heads, fw, sw, d):
        self.nsi = nsi
        self.pw = -(-(nsi + 127) // 128) * 128
        self.o_u = 3 * fw + heads
        self.ins = 128 - heads
        self.widths = [3 * fw, 128, 2 * sw, 2 * d]
        self.total = sum(self.widths)

    def regions(self, s):
        g0 = self.nsi * s
        out = []
        lo, hi = g0, min(g0 + self.nsi, self.o_u)
        if lo < hi:
            out.append((lo, hi, g0 % 128, g0 // 128, 0, hi - g0))
        lo, hi = max(g0, self.o_u), g0 + self.nsi
        if lo < hi:
            gi = g0 + self.ins
            out.append((lo + self.ins, hi + self.ins, gi % 128, gi // 128, lo - g0, self.nsi))
        return out


def _lane_mix(lane, pieces):
    val = None
    for piece, lo, hi in pieces:
        if lo <= 0 and hi >= 128:
            val = piece
        else:
            val = jnp.where((lane >= lo) & (lane < hi), piece, jnp.zeros_like(piece) if val is None else val)
    return val


def _w_in_unpack(g, lay, *, name):
    _, d, pw = g.shape
    tr = _tile(d, 128)
    ntiles = lay.total // 128
    plan = [[] for _ in range(ntiles)]
    for s in range(N_DEV):
        for r, (lo, hi, _, tile0, _, _) in enumerate(lay.regions(s)):
            for tt in range(lo // 128, (hi - 1) // 128 + 1):
                plan[tt].append((s, r, tt - tile0, lo - 128 * tt, hi - 128 * tt))
    bounds = np.cumsum([0] + [w // 128 for w in lay.widths])

    def body(g_ref, *o_refs):
        lane = lax.broadcasted_iota(jnp.int32, (tr, 128), 1)
        rolled = {}

        def src(s, r):
            if (s, r) not in rolled:
                shift = lay.regions(s)[r][2]
                xs = g_ref[s].astype(F32)
                rolled[(s, r)] = pltpu.roll(xs, shift, 1) if shift else xs
            return rolled[(s, r)]

        for tt in range(ntiles):
            val = _lane_mix(lane, [(src(s, r)[:, 128 * st:128 * (st + 1)], lo, hi) for s, r, st, lo, hi in plan[tt]])
            o = int(np.searchsorted(bounds, tt, side="right")) - 1
            lt = tt - int(bounds[o])
            o_refs[o][:, 128 * lt:128 * (lt + 1)] = val.astype(BF16)

    return _pcall(body, name=name, grid=(d // tr,), in_specs=[pl.BlockSpec((N_DEV, tr, pw), lambda i: (0, i, 0))],
                  out_specs=[pl.BlockSpec((tr, w), lambda i: (i, 0)) for w in lay.widths],
                  out_shape=[jax.ShapeDtypeStruct((d, w), BF16) for w in lay.widths], args=(g,),
                  sem=("parallel",))[0]


def _w_in_grad_pack(pieces, lay, *, name):
    d = pieces[0].shape[0]
    tr = _tile(d, 128)
    assert sum(p.shape[1] for p in pieces) == lay.total
    starts = np.cumsum([0] + [p.shape[1] // 128 for p in pieces])
    ntot = lay.total // 128
    nfull, rem = lay.nsi // 128, lay.nsi % 128
    npc = len(pieces)

    def body(*refs):
        p_refs, o_ref = refs[:npc], refs[npc]
        lane = lax.broadcasted_iota(jnp.int32, (tr, 128), 1)

        def padded_tile(tau):
            pi = int(np.searchsorted(starts, tau, side="right")) - 1
            lt = tau - int(starts[pi])
            return p_refs[pi][:, 128 * lt:128 * (lt + 1)].astype(F32)

        for s in range(N_DEV):
            unrolled = []
            for _, _, shift, tile0, j_lo, j_hi in lay.regions(s):
                win = jnp.concatenate([padded_tile(min(tile0 + q, ntot - 1)) for q in range(lay.pw // 128)], axis=1)
                unrolled.append((pltpu.roll(win, lay.pw - shift, 1) if shift else win, j_lo, j_hi))
            for kt in range(nfull + (1 if rem else 0)):
                val = _lane_mix(lane, [(u[:, 128 * kt:128 * (kt + 1)], j_lo - 128 * kt, j_hi - 128 * kt)
                                       for u, j_lo, j_hi in unrolled if j_lo < 128 * (kt + 1) and j_hi > 128 * kt])
                if kt < nfull:
                    o_ref[s, :, 128 * kt:128 * (kt + 1)] = val.astype(BF16)
                else:
                    o_ref[s, :, 128 * nfull:lay.nsi] = val[:, :rem].astype(BF16)

    return _pcall(body, name=name, grid=(d // tr,),
                  in_specs=[pl.BlockSpec((tr, p.shape[1]), lambda i: (i, 0)) for p in pieces],
                  out_specs=[pl.BlockSpec((N_DEV, tr, lay.nsi), lambda i: (0, i, 0))],
                  out_shape=[jax.ShapeDtypeStruct((N_DEV, d, lay.nsi), BF16)], args=tuple(pieces),
                  sem=("parallel",))[0][0]


def _row_tile(rows, cols):
    for cand in (512, 256, 128, 64, 32, 16, 8):
        if rows % cand == 0 and cand * cols * 4 <= 2 * 1024 * 1024:
            return cand
    return rows


def _pair_add(grad8, recv4, cidx, *, name):
    _, rows, cols = grad8.shape
    tr = _row_tile(rows, cols)

    def body(c_ref, g_ref, r_ref, o_ref):
        o_ref[...] = (g_ref[...].astype(F32) + r_ref[...].astype(F32)).astype(BF16)

    blk = (None, tr, cols)
    return _pcall(body, name=name, grid=(4, rows // tr), prefetch=1,
                  in_specs=[pl.BlockSpec(blk, lambda k, i, c_ref: (2 * k + c_ref[0], i, 0)),
                            pl.BlockSpec(blk, lambda k, i, c_ref: (k, i, 0))],
                  out_specs=[pl.BlockSpec(blk, lambda k, i, c_ref: (k, i, 0))],
                  out_shape=[jax.ShapeDtypeStruct((4, rows, cols), BF16)], args=(cidx, grad8, recv4),
                  sem=("parallel", "parallel"))[0][0]


def _adamw_math(w, g, m, v):
    m = ADAM_B1 * m + (1.0 - ADAM_B1) * g
    v = ADAM_B2 * v + (1.0 - ADAM_B2) * (g * g)
    m_hat = m / (1.0 - ADAM_B1 ** ADAM_STEP)
    v_hat = v / (1.0 - ADAM_B2 ** ADAM_STEP)
    delta = -ADAM_LR * (m_hat / (jnp.sqrt(v_hat) + ADAM_EPS) + ADAM_WD * w)
    return delta, m, v


def _shard_adamw(part4, recv3, w, m, v, kidx, layer, bufs, *, name):
    depth, rows, cols = w.shape
    tr = _row_tile(rows, cols)

    def body(k_ref, p_ref, r0_ref, r1_ref, r2_ref, w_ref, m_ref, v_ref, *rest):
        g_out, d_out, m_out, v_out = rest[-4:]
        g = ((p_ref[...].astype(F32) + r0_ref[...].astype(F32)) + r1_ref[...].astype(F32)) + r2_ref[...].astype(F32)
        delta, mn, vn = _adamw_math(w_ref[...], g, m_ref[...], v_ref[...])
        g_out[...] = g
        d_out[...] = delta
        m_out[...] = mn
        v_out[...] = vn

    blk = (None, tr, cols)
    rspec = lambda j: pl.BlockSpec(blk, lambda i, k_ref: (j, i, 0))
    espec = pl.BlockSpec(blk, lambda i, k_ref: (layer, i, 0))
    in_specs = [pl.BlockSpec(blk, lambda i, k_ref: (k_ref[0], i, 0)), rspec(0), rspec(1), rspec(2), espec, espec, espec]
    args = [kidx, part4, recv3, recv3, recv3, w, m, v]
    aliases = {}
    if bufs is not None:
        in_specs += [ANY] * 4
        aliases = {8 + q: q for q in range(4)}
        args += list(bufs)
    return _pcall(body, name=name, grid=(rows // tr,), prefetch=1, in_specs=in_specs, out_specs=[espec] * 4,
                  out_shape=[jax.ShapeDtypeStruct((depth, rows, cols), F32)] * 4, args=args, sem=("parallel",),
                  aliases=aliases)[0]


def _small_adamw(gath, w, m, v, *, name):
    rows = w.shape[0]

    def body(g_ref, w_ref, m_ref, v_ref, g_out, d_out, m_out, v_out):
        g = g_ref[0]
        for dev in range(1, N_DEV):
            g = g + g_ref[dev]
        delta, mn, vn = _adamw_math(w_ref[...], g, m_ref[...], v_ref[...])
        g_out[...] = g
        d_out[...] = delta
        m_out[...] = mn
        v_out[...] = vn

    tr = _row_tile(rows, 128 * N_DEV)
    espec = pl.BlockSpec((tr, 128), lambda i: (i, 0))
    return _pcall(body, name=name, grid=(rows // tr,),
                  in_specs=[pl.BlockSpec((N_DEV, tr, 128), lambda i: (0, i, 0)), espec, espec, espec],
                  out_specs=[espec] * 4, out_shape=[jax.ShapeDtypeStruct((rows, 128), F32)] * 4,
                  args=(gath, w, m, v), sem=("parallel",))[0]


def _pack(arrs):
    flat = jnp.concatenate([a.reshape(-1) for a in arrs])
    total = flat.shape[0]
    rows = -(-total // (128 * 64)) * 64
    return jnp.pad(flat, (0, rows * 128 - total)).reshape(rows, 128)


def _unpack(packed, like):
    flat = packed.reshape(-1)
    out, off = [], 0
    for a in like:
        out.append(flat[off:off + a.size].reshape(a.shape))
        off += a.size
    return out


def kernel(x, g_mix, w_in, b_forget, g_sgu, w_spatial, b_spatial, w_branch_a, w_branch_b, w_out, g_ffn, w_up, conv_w, conv_b, w_down, g_final, loss_target, m_g_mix, m_w_in, m_b_forget, m_g_sgu, m_w_spatial, m_b_spatial, m_w_branch_a, m_w_branch_b, m_w_out, m_g_ffn, m_w_up, m_conv_w, m_conv_b, m_w_down, m_g_final, v_g_mix, v_w_in, v_b_forget, v_g_sgu, v_w_spatial, v_b_spatial, v_w_branch_a, v_w_branch_b, v_w_out, v_g_ffn, v_w_up, v_conv_w, v_conv_b, v_w_down, v_g_final):
    depth, d = g_mix.shape
    heads = b_forget.shape[1]
    fw = heads * HEAD_DIM
    sw = g_sgu.shape[1]
    dff = conv_b.shape[1]
    t = x.shape[1]
    nsi = w_in.shape[2]
    nsu = w_up.shape[2]
    o_f, o_u, o_g = 3 * fw, 3 * fw + heads, 3 * fw + heads + 2 * sw

    bpad = jnp.pad(b_forget, ((0, 0), (0, 128 - heads)))
    bst = jnp.pad(jnp.swapaxes(b_spatial, 1, 2), ((0, 0), (0, 0), (0, 128 - b_spatial.shape[1])))

    lay = _InLayout(nsi, heads, fw, sw, d)

    def in_shards(l):
        return [jnp.pad(w_in[l].astype(BF16), ((0, 0), (0, lay.pw - nsi)))]

    def small_shards(l):
        return [w_branch_a[l].astype(BF16), w_branch_b[l].astype(BF16), w_out[l].astype(BF16), conv_w[l]]

    def ffn_shards(l):
        return [w_up[l].astype(BF16), w_down[l].astype(BF16)]

    def unpack_in(bufs, l):
        wqkv, wf, wuv, wg = _w_in_unpack(bufs[0], lay, name=f"unpack_w_in_{l}")
        return dict(wqkv=wqkv, wf=wf, wuv=wuv, wg=wg)

    def unpack_small(bufs):
        g_wa, g_wb, g_wo, g_cw = bufs
        plain = lambda g: jnp.moveaxis(g, 0, 1).reshape(g.shape[1], d)
        return dict(wa=g_wa, wb=g_wb, wa_t=plain(g_wa), wb_t=plain(g_wb), wo=g_wo.reshape(d, d),
                    cw=jnp.moveaxis(g_cw, 0, 1).reshape(CONV_WIDTH, dff))

    mixer_w = [None] * depth
    ffn_w = [None] * depth
    first = _comm_only(_gather_first(in_shards(0)), name="gather_in_first_0")
    mixer_w[0] = unpack_in(_comm_only(_gather_second(first), name="gather_in_second_0"), 0)
    small_first = None

    xs = x[0]
    saved = []
    for l in range(depth):
        n = lambda s: f"{s}_{l}"
        mw = mixer_w[l]
        h = _rms_fwd(xs, g_mix[l][None], name=n("rms_mix"))
        if small_first is None:
            qkv, small_first = _mm(h, mw["wqkv"], name=n("proj_qkv"), comm=_gather_first(small_shards(l)))
        else:
            qkv = _mm(h, mw["wqkv"], name=n("proj_qkv"))
        uv = _mm(h, mw["wuv"], name=n("proj_uv"))
        gates, small_bufs = _mm(h, mw["wg"], name=n("proj_gates"), comm=_gather_second(small_first))
        mw.update(unpack_small(small_bufs))
        flog = _mm(h, mw["wf"], out_dtype=F32, name=n("proj_forget"))
        c3 = _forget_fwd(flog, bpad[l][None], name=n("forget_fwd")).reshape(128, 1, t)
        (ya, ya32, lse), ffn_first = _fox_fwd(qkv, c3, heads, name=n("fox_fwd"), comm=_gather_first(ffn_shards(l)))
        yb = _sgu_fwd(uv, g_sgu[l][None], w_spatial[l], bst[l], name=n("sgu_fwd"))
        (merged, za, zb), _ = _merge_fwd(ya, yb, mw["wa_t"], mw["wb_t"], gates, name=n("merge_fwd"))
        x1, ffn_bufs = _mm(merged, mw["wo"], out_dtype=F32, res=xs, name=n("out_proj"), comm=_gather_second(ffn_first))
        g_wu, g_wd = ffn_bufs
        ffn_w[l] = dict(wu=g_wu, wd=g_wd.reshape(dff, d))
        h2 = _rms_fwd(x1, g_ffn[l][None], name=n("rms_ffn"))
        if l + 1 < depth:
            ab, in_first = _mm(h2, g_wu, b_cols=True, name=n("ffn_up"), comm=_gather_first(in_shards(l + 1)))
        else:
            ab = _mm(h2, g_wu, b_cols=True, name=n("ffn_up"))
        gg, g1, g2 = _conv_fwd(ab, mw["cw"], conv_b[l][None], name=n("conv_fwd"))
        if l + 1 < depth:
            both = _Join([_gather_second(in_first), _gather_first(small_shards(l + 1))])
            x2, nxt = _mm(gg, ffn_w[l]["wd"], out_dtype=F32, res=x1, name=n("ffn_down"), comm=both)
            in_bufs, small_first = both.split(nxt)
            mixer_w[l + 1] = unpack_in(in_bufs, l + 1)
        else:
            x2 = _mm(gg, ffn_w[l]["wd"], out_dtype=F32, res=x1, name=n("ffn_down"))
        saved.append((xs, h, qkv, uv, gates, flog, c3, ya, ya32, lse, yb, merged, za, zb, x1, h2, ab, gg, g1, g2))
        xs = x2

    loss_row, dx, dxb, d_g_final = _final_loss(xs, g_final[None], loss_target[0], name="final_loss")

    cidx = lax.axis_index("c").astype(jnp.int32).reshape(1)
    kidx = (2 * lax.axis_index("x") + lax.axis_index("y")).astype(jnp.int32).reshape(1)
    small_g = {k: [None] * depth for k in ("g_mix", "b_forget", "g_sgu", "w_spatial", "b_spatial", "g_ffn", "conv_b")}
    parts = {}
    recvs = {}
    mixer_names = ["w_in", "w_branch_a", "w_branch_b", "w_out"]
    ffn_names = ["w_up", "w_down", "conv_w"]
    for nm in mixer_names + ffn_names:
        parts[nm], recvs[nm] = [None] * depth, [None] * depth
    pending_mixer = None
    for l in reversed(range(depth)):
        n = lambda s: f"{s}_{l}"
        mw, fw_ = mixer_w[l], ffn_w[l]
        xs, h, qkv, uv, gates, flog, c3, ya, ya32, lse, yb, merged, za, zb, x1, h2, ab, gg, g1, g2 = saved[l]
        g_wd = _mm(gg, dxb, ta=True, name=n("d_w_down"))
        if pending_mixer is not None:
            dgg, got = _mm(dxb, fw_["wd"], tb=True, name=n("d_gg"), comm=_scatter_second(pending_mixer[1:]))
            for nm, r in zip(mixer_names[1:], got):
                recvs[nm][l + 1] = r
            half = d // 2
            (d_a, d_b, d_cw, d_cb), got = _conv_bwd(ab, g1, g2, mw["cw"], dgg, name=n("conv_bwd"),
                                                    comm=_scatter_second(pending_mixer[:1], rows=(0, half)))
            g_wu, got = _mm(h2, d_a, ta=True, out_cols=(N_DEV, 0, nsu), name=n("d_w_up_a"),
                            comm=_scatter_second(pending_mixer[:1], rows=(half, d - half), bufs=got))
            recvs[mixer_names[0]][l + 1] = got[0]
        else:
            dgg = _mm(dxb, fw_["wd"], tb=True, name=n("d_gg"))
            (d_a, d_b, d_cw, d_cb), _ = _conv_bwd(ab, g1, g2, mw["cw"], dgg, name=n("conv_bwd"))
            g_wu = _mm(h2, d_a, ta=True, out_cols=(N_DEV, 0, nsu), name=n("d_w_up_a"))
        small_g["conv_b"][l] = d_cb[0]
        g_wu = _mm(h2, d_b, ta=True, out_cols=(N_DEV, N_DEV // 2, nsu), out_alias=g_wu, name=n("d_w_up_b"))
        ffn_grads = [g_wu, g_wd.reshape(N_DEV, dff // N_DEV, d),
                     jnp.moveaxis(d_cw.reshape(CONV_WIDTH, N_DEV, dff // N_DEV), 1, 0)]
        dh2 = _mm(d_a, fw_["wu"], tb=True, b_cols=True, out_dtype=F32, name=n("d_h2_a"))
        dh2, from_sib = _mm(d_b, fw_["wu"], tb=True, b_cols=True, b_off=N_DEV // 2, out_dtype=F32, res=dh2,
                            name=n("d_h2_b"), comm=_scatter_first(ffn_grads))
        ffn_parts = [_pair_add(g8, r4, cidx, name=n(f"pair_add_{nm}")) for nm, g8, r4 in zip(ffn_names, ffn_grads, from_sib)]
        for nm, p in zip(ffn_names, ffn_parts):
            parts[nm][l] = p
        dx, dxb, dg = _rms_bwd(x1, g_ffn[l][None], dh2, dx, name=n("rms_ffn_bwd"))
        small_g["g_ffn"][l] = dg[0]
        g_wo = _mm(merged, dxb, ta=True, name=n("d_w_out"))
        dmg = _mm(dxb, mw["wo"], tb=True, name=n("d_merged"))
        dza, dzb, dgates = _merge_bwd(dmg, gates, za, zb, name=n("merge_bwd"))
        g_wa = _mm(ya, dza, ta=True, out_cols=(N_DEV, 0, d // N_DEV), name=n("d_w_a"))
        g_wb = _mm(yb, dzb, ta=True, out_cols=(N_DEV, 0, d // N_DEV), name=n("d_w_b"))
        dya = _mm(dza, mw["wa_t"], tb=True, name=n("d_ya"))
        dyb = _mm(dzb, mw["wb_t"], tb=True, name=n("d_yb"))
        duv, d_ws, d_bst, d_gs = _sgu_bwd(uv, g_sgu[l][None], w_spatial[l], bst[l], dyb, name=n("sgu_bwd"))
        small_g["w_spatial"][l], small_g["g_sgu"][l] = d_ws, d_gs[0]
        small_g["b_spatial"][l] = d_bst[:, :b_spatial.shape[1]].T
        (dq, dk, dv, dc3), got = _fox_bwd(qkv, c3, ya32, dya, lse, heads, name=n("fox_bwd"),
                                          comm=_scatter_second(ffn_parts))
        for nm, r in zip(ffn_names, got):
            recvs[nm][l] = r
        dct = jnp.pad(dc3.reshape(heads, t), ((0, 128 - heads), (0, 0)))
        dflog, d_bf = _forget_bwd(flog, bpad[l][None], dct, name=n("forget_bwd"))
        small_g["b_forget"][l] = d_bf[0, :heads]
        gw = [_mm(h, dq, ta=True, name=n("d_w_q")), _mm(h, dk, ta=True, name=n("d_w_k")),
              _mm(h, dv, ta=True, name=n("d_w_v")), _mm(h, dflog, ta=True, name=n("d_w_forget")),
              _mm(h, duv, ta=True, name=n("d_w_uv")), _mm(h, dgates, ta=True, name=n("d_w_gates"))]
        g_in = _w_in_grad_pack(gw, lay, name=n("pack_d_w_in"))
        mixer_grads = [g_in, g_wa, g_wb, g_wo.reshape(N_DEV, d // N_DEV, d)]
        dh = _mm(dflog, mw["wf"], tb=True, out_dtype=F32, name=n("d_h_forget"))
        dh = _mm(dq, mw["wqkv"], tb=True, out_dtype=F32, res=dh, name=n("d_h_q"))
        dh = _mm(dk, mw["wqkv"], tb=True, b_off=fw, out_dtype=F32, res=dh, name=n("d_h_k"))
        dh = _mm(dv, mw["wqkv"], tb=True, b_off=2 * fw, out_dtype=F32, res=dh, name=n("d_h_v"))
        dh = _mm(duv, mw["wuv"], tb=True, out_dtype=F32, res=dh, name=n("d_h_uv"))
        dh, from_sib = _mm(dgates, mw["wg"], tb=True, out_dtype=F32, res=dh, name=n("d_h_gates"),
                           comm=_scatter_first(mixer_grads))
        pending_mixer = [_pair_add(g8, r4, cidx, name=n(f"pair_add_{nm}"))
                         for nm, g8, r4 in zip(mixer_names, mixer_grads, from_sib)]
        for nm, p in zip(mixer_names, pending_mixer):
            parts[nm][l] = p
        dx, dxb, dg = _rms_bwd(xs, g_mix[l][None], dh, dx, name=n("rms_mix_bwd"))
        small_g["g_mix"][l] = dg[0]
    grad_x = dx[None]
    for nm, r in zip(mixer_names, _comm_only(_scatter_second(pending_mixer), name="scatter_mixer_second_0")):
        recvs[nm][0] = r

    weights = {"w_in": (w_in, m_w_in, v_w_in), "w_branch_a": (w_branch_a, m_w_branch_a, v_w_branch_a),
               "w_branch_b": (w_branch_b, m_w_branch_b, v_w_branch_b), "w_out": (w_out, m_w_out, v_w_out),
               "w_up": (w_up, m_w_up, v_w_up), "conv_w": (conv_w, m_conv_w, v_conv_w),
               "w_down": (w_down, m_w_down, v_w_down)}
    res = {}
    for nm, (w, m, v) in weights.items():
        bufs = None
        for l in range(depth):
            bufs = _shard_adamw(parts[nm][l], recvs[nm][l], w, m, v, kidx, l, bufs, name=f"adamw_{nm}_{l}")
        res[nm] = bufs

    small = ["g_mix", "b_forget", "g_sgu", "w_spatial", "b_spatial", "g_ffn", "conv_b", "g_final", "loss"]
    zero = jnp.zeros((1,), F32)
    small_w = [g_mix, b_forget, g_sgu, w_spatial, b_spatial, g_ffn, conv_b, g_final, zero]
    small_m = [m_g_mix, m_b_forget, m_g_sgu, m_w_spatial, m_b_spatial, m_g_ffn, m_conv_b, m_g_final, zero]
    small_v = [v_g_mix, v_b_forget, v_g_sgu, v_w_spatial, v_b_spatial, v_g_ffn, v_conv_b, v_g_final, zero]
    small_grads = [jnp.stack(small_g[nm]) for nm in small[:-2]] + [d_g_final[0], loss_row[0, 0:1]]
    first = _comm_only(_gather_first([_pack(small_grads)]), name="gather_small_first")
    (gath,) = _comm_only(_gather_second(first), name="gather_small_second")
    outs = _small_adamw(gath, _pack(small_w), _pack(small_m), _pack(small_v), name="adamw_replicated")
    for nm, vals in zip(small, zip(*[_unpack(o, small_w) for o in outs])):
        res[nm] = list(vals)
    loss = res["loss"][0][0]

    order = ["g_mix", "w_in", "b_forget", "g_sgu", "w_spatial", "b_spatial", "w_branch_a", "w_branch_b", "w_out",
             "g_ffn", "w_up", "conv_w", "conv_b", "w_down", "g_final"]
    return (loss, grad_x, *[res[nm][0] for nm in order], *[res[nm][1] for nm in order],
            *[res[nm][2] for nm in order], *[res[nm][3] for nm in order])
```
